```python
import math
import jax, jax.numpy as jnp
from jax import lax
import numpy as np

D_MODEL = 2048
BATCH = 8
SEQ = 2048
DEPTH = 2

D_MIX = D_MODEL
CONV_WIDTH = D_MIX // 4
CONV_GROUPS = 4
CONV_GROUP_DIM = CONV_WIDTH // CONV_GROUPS
CONV_K = 3
ATTN_HEAD_DIM = 64
ATTN_HEADS = (D_MIX // 4) // ATTN_HEAD_DIM
ATTN_KV_HEADS = 2
ATTN_WIDTH = ATTN_HEADS * ATTN_HEAD_DIM
ATTN_KV_WIDTH = ATTN_KV_HEADS * ATTN_HEAD_DIM
WINDOW = 128
ATTN_BLOCK = 128
GDN_HEAD_DIM = 128
GDN_WIDTH = D_MIX - CONV_WIDTH - ATTN_WIDTH
GDN_HEADS = GDN_WIDTH // GDN_HEAD_DIM
GDN_CONV_K = 4
GDN_CHUNK = 64
D_FF = 4 * D_MODEL
EPS = 1e-6

IN_SPLITS = (CONV_WIDTH, CONV_WIDTH, CONV_WIDTH,
             ATTN_WIDTH, ATTN_KV_WIDTH, ATTN_KV_WIDTH,
             GDN_WIDTH, GDN_WIDTH, GDN_WIDTH, GDN_WIDTH,
             GDN_HEADS, GDN_HEADS)
D_IN = sum(IN_SPLITS)

kernel_name = "hybrid_parallel_heads_conv_swa_gdn"


def rms_norm(x, g):
    xf = x.astype(jnp.float32)
    y = xf * lax.rsqrt(jnp.mean(xf * xf, axis=-1, keepdims=True) + EPS)
    return (y * g.astype(jnp.float32)).astype(x.dtype)


def l2_norm(x):
    return x * lax.rsqrt(jnp.sum(x * x, axis=-1, keepdims=True) + EPS)


def causal_depthwise_conv(x, w):
    K = w.shape[0]
    S = x.shape[1]
    xp = jnp.pad(x, ((0, 0), (K - 1, 0), (0, 0)))
    y = w[0] * xp[:, 0:S]
    for k in range(1, K):
        y = y + w[k] * xp[:, k:k + S]
    return y


def short_conv_mixer(b_gate, c_gate, x_in, conv_w):
    return b_gate * causal_depthwise_conv(c_gate * x_in, conv_w)


def sliding_window_attention(q, k, v, q_g, k_g, sinks):
    B, S, H, D = q.shape
    KV = k.shape[2]
    G = H // KV
    nb = S // ATTN_BLOCK
    q = rms_norm(q, q_g)
    k = rms_norm(k, k_g)
    qb = q.reshape(B, nb, ATTN_BLOCK, KV, G, D)
    pad = ((0, 0), (ATTN_BLOCK, 0), (0, 0), (0, 0))
    kp = jnp.pad(k, pad).reshape(B, nb + 1, ATTN_BLOCK, KV, D)
    vp = jnp.pad(v, pad).reshape(B, nb + 1, ATTN_BLOCK, KV, D)
    kb = jnp.concatenate([kp[:, :-1], kp[:, 1:]], axis=2)
    vb = jnp.concatenate([vp[:, :-1], vp[:, 1:]], axis=2)
    s = jnp.einsum('bnqkgd,bnskd->bnkgqs', qb, kb,
                   preferred_element_type=jnp.float32) * (D ** -0.5)
    qi = jnp.arange(ATTN_BLOCK)[:, None]
    si = jnp.arange(2 * ATTN_BLOCK)[None, :]
    rel = qi + ATTN_BLOCK - si
    band = (rel >= 0) & (rel < WINDOW)
    has_prev = (jnp.arange(nb) > 0)[:, None, None]
    valid = band[None] & (has_prev | (si >= ATTN_BLOCK)[None])
    s = jnp.where(valid[None, :, None, None], s, -jnp.inf)
    sink = sinks.astype(jnp.float32).reshape(KV, G)[None, None, :, :, None, None]
    m = jnp.maximum(jnp.max(s, axis=-1, keepdims=True), sink)
    p = jnp.exp(s - m)
    denom = jnp.sum(p, axis=-1, keepdims=True) + jnp.exp(sink - m)
    o = jnp.einsum('bnkgqs,bnskd->bnqkgd', p / denom, vb.astype(jnp.float32))
    return o.reshape(B, S, H, D).astype(q.dtype)


def gated_delta_rule(q, k, v, beta_logit, a_logit, A_log, dt_bias):
    dtype = v.dtype
    B, S, H, Dk = q.shape
    Dv = v.shape[-1]
    C = GDN_CHUNK
    nc = S // C
    f32 = jnp.float32
    q = l2_norm(q.astype(f32)) * (Dk ** -0.5)
    k = l2_norm(k.astype(f32))
    v = v.astype(f32)
    beta = jax.nn.sigmoid(beta_logit.astype(f32))
    g = -jnp.exp(A_log.astype(f32)) * jax.nn.softplus(a_logit.astype(f32) + dt_bias.astype(f32))

    def chunk4(t):
        return jnp.transpose(t.reshape(B, nc, C, H, t.shape[-1]), (0, 3, 1, 2, 4))

    def chunk3(t):
        return jnp.transpose(t.reshape(B, nc, C, H), (0, 3, 1, 2))

    qc, kc, vc = chunk4(q), chunk4(k), chunk4(v)
    bc, gc = chunk3(beta), jnp.cumsum(chunk3(g), axis=-1)
    tril = jnp.tril(jnp.ones((C, C), dtype=bool))
    strict = jnp.tril(jnp.ones((C, C), dtype=bool), -1)
    diff = gc[..., :, None] - gc[..., None, :]
    decay = jnp.where(tril, jnp.exp(jnp.where(tril, diff, 0.0)), 0.0)
    k_beta = kc * bc[..., None]
    A = jnp.where(strict, jnp.einsum('bhncd,bhnsd->bhncs', k_beta, kc) * decay, 0.0)
    T = jnp.eye(C, dtype=f32) + A
    w = lax.linalg.triangular_solve(T, k_beta * jnp.exp(gc)[..., None],
                                    left_side=True, lower=True, unit_diagonal=True)
    u = lax.linalg.triangular_solve(T, vc * bc[..., None],
                                    left_side=True, lower=True, unit_diagonal=True)
    qk = jnp.where(tril, jnp.einsum('bhncd,bhnsd->bhncs', qc, kc) * decay, 0.0)

    def to_scan(t):
        return jnp.moveaxis(t, 2, 0)

    def step(state, inp):
        q_i, k_i, w_i, u_i, qk_i, g_i = inp
        v_new = u_i - jnp.einsum('bhck,bhkv->bhcv', w_i, state)
        o = (jnp.einsum('bhck,bhkv->bhcv', q_i * jnp.exp(g_i)[..., None], state)
             + jnp.einsum('bhcs,bhsv->bhcv', qk_i, v_new))
        g_last = g_i[..., -1]
        state = (state * jnp.exp(g_last)[..., None, None]
                 + jnp.einsum('bhck,bhcv->bhkv', k_i * jnp.exp(g_last[..., None] - g_i)[..., None], v_new))
        return state, o

    state0 = jnp.zeros((B, H, Dk, Dv), f32)
    _, o = lax.scan(step, state0, (to_scan(qc), to_scan(kc), to_scan(w), to_scan(u),
                                   to_scan(qk), to_scan(gc)))
    o = jnp.transpose(o, (1, 0, 3, 2, 4)).reshape(B, S, H, Dv)
    return o.astype(dtype)


def hybrid_layer(x, norm1_g, w_in, conv_w, conv_out_g, q_norm_g, k_norm_g, attn_sinks, attn_out_g,
                 gdn_conv_w, gdn_A_log, gdn_dt_bias, gdn_norm_g, w_out, norm2_g, w_up, w_down):
    B, S, _ = x.shape
    h = rms_norm(x, norm1_g)
    proj = jnp.einsum('bsd,de->bse', h, w_in)
    split_points = np.cumsum(IN_SPLITS)[:-1].tolist()
    (cb, cc, cx, aq, ak, av, gq, gk, gv, gz, gb, ga) = jnp.split(proj, split_points, axis=-1)

    yc = short_conv_mixer(cb, cc, cx, conv_w)
    yc = rms_norm(yc.reshape(B, S, CONV_GROUPS, CONV_GROUP_DIM),
                  conv_out_g.reshape(CONV_GROUPS, CONV_GROUP_DIM)).reshape(B, S, CONV_WIDTH)

    ya = sliding_window_attention(aq.reshape(B, S, ATTN_HEADS, ATTN_HEAD_DIM),
                                  ak.reshape(B, S, ATTN_KV_HEADS, ATTN_HEAD_DIM),
                                  av.reshape(B, S, ATTN_KV_HEADS, ATTN_HEAD_DIM),
                                  q_norm_g, k_norm_g, attn_sinks)
    ya = rms_norm(ya, attn_out_g.reshape(ATTN_HEADS, ATTN_HEAD_DIM)).reshape(B, S, ATTN_WIDTH)

    qkv = jax.nn.silu(causal_depthwise_conv(jnp.concatenate([gq, gk, gv], axis=-1), gdn_conv_w))
    q_d, k_d, v_d = jnp.split(qkv, 3, axis=-1)
    hs = (B, S, GDN_HEADS, GDN_HEAD_DIM)
    o = gated_delta_rule(q_d.reshape(hs), k_d.reshape(hs), v_d.reshape(hs), gb, ga,
                         gdn_A_log, gdn_dt_bias)
    yg = (rms_norm(o, gdn_norm_g) * jax.nn.silu(gz.reshape(hs))).reshape(B, S, GDN_WIDTH)

    mix = jnp.concatenate([yc, ya, yg], axis=-1)
    x = x + jnp.einsum('bse,ed->bsd', mix, w_out)

    h2 = rms_norm(x, norm2_g)
    hid = jnp.square(jax.nn.relu(jnp.einsum('bsd,df->bsf', h2, w_up)))
    return x + jnp.einsum('bsf,fd->bsd', hid, w_down)


def setup_inputs(seed: int = 0) -> dict:
    key = jax.random.key(seed)
    ks = jax.random.split(key, 18)
    f32 = jnp.float32

    def nrm(k, shape, scale):
        return jax.random.normal(k, shape, f32) * scale

    def gain(k, shape):
        return 1.0 + 0.05 * jax.random.normal(k, shape, f32)

    dt = jnp.exp(jax.random.uniform(ks[12], (DEPTH, GDN_HEADS), f32, math.log(1e-3), math.log(1e-1)))
    return {
        "x": nrm(ks[0], (BATCH, SEQ, D_MODEL), 1.0),
        "norm1_g": gain(ks[1], (DEPTH, D_MODEL)),
        "w_in": nrm(ks[2], (DEPTH, D_MODEL, D_IN), D_MODEL ** -0.5),
        "conv_w": nrm(ks[3], (DEPTH, CONV_K, CONV_WIDTH), CONV_K ** -0.5),
        "conv_out_g": gain(ks[4], (DEPTH, CONV_WIDTH)),
        "q_norm_g": gain(ks[5], (DEPTH, ATTN_HEAD_DIM)),
        "k_norm_g": gain(ks[6], (DEPTH, ATTN_HEAD_DIM)),
        "attn_sinks": nrm(ks[7], (DEPTH, ATTN_HEADS), 1.0),
        "attn_out_g": gain(ks[8], (DEPTH, ATTN_WIDTH)),
        "gdn_conv_w": nrm(ks[9], (DEPTH, GDN_CONV_K, 3 * GDN_WIDTH), GDN_CONV_K ** -0.5),
        "gdn_A_log": jnp.log(jax.random.uniform(ks[10], (DEPTH, GDN_HEADS), f32, 1.0, 16.0)),
        "gdn_dt_bias": dt + jnp.log(-jnp.expm1(-dt)),
        "gdn_norm_g": gain(ks[11], (DEPTH, GDN_HEAD_DIM)),
        "w_out": nrm(ks[13], (DEPTH, D_MIX, D_MODEL), D_MIX ** -0.5),
        "norm2_g": gain(ks[14], (DEPTH, D_MODEL)),
        "w_up": nrm(ks[15], (DEPTH, D_MODEL, D_FF), D_MODEL ** -0.5),
        "w_down": nrm(ks[16], (DEPTH, D_FF, D_MODEL), D_FF ** -0.5),
    }


def reference(x, norm1_g, w_in, conv_w, conv_out_g, q_norm_g, k_norm_g, attn_sinks, attn_out_g,
              gdn_conv_w, gdn_A_log, gdn_dt_bias, gdn_norm_g, w_out, norm2_g, w_up, w_down):
    for l in range(DEPTH):
        x = hybrid_layer(x, norm1_g[l], w_in[l], conv_w[l], conv_out_g[l], q_norm_g[l], k_norm_g[l],
                         attn_sinks[l], attn_out_g[l], gdn_conv_w[l], gdn_A_log[l], gdn_dt_bias[l],
                         gdn_norm_g[l], w_out[l], norm2_g[l], w_up[l], w_down[l])
    return x
```

```python
import functools

import jax
import jax.numpy as jnp
from jax import lax
from jax.experimental import pallas as pl
from jax.experimental.pallas import tpu as pltpu

F32 = jnp.float32
BF16 = jnp.bfloat16
HIGHEST = lax.Precision.HIGHEST

D_MODEL = 2048
CONV_WIDTH = 512
CONV_GROUPS = 4
CONV_GROUP_DIM = 128
ATTN_HEAD_DIM = 64
ATTN_HEADS = 8
ATTN_KV_HEADS = 2
ATTN_WIDTH = 512
ATTN_KV_WIDTH = 128
WINDOW = 128
ATTN_BLOCK = 128
GDN_HEAD_DIM = 128
GDN_WIDTH = 1024
GDN_HEADS = 8
GDN_CONV_K = 4
GDN_CHUNK = 64
D_FF = 8192
EPS = 1e-6

LANES = 128
SUBLANES = 8
VMEM_LIMIT = 56 * 1024 * 1024

GATE_PAD = LANES
D_IN_PAD = 4 * GDN_WIDTH + 3 * CONV_WIDTH + ATTN_WIDTH + 2 * ATTN_KV_WIDTH + GATE_PAD
OFF_GQKV = 0
OFF_GZ = 3 * GDN_WIDTH
OFF_CB = 4 * GDN_WIDTH
OFF_CC = OFF_CB + CONV_WIDTH
OFF_CX = OFF_CC + CONV_WIDTH
OFF_AQ = OFF_CX + CONV_WIDTH
OFF_AK = OFF_AQ + ATTN_WIDTH
OFF_AV = OFF_AK + ATTN_KV_WIDTH
OFF_GATE = OFF_AV + ATTN_KV_WIDTH


def _dot(a, b, precision=None):
    return jnp.dot(a, b, preferred_element_type=F32, precision=precision)


def _dot_nt(a, b, precision=None):
    return lax.dot_general(a, b, (((1,), (1,)), ((), ())), preferred_element_type=F32, precision=precision)


def _dot_tn(a, b, precision=None):
    return lax.dot_general(a, b, (((0,), (0,)), ((), ())), preferred_element_type=F32, precision=precision)


def _sigmoid(x):
    return 1.0 / (1.0 + jnp.exp(-x))


def _silu(x):
    return x * _sigmoid(x)


def _softplus(x):
    return jnp.maximum(x, 0.0) + jnp.log(1.0 + jnp.exp(-jnp.abs(x)))


def _in_proj_kernel(x_ref, g_ref, w_ref, o_ref, h_ref):
    @pl.when(pl.program_id(1) == 0)
    def _():
        x = x_ref[...]
        ms = jnp.mean(x * x, axis=-1, keepdims=True)
        h_ref[...] = (x * lax.rsqrt(ms + EPS) * g_ref[...]).astype(BF16)

    o_ref[...] = _dot(h_ref[...], w_ref[...])


def _in_proj(x2, g, w, tm=512, tn=2176):
    m = x2.shape[0]
    n = w.shape[1]
    return pl.pallas_call(
        _in_proj_kernel,
        grid=(m // tm, n // tn),
        in_specs=[
            pl.BlockSpec((tm, D_MODEL), lambda i, j: (i, 0)),
            pl.BlockSpec((1, D_MODEL), lambda i, j: (0, 0)),
            pl.BlockSpec((D_MODEL, tn), lambda i, j: (0, j)),
        ],
        out_specs=pl.BlockSpec((tm, tn), lambda i, j: (i, j)),
        out_shape=jax.ShapeDtypeStruct((m, n), F32),
        scratch_shapes=[pltpu.VMEM((tm, D_MODEL), BF16)],
        compiler_params=pltpu.CompilerParams(
            dimension_semantics=("parallel", "arbitrary"), vmem_limit_bytes=VMEM_LIMIT),
        name="in_proj",
    )(x2, g, w)


def _conv_kernel(cb_ref, cc_ref, cx_ref, w_ref, g_ref, o_ref):
    z = cc_ref[0] * cx_ref[0]
    row = lax.broadcasted_iota(jnp.int32, z.shape, 0)
    z1 = jnp.where(row >= 1, pltpu.roll(z, 1, axis=0), 0.0)
    z2 = jnp.where(row >= 2, pltpu.roll(z, 2, axis=0), 0.0)
    w = w_ref[...]
    y = w[0:1] * z2 + w[1:2] * z1 + w[2:3] * z
    y = cb_ref[0] * y
    ms = jnp.mean(y * y, axis=-1, keepdims=True)
    o_ref[0] = (y * lax.rsqrt(ms + EPS) * g_ref[...]).astype(o_ref.dtype)


def _conv_mixer(proj3, conv_w, conv_out_g):
    b, s, _ = proj3.shape
    gd = CONV_GROUP_DIM

    def col(off):
        return lambda i, g: (i, 0, off // gd + g)

    return pl.pallas_call(
        _conv_kernel,
        grid=(b, CONV_GROUPS),
        in_specs=[
            pl.BlockSpec((1, s, gd), col(OFF_CB)),
            pl.BlockSpec((1, s, gd), col(OFF_CC)),
            pl.BlockSpec((1, s, gd), col(OFF_CX)),
            pl.BlockSpec((3, gd), lambda i, g: (0, g)),
            pl.BlockSpec((1, gd), lambda i, g: (0, g)),
        ],
        out_specs=pl.BlockSpec((1, s, gd), lambda i, g: (i, 0, g)),
        out_shape=jax.ShapeDtypeStruct((b, s, CONV_WIDTH), BF16),
        compiler_params=pltpu.CompilerParams(
            dimension_semantics=("parallel", "parallel"), vmem_limit_bytes=VMEM_LIMIT),
        name="conv_mixer",
    )(proj3, proj3, proj3, conv_w, conv_out_g)


def _swa_kernel(q_ref, k_ref, v_ref, qg_ref, kg_ref, sink_ref, og_ref, o_ref, kp_ref, vp_ref):
    s_len = q_ref.shape[1]
    blk = ATTN_BLOCK
    hd = ATTN_HEAD_DIM
    group = ATTN_HEADS // ATTN_KV_HEADS

    k = k_ref[0]
    k2 = k * k
    lane = lax.broadcasted_iota(jnp.int32, k.shape, 1)
    ms0 = jnp.sum(k2[:, :hd], axis=-1, keepdims=True) * (1.0 / hd)
    ms1 = jnp.sum(k2[:, hd:], axis=-1, keepdims=True) * (1.0 / hd)
    r = jnp.where(lane < hd, lax.rsqrt(ms0 + EPS), lax.rsqrt(ms1 + EPS))
    kp_ref[0:blk, :] = jnp.zeros((blk, k.shape[1]), BF16)
    vp_ref[0:blk, :] = jnp.zeros((blk, k.shape[1]), BF16)
    kp_ref[blk:, :] = (k * r * kg_ref[...]).astype(BF16)
    vp_ref[blk:, :] = v_ref[0].astype(BF16)

    qi = lax.broadcasted_iota(jnp.int32, (blk, 2 * blk), 0)
    si = lax.broadcasted_iota(jnp.int32, (blk, 2 * blk), 1)
    rel = qi + blk - si
    band = (rel >= 0) & (rel < WINDOW)
    cur = si >= blk

    def body(n, carry):
        r0 = pl.multiple_of(n * blk, blk)
        q = q_ref[0, pl.ds(r0, blk), :]
        kb = kp_ref[pl.ds(r0, 2 * blk), :]
        vb = vp_ref[pl.ds(r0, 2 * blk), :]
        valid = band & (cur | (n > 0))
        outs = []
        for h in range(ATTN_HEADS):
            j = h // group
            qh = q[:, h * hd:(h + 1) * hd]
            ms = jnp.mean(qh * qh, axis=-1, keepdims=True)
            qn = (qh * lax.rsqrt(ms + EPS) * qg_ref[...] * (hd ** -0.5)).astype(BF16)
            s = _dot_nt(qn, kb[:, j * hd:(j + 1) * hd])
            s = jnp.where(valid, s, -jnp.inf)
            sink = sink_ref[h]
            m = jnp.maximum(jnp.max(s, axis=-1, keepdims=True), sink)
            p = jnp.exp(s - m)
            denom = jnp.sum(p, axis=-1, keepdims=True) + jnp.exp(sink - m)
            o = _dot(p.astype(BF16), vb[:, j * hd:(j + 1) * hd]) / denom
            oms = jnp.mean(o * o, axis=-1, keepdims=True)
            outs.append(o * lax.rsqrt(oms + EPS) * og_ref[:, h * hd:(h + 1) * hd])
        o_ref[0, pl.ds(r0, blk), :] = jnp.concatenate(outs, axis=-1).astype(o_ref.dtype)
        return carry

    lax.fori_loop(0, s_len // blk, body, 0)


def _swa(proj3, q_g, k_g2, sinks, out_g):
    b, s, _ = proj3.shape
    return pl.pallas_call(
        _swa_kernel,
        grid=(b,),
        in_specs=[
            pl.BlockSpec((1, s, ATTN_WIDTH), lambda i: (i, 0, OFF_AQ // ATTN_WIDTH)),
            pl.BlockSpec((1, s, ATTN_KV_WIDTH), lambda i: (i, 0, OFF_AK // ATTN_KV_WIDTH)),
            pl.BlockSpec((1, s, ATTN_KV_WIDTH), lambda i: (i, 0, OFF_AV // ATTN_KV_WIDTH)),
            pl.BlockSpec((1, ATTN_HEAD_DIM), lambda i: (0, 0)),
            pl.BlockSpec((1, ATTN_KV_WIDTH), lambda i: (0, 0)),
            pl.BlockSpec(memory_space=pltpu.SMEM),
            pl.BlockSpec((1, ATTN_WIDTH), lambda i: (0, 0)),
        ],
        out_specs=pl.BlockSpec((1, s, ATTN_WIDTH), lambda i: (i, 0, 0)),
        out_shape=jax.ShapeDtypeStruct((b, s, ATTN_WIDTH), BF16),
        scratch_shapes=[pltpu.VMEM((s + ATTN_BLOCK, ATTN_KV_WIDTH), BF16),
                        pltpu.VMEM((s + ATTN_BLOCK, ATTN_KV_WIDTH), BF16)],
        compiler_params=pltpu.CompilerParams(
            dimension_semantics=("parallel",), vmem_limit_bytes=VMEM_LIMIT),
        name="swa",
    )(proj3, proj3, proj3, q_g, k_g2, sinks, out_g)


def _inv_unit_lower(a, eye):
    x = eye - a
    p = a
    for _ in range(5):
        p = _dot(p, p, HIGHEST)
        x = x + _dot(x, p, HIGHEST)
    return x


def _gdn_kernel(qkv_ref, z_ref, gate_ref, cw_ref, alog_ref, dtb_ref, ng_ref, o_ref, xpad_ref, state_ref):
    ts = qkv_ref.shape[1]
    c = GDN_CHUNK
    hd = GDN_HEAD_DIM
    halo = SUBLANES
    t = pl.program_id(1)

    @pl.when(t == 0)
    def _():
        state_ref[...] = jnp.zeros(state_ref.shape, F32)
        xpad_ref[0:halo, :] = jnp.zeros((halo, xpad_ref.shape[1]), F32)

    @pl.when(t > 0)
    def _():
        xpad_ref[0:halo, :] = xpad_ref[ts:ts + halo, :]

    xpad_ref[halo:, :] = qkv_ref[0]

    row = lax.broadcasted_iota(jnp.int32, (c, c), 0)
    col = lax.broadcasted_iota(jnp.int32, (c, c), 1)
    tril = row >= col
    strict = row > col
    eye = (row == col).astype(F32)
    strict_f = strict.astype(F32)
    sel = (lax.broadcasted_iota(jnp.int32, (2 * GDN_HEADS, LANES), 0)
           == lax.broadcasted_iota(jnp.int32, (2 * GDN_HEADS, LANES), 1)).astype(F32)
    neg_a = -jnp.exp(alog_ref[:, :c])
    dtb = dtb_ref[:, :c]

    def conv_silu(win, w):
        y = w[3:4] * win[halo:]
        for k in range(1, GDN_CONV_K):
            y = y + w[3 - k:4 - k] * pltpu.roll(win, k, axis=0)[halo:]
        return _silu(y)

    def chunk(ci, carry):
        t0 = pl.multiple_of(ci * c, c)
        gt = _dot_nt(sel, gate_ref[0, pl.ds(t0, c), :], HIGHEST)
        beta_r = _sigmoid(gt[0:GDN_HEADS])
        g_r = neg_a * _softplus(gt[GDN_HEADS:] + dtb)
        for h in range(GDN_HEADS):
            def head_cols(base):
                lo = base + h * hd
                return xpad_ref[pl.ds(t0, c + halo), lo:lo + hd], cw_ref[:, lo:lo + hd]

            q = conv_silu(*head_cols(0))
            k = conv_silu(*head_cols(GDN_WIDTH))
            v = conv_silu(*head_cols(2 * GDN_WIDTH))
            q = q * lax.rsqrt(jnp.sum(q * q, axis=-1, keepdims=True) + EPS) * (hd ** -0.5)
            k = k * lax.rsqrt(jnp.sum(k * k, axis=-1, keepdims=True) + EPS)

            g_row = g_r[h:h + 1, :]
            beta_row = beta_r[h:h + 1, :]
            m1 = jnp.where(tril, g_row, 0.0)
            gc = jnp.sum(m1, axis=-1, keepdims=True)
            diff = _dot(m1, strict_f, HIGHEST)
            beta = jnp.sum(eye * beta_row, axis=-1, keepdims=True)
            g_last = jnp.sum(g_row, axis=-1, keepdims=True)
            decay = jnp.exp(jnp.where(tril, diff, 0.0))
            kb = k.astype(BF16)
            a = jnp.where(strict, _dot_nt(kb, kb) * beta * decay, 0.0)
            qk = jnp.where(tril, _dot_nt(q.astype(BF16), kb) * decay, 0.0)
            tinv_b = _inv_unit_lower(a, eye) * beta_row
            egc = jnp.exp(gc)
            kg = k * egc
            wu = _dot(tinv_b, jnp.concatenate([kg, v], axis=-1), HIGHEST)
            w = wu[:, :hd]
            u = wu[:, hd:]

            s = state_ref[h]
            sb = s.astype(BF16)
            lhs = jnp.concatenate([w, q * egc], axis=0).astype(BF16)
            ws_qs = _dot(lhs, sb)
            v_new = u - ws_qs[:c]
            vb = v_new.astype(BF16)
            o = ws_qs[c:] + _dot(qk.astype(BF16), vb)
            kd = (k * jnp.exp(g_last - gc)).astype(BF16)
            state_ref[h] = s * jnp.exp(g_last) + _dot_tn(kd, vb)

            oms = jnp.mean(o * o, axis=-1, keepdims=True)
            zh = z_ref[0, pl.ds(t0, c), h * hd:(h + 1) * hd]
            y = o * lax.rsqrt(oms + EPS) * ng_ref[...] * _silu(zh)
            o_ref[0, pl.ds(t0, c), h * hd:(h + 1) * hd] = y.astype(o_ref.dtype)
        return carry

    lax.fori_loop(0, ts // c, chunk, 0)


def _gdn(proj3, conv_w, alog_b, dtb_b, norm_g, ts=256):
    b, s, _ = proj3.shape
    qkv_w = 3 * GDN_WIDTH
    return pl.pallas_call(
        _gdn_kernel,
        grid=(b, s // ts),
        in_specs=[
            pl.BlockSpec((1, ts, qkv_w), lambda i, t: (i, t, OFF_GQKV // qkv_w)),
            pl.BlockSpec((1, ts, GDN_WIDTH), lambda i, t: (i, t, OFF_GZ // GDN_WIDTH)),
            pl.BlockSpec((1, ts, GATE_PAD), lambda i, t: (i, t, OFF_GATE // GATE_PAD)),
            pl.BlockSpec((GDN_CONV_K, qkv_w), lambda i, t: (0, 0)),
            pl.BlockSpec((GDN_HEADS, LANES), lambda i, t: (0, 0)),
            pl.BlockSpec((GDN_HEADS, LANES), lambda i, t: (0, 0)),
            pl.BlockSpec((1, GDN_HEAD_DIM), lambda i, t: (0, 0)),
        ],
        out_specs=pl.BlockSpec((1, ts, GDN_WIDTH), lambda i, t: (i, t, 0)),
        out_shape=jax.ShapeDtypeStruct((b, s, GDN_WIDTH), BF16),
        scratch_shapes=[pltpu.VMEM((ts + SUBLANES, qkv_w), F32),
                        pltpu.VMEM((GDN_HEADS, GDN_HEAD_DIM, GDN_HEAD_DIM), F32)],
        compiler_params=pltpu.CompilerParams(
            dimension_semantics=("parallel", "arbitrary"), vmem_limit_bytes=VMEM_LIMIT),
        name="gdn",
    )(proj3, proj3, proj3, conv_w, alog_b, dtb_b, norm_g)


def _out_proj_kernel(x_ref, yc_ref, ya_ref, yg_ref, w_ref, g_ref, x1_ref, h_ref):
    acc = _dot(yc_ref[...], w_ref[0:CONV_WIDTH, :])
    acc = acc + _dot(ya_ref[...], w_ref[CONV_WIDTH:CONV_WIDTH + ATTN_WIDTH, :])
    acc = acc + _dot(yg_ref[...], w_ref[CONV_WIDTH + ATTN_WIDTH:, :])
    x1 = x_ref[...] + acc
    x1_ref[...] = x1
    ms = jnp.mean(x1 * x1, axis=-1, keepdims=True)
    h_ref[...] = (x1 * lax.rsqrt(ms + EPS) * g_ref[...]).astype(BF16)


def _out_proj(x2, yc, ya, yg, w, g, tm=512):
    m = x2.shape[0]
    return pl.pallas_call(
        _out_proj_kernel,
        grid=(m // tm,),
        in_specs=[
            pl.BlockSpec((tm, D_MODEL), lambda i: (i, 0)),
            pl.BlockSpec((tm, CONV_WIDTH), lambda i: (i, 0)),
            pl.BlockSpec((tm, ATTN_WIDTH), lambda i: (i, 0)),
            pl.BlockSpec((tm, GDN_WIDTH), lambda i: (i, 0)),
            pl.BlockSpec((D_MODEL, D_MODEL), lambda i: (0, 0)),
            pl.BlockSpec((1, D_MODEL), lambda i: (0, 0)),
        ],
        out_specs=[pl.BlockSpec((tm, D_MODEL), lambda i: (i, 0)),
                   pl.BlockSpec((tm, D_MODEL), lambda i: (i, 0))],
        out_shape=[jax.ShapeDtypeStruct((m, D_MODEL), F32),
                   jax.ShapeDtypeStruct((m, D_MODEL), BF16)],
        compiler_params=pltpu.CompilerParams(
            dimension_semantics=("parallel",), vmem_limit_bytes=VMEM_LIMIT),
        name="out_proj",
    )(x2, yc, ya, yg, w, g)


def _mlp_kernel(x1_ref, h_ref, wu_ref, wd_ref, o_ref):
    @pl.when(pl.program_id(1) == 0)
    def _():
        o_ref[...] = x1_ref[...]

    hid = jnp.maximum(_dot(h_ref[...], wu_ref[...]), 0.0)
    hid = (hid * hid).astype(BF16)
    o_ref[...] += _dot(hid, wd_ref[...])


def _mlp(x1, h, w_up, w_down, tm=512, tf=1024):
    m = x1.shape[0]
    return pl.pallas_call(
        _mlp_kernel,
        grid=(m // tm, D_FF // tf),
        in_specs=[
            pl.BlockSpec((tm, D_MODEL), lambda i, f: (i, 0)),
            pl.BlockSpec((tm, D_MODEL), lambda i, f: (i, 0)),
            pl.BlockSpec((D_MODEL, tf), lambda i, f: (0, f)),
            pl.BlockSpec((tf, D_MODEL), lambda i, f: (f, 0)),
        ],
        out_specs=pl.BlockSpec((tm, D_MODEL), lambda i, f: (i, 0)),
        out_shape=jax.ShapeDtypeStruct((m, D_MODEL), F32),
        compiler_params=pltpu.CompilerParams(
            dimension_semantics=("parallel", "arbitrary"), vmem_limit_bytes=VMEM_LIMIT),
        name="mlp",
    )(x1, h, w_up, w_down)


def _regroup_w_in(w):
    n_conv = 3 * CONV_WIDTH
    n_attn = ATTN_WIDTH + 2 * ATTN_KV_WIDTH
    n_gdn = 4 * GDN_WIDTH
    conv = w[:, :n_conv]
    attn = w[:, n_conv:n_conv + n_attn]
    gdn = w[:, n_conv + n_attn:n_conv + n_attn + n_gdn]
    gate = w[:, n_conv + n_attn + n_gdn:]
    pad = jnp.zeros((w.shape[0], GATE_PAD - gate.shape[1]), w.dtype)
    return jnp.concatenate([gdn, conv, attn, gate, pad], axis=1).astype(BF16)


def _layer(x2, b, s, norm1_g, w_in, conv_w, conv_out_g, q_norm_g, k_norm_g, attn_sinks, attn_out_g,
           gdn_conv_w, gdn_A_log, gdn_dt_bias, gdn_norm_g, w_out, norm2_g, w_up, w_down):
    proj = _in_proj(x2, norm1_g[None, :], _regroup_w_in(w_in))
    proj3 = proj.reshape(b, s, D_IN_PAD)
    yc = _conv_mixer(proj3, conv_w, conv_out_g[None, :])
    ya = _swa(proj3, q_norm_g[None, :], jnp.tile(k_norm_g, ATTN_KV_HEADS)[None, :], attn_sinks,
              attn_out_g[None, :])
    yg = _gdn(proj3, gdn_conv_w,
              jnp.broadcast_to(gdn_A_log[:, None], (GDN_HEADS, LANES)),
              jnp.broadcast_to(gdn_dt_bias[:, None], (GDN_HEADS, LANES)),
              gdn_norm_g[None, :])
    m = b * s
    x1, h2 = _out_proj(x2, yc.reshape(m, CONV_WIDTH), ya.reshape(m, ATTN_WIDTH), yg.reshape(m, GDN_WIDTH),
                       w_out.astype(BF16), norm2_g[None, :])
    return _mlp(x1, h2, w_up.astype(BF16), w_down.astype(BF16))


def kernel(x, norm1_g, w_in, conv_w, conv_out_g, q_norm_g, k_norm_g, attn_sinks, attn_out_g, gdn_conv_w,
           gdn_A_log, gdn_dt_bias, gdn_norm_g, w_out, norm2_g, w_up, w_down):
    b, s, d = x.shape
    x2 = x.reshape(b * s, d)
    for l in range(norm1_g.shape[0]):
        x2 = _layer(x2, b, s, norm1_g[l], w_in[l], conv_w[l], conv_out_g[l], q_norm_g[l], k_norm_g[l],
                    attn_sinks[l], attn_out_g[l], gdn_conv_w[l], gdn_A_log[l], gdn_dt_bias[l],
                    gdn_norm_g[l], w_out[l], norm2_g[l], w_up[l], w_down[l])
    return x2.reshape(b, s, d)
```

```python
import functools

import jax
import jax.numpy as jnp
from jax import lax
from jax.experimental import pallas as pl
from jax.experimental.pallas import tpu as pltpu

F32 = jnp.float32
BF16 = jnp.bfloat16
HIGHEST = lax.Precision.HIGHEST

D_MODEL = 2048
CONV_WIDTH = 512
CONV_GROUPS = 4
CONV_GROUP_DIM = 128
ATTN_HEAD_DIM = 64
ATTN_HEADS = 8
ATTN_KV_HEADS = 2
ATTN_WIDTH = 512
ATTN_KV_WIDTH = 128
WINDOW = 128
ATTN_BLOCK = 128
GDN_HEAD_DIM = 128
GDN_WIDTH = 1024
GDN_HEADS = 8
GDN_CONV_K = 4
GDN_CHUNK = 64
D_FF = 8192
EPS = 1e-6

LANES = 128
SUBLANES = 8
VMEM_LIMIT = 56 * 1024 * 1024

GATE_PAD = LANES
D_IN_PAD = 4 * GDN_WIDTH + 3 * CONV_WIDTH + ATTN_WIDTH + 2 * ATTN_KV_WIDTH + GATE_PAD
OFF_GQKV = 0
OFF_GZ = 3 * GDN_WIDTH
OFF_CB = 4 * GDN_WIDTH
OFF_CC = OFF_CB + CONV_WIDTH
OFF_CX = OFF_CC + CONV_WIDTH
OFF_AQ = OFF_CX + CONV_WIDTH
OFF_AK = OFF_AQ + ATTN_WIDTH
OFF_AV = OFF_AK + ATTN_KV_WIDTH
OFF_GATE = OFF_AV + ATTN_KV_WIDTH


def _dot(a, b, precision=None):
    return jnp.dot(a, b, preferred_element_type=F32, precision=precision)


def _dot_nt(a, b, precision=None):
    return lax.dot_general(a, b, (((1,), (1,)), ((), ())), preferred_element_type=F32, precision=precision)


def _dot_tn(a, b, precision=None):
    return lax.dot_general(a, b, (((0,), (0,)), ((), ())), preferred_element_type=F32, precision=precision)


def _sigmoid(x):
    return 1.0 / (1.0 + jnp.exp(-x))


def _silu(x):
    return x * _sigmoid(x)


def _softplus(x):
    return jnp.maximum(x, 0.0) + jnp.log(1.0 + jnp.exp(-jnp.abs(x)))


def _in_proj_kernel(x_ref, g_ref, w_ref, o_ref, h_ref):
    @pl.when(pl.program_id(1) == 0)
    def _():
        x = x_ref[...]
        ms = jnp.mean(x * x, axis=-1, keepdims=True)
        h_ref[...] = (x * lax.rsqrt(ms + EPS) * g_ref[...]).astype(BF16)

    o_ref[...] = _dot(h_ref[...], w_ref[...])


def _in_proj(x2, g, w, tm=512, tn=2176):
    m = x2.shape[0]
    n = w.shape[1]
    return pl.pallas_call(
        _in_proj_kernel,
        grid=(m // tm, n // tn),
        in_specs=[
            pl.BlockSpec((tm, D_MODEL), lambda i, j: (i, 0)),
            pl.BlockSpec((1, D_MODEL), lambda i, j: (0, 0)),
            pl.BlockSpec((D_MODEL, tn), lambda i, j: (0, j)),
        ],
        out_specs=pl.BlockSpec((tm, tn), lambda i, j: (i, j)),
        out_shape=jax.ShapeDtypeStruct((m, n), F32),
        scratch_shapes=[pltpu.VMEM((tm, D_MODEL), BF16)],
        compiler_params=pltpu.CompilerParams(
            dimension_semantics=("parallel", "arbitrary"), vmem_limit_bytes=VMEM_LIMIT),
        name="in_proj",
    )(x2, g, w)


def _conv_kernel(cb_ref, cc_ref, cx_ref, w_ref, g_ref, o_ref):
    z = cc_ref[0] * cx_ref[0]
    row = lax.broadcasted_iota(jnp.int32, z.shape, 0)
    z1 = jnp.where(row >= 1, pltpu.roll(z, 1, axis=0), 0.0)
    z2 = jnp.where(row >= 2, pltpu.roll(z, 2, axis=0), 0.0)
    w = w_ref[...]
    y = w[0:1] * z2 + w[1:2] * z1 + w[2:3] * z
    y = cb_ref[0] * y
    ms = jnp.mean(y * y, axis=-1, keepdims=True)
    o_ref[0] = (y * lax.rsqrt(ms + EPS) * g_ref[...]).astype(o_ref.dtype)


def _conv_mixer(proj3, conv_w, conv_out_g):
    b, s, _ = proj3.shape
    gd = CONV_GROUP_DIM

    def col(off):
        return lambda i, g: (i, 0, off // gd + g)

    return pl.pallas_call(
        _conv_kernel,
        grid=(b, CONV_GROUPS),
        in_specs=[
            pl.BlockSpec((1, s, gd), col(OFF_CB)),
            pl.BlockSpec((1, s, gd), col(OFF_CC)),
            pl.BlockSpec((1, s, gd), col(OFF_CX)),
            pl.BlockSpec((3, gd), lambda i, g: (0, g)),
            pl.BlockSpec((1, gd), lambda i, g: (0, g)),
        ],
        out_specs=pl.BlockSpec((1, s, gd), lambda i, g: (i, 0, g)),
        out_shape=jax.ShapeDtypeStruct((b, s, CONV_WIDTH), BF16),
        compiler_params=pltpu.CompilerParams(
            dimension_semantics=("parallel", "parallel"), vmem_limit_bytes=VMEM_LIMIT),
        name="conv_mixer",
    )(proj3, proj3, proj3, conv_w, conv_out_g)


def _swa_kernel(q_ref, k_ref, v_ref, qg_ref, kg_ref, sink_ref, og_ref, o_ref, kp_ref, vp_ref):
    s_len = q_ref.shape[1]
    blk = ATTN_BLOCK
    hd = ATTN_HEAD_DIM
    group = ATTN_HEADS // ATTN_KV_HEADS

    k = k_ref[0]
    k2 = k * k
    lane = lax.broadcasted_iota(jnp.int32, k.shape, 1)
    ms0 = jnp.sum(k2[:, :hd], axis=-1, keepdims=True) * (1.0 / hd)
    ms1 = jnp.sum(k2[:, hd:], axis=-1, keepdims=True) * (1.0 / hd)
    r = jnp.where(lane < hd, lax.rsqrt(ms0 + EPS), lax.rsqrt(ms1 + EPS))
    kp_ref[0:blk, :] = jnp.zeros((blk, k.shape[1]), BF16)
    vp_ref[0:blk, :] = jnp.zeros((blk, k.shape[1]), BF16)
    kp_ref[blk:, :] = (k * r * kg_ref[...]).astype(BF16)
    vp_ref[blk:, :] = v_ref[0].astype(BF16)

    qi = lax.broadcasted_iota(jnp.int32, (blk, 2 * blk), 0)
    si = lax.broadcasted_iota(jnp.int32, (blk, 2 * blk), 1)
    rel = qi + blk - si
    band = (rel >= 0) & (rel < WINDOW)
    cur = si >= blk

    def body(n, carry):
        r0 = pl.multiple_of(n * blk, blk)
        q = q_ref[0, pl.ds(r0, blk), :]
        kb = kp_ref[pl.ds(r0, 2 * blk), :]
        vb = vp_ref[pl.ds(r0, 2 * blk), :]
        valid = band & (cur | (n > 0))
        outs = []
        for h in range(ATTN_HEADS):
            j = h // group
            qh = q[:, h * hd:(h + 1) * hd]
            ms = jnp.mean(qh * qh, axis=-1, keepdims=True)
            qn = (qh * lax.rsqrt(ms + EPS) * qg_ref[...] * (hd ** -0.5)).astype(BF16)
            s = _dot_nt(qn, kb[:, j * hd:(j + 1) * hd])
            s = jnp.where(valid, s, -jnp.inf)
            sink = sink_ref[h]
            m = jnp.maximum(jnp.max(s, axis=-1, keepdims=True), sink)
            p = jnp.exp(s - m)
            denom = jnp.sum(p, axis=-1, keepdims=True) + jnp.exp(sink - m)
            o = _dot(p.astype(BF16), vb[:, j * hd:(j + 1) * hd]) / denom
            oms = jnp.mean(o * o, axis=-1, keepdims=True)
            outs.append(o * lax.rsqrt(oms + EPS) * og_ref[:, h * hd:(h + 1) * hd])
        o_ref[0, pl.ds(r0, blk), :] = jnp.concatenate(outs, axis=-1).astype(o_ref.dtype)
        return carry

    lax.fori_loop(0, s_len // blk, body, 0)


def _swa(proj3, q_g, k_g2, sinks, out_g):
    b, s, _ = proj3.shape
    return pl.pallas_call(
        _swa_kernel,
        grid=(b,),
        in_specs=[
            pl.BlockSpec((1, s, ATTN_WIDTH), lambda i: (i, 0, OFF_AQ // ATTN_WIDTH)),
            pl.BlockSpec((1, s, ATTN_KV_WIDTH), lambda i: (i, 0, OFF_AK // ATTN_KV_WIDTH)),
            pl.BlockSpec((1, s, ATTN_KV_WIDTH), lambda i: (i, 0, OFF_AV // ATTN_KV_WIDTH)),
            pl.BlockSpec((1, ATTN_HEAD_DIM), lambda i: (0, 0)),
            pl.BlockSpec((1, ATTN_KV_WIDTH), lambda i: (0, 0)),
            pl.BlockSpec(memory_space=pltpu.SMEM),
            pl.BlockSpec((1, ATTN_WIDTH), lambda i: (0, 0)),
        ],
        out_specs=pl.BlockSpec((1, s, ATTN_WIDTH), lambda i: (i, 0, 0)),
        out_shape=jax.ShapeDtypeStruct((b, s, ATTN_WIDTH), BF16),
        scratch_shapes=[pltpu.VMEM((s + ATTN_BLOCK, ATTN_KV_WIDTH), BF16),
                        pltpu.VMEM((s + ATTN_BLOCK, ATTN_KV_WIDTH), BF16)],
        compiler_params=pltpu.CompilerParams(
            dimension_semantics=("parallel",), vmem_limit_bytes=VMEM_LIMIT),
        name="swa",
    )(proj3, proj3, proj3, q_g, k_g2, sinks, out_g)


def _gdn_kernel(qkv_ref, z_ref, gate_ref, cw_ref, alog_ref, dtb_ref, ng_ref, o_ref, xpad_ref, state_ref):
    ts = qkv_ref.shape[1]
    c = GDN_CHUNK
    hd = GDN_HEAD_DIM
    nh = GDN_HEADS
    halo = SUBLANES
    t = pl.program_id(1)

    @pl.when(t == 0)
    def _():
        state_ref[...] = jnp.zeros(state_ref.shape, F32)
        xpad_ref[0:halo, :] = jnp.zeros((halo, xpad_ref.shape[1]), F32)

    @pl.when(t > 0)
    def _():
        xpad_ref[0:halo, :] = xpad_ref[ts:ts + halo, :]

    xpad_ref[halo:, :] = qkv_ref[0]

    row = lax.broadcasted_iota(jnp.int32, (c, c), 0)
    col = lax.broadcasted_iota(jnp.int32, (c, c), 1)
    tril = row >= col
    strict = row > col
    eye = (row == col).astype(F32)
    tril_f = tril.astype(F32)
    sel = (lax.broadcasted_iota(jnp.int32, (2 * nh, LANES), 0)
           == lax.broadcasted_iota(jnp.int32, (2 * nh, LANES), 1)).astype(F32)
    is_beta = lax.broadcasted_iota(jnp.int32, (1, LANES), 1) < nh
    neg_a = -jnp.exp(alog_ref[...])
    dtb = dtb_ref[...]

    def chunk(ci, carry):
        t0 = pl.multiple_of(ci * c, c)

        def conv_silu(base, h):
            lo = base + h * hd
            w = cw_ref[:, lo:lo + hd]
            win = xpad_ref[pl.ds(t0, c + halo), lo:lo + hd]
            y = w[GDN_CONV_K - 1:GDN_CONV_K] * win[halo:]
            for k in range(1, GDN_CONV_K):
                y = y + w[GDN_CONV_K - 1 - k:GDN_CONV_K - k] * pltpu.roll(win, k, axis=0)[halo:]
            return _silu(y)

        gl = gate_ref[0, pl.ds(t0, c), :]
        gmix = jnp.where(is_beta, _sigmoid(gl), neg_a * _softplus(gl + dtb))
        gc_all = _dot(tril_f, gmix, HIGHEST)
        rows = _dot_nt(sel, jnp.concatenate([gmix, gc_all], axis=0), HIGHEST)

        qs, ks, vs = [], [], []
        for h in range(nh):
            q = conv_silu(0, h)
            k = conv_silu(GDN_WIDTH, h)
            vs.append(conv_silu(2 * GDN_WIDTH, h))
            qs.append(q * (lax.rsqrt(jnp.sum(q * q, axis=-1, keepdims=True) + EPS) * (hd ** -0.5)))
            ks.append(k * lax.rsqrt(jnp.sum(k * k, axis=-1, keepdims=True) + EPS))

        heads = range(nh)
        kkqk = []
        for h in heads:
            kb = ks[h].astype(BF16)
            kkqk.append(_dot_nt(jnp.concatenate([kb, qs[h].astype(BF16)], axis=0), kb))
        beta = [gmix[:, h:h + 1] for h in heads]
        gc = [gc_all[:, nh + h:nh + h + 1] for h in heads]
        g_last = [gc_all[c - 1:c, nh + h:nh + h + 1] for h in heads]
        xs, ps, qkb = [], [], []
        for h in heads:
            gc_row = rows[nh + h:nh + h + 1, c:]
            decay = jnp.exp(jnp.where(tril, gc[h] - gc_row, 0.0))
            a = jnp.where(strict, kkqk[h][:c] * beta[h] * decay, 0.0)
            qkb.append(jnp.where(tril, kkqk[h][c:] * decay, 0.0).astype(BF16))
            xs.append(eye - a)
            ps.append(a)
        ps = [_dot(ps[h].astype(BF16), ps[h].astype(BF16)) for h in heads]
        for i in range(5):
            if i < 4:
                xp = [_dot(jnp.concatenate([xs[h], ps[h]], axis=0).astype(BF16), ps[h].astype(BF16))
                      for h in heads]
                xs = [xs[h] + xp[h][:c] for h in heads]
                ps = [xp[h][c:] for h in heads]
            else:
                xp = [_dot(xs[h].astype(BF16), ps[h].astype(BF16)) for h in heads]
                xs = [xs[h] + xp[h] for h in heads]
        egc = [jnp.exp(gc[h]) for h in heads]
        wu = []
        for h in heads:
            tinv_b = (xs[h] * rows[h:h + 1, :c]).astype(BF16)
            wu.append(_dot(tinv_b, jnp.concatenate([ks[h] * egc[h], vs[h]], axis=-1).astype(BF16)))
        st = [state_ref[h] for h in heads]
        ws_qs = [_dot(jnp.concatenate([wu[h][:, :hd], qs[h] * egc[h]], axis=0).astype(BF16), st[h].astype(BF16))
                 for h in heads]
        vb = [(wu[h][:, hd:] - ws_qs[h][:c]).astype(BF16) for h in heads]
        kd = [(ks[h] * jnp.exp(g_last[h] - gc[h])).astype(BF16) for h in heads]
        os_ = [ws_qs[h][c:] + _dot(qkb[h], vb[h]) for h in heads]
        for h in heads:
            state_ref[h] = st[h] * jnp.exp(g_last[h]) + _dot_tn(kd[h], vb[h])
        for h in heads:
            o = os_[h]
            oms = jnp.mean(o * o, axis=-1, keepdims=True)
            zh = z_ref[0, pl.ds(t0, c), h * hd:(h + 1) * hd]
            y = o * lax.rsqrt(oms + EPS) * ng_ref[...] * _silu(zh)
            o_ref[0, pl.ds(t0, c), h * hd:(h + 1) * hd] = y.astype(o_ref.dtype)
        return carry

    lax.fori_loop(0, ts // c, chunk, 0)


def _gdn(proj3, conv_w, alog_b, dtb_b, norm_g, ts=256):
    b, s, _ = proj3.shape
    qkv_w = 3 * GDN_WIDTH
    return pl.pallas_call(
        _gdn_kernel,
        grid=(b, s // ts),
        in_specs=[
            pl.BlockSpec((1, ts, qkv_w), lambda i, t: (i, t, OFF_GQKV // qkv_w)),
            pl.BlockSpec((1, ts, GDN_WIDTH), lambda i, t: (i, t, OFF_GZ // GDN_WIDTH)),
            pl.BlockSpec((1, ts, GATE_PAD), lambda i, t: (i, t, OFF_GATE // GATE_PAD)),
            pl.BlockSpec((GDN_CONV_K, qkv_w), lambda i, t: (0, 0)),
            pl.BlockSpec((1, LANES), lambda i, t: (0, 0)),
            pl.BlockSpec((1, LANES), lambda i, t: (0, 0)),
            pl.BlockSpec((1, GDN_HEAD_DIM), lambda i, t: (0, 0)),
        ],
        out_specs=pl.BlockSpec((1, ts, GDN_WIDTH), lambda i, t: (i, t, 0)),
        out_shape=jax.ShapeDtypeStruct((b, s, GDN_WIDTH), BF16),
        scratch_shapes=[pltpu.VMEM((ts + SUBLANES, qkv_w), F32),
                        pltpu.VMEM((GDN_HEADS, GDN_HEAD_DIM, GDN_HEAD_DIM), F32)],
        compiler_params=pltpu.CompilerParams(
            dimension_semantics=("parallel", "arbitrary"), vmem_limit_bytes=VMEM_LIMIT),
        name="gdn",
    )(proj3, proj3, proj3, conv_w, alog_b, dtb_b, norm_g)


def _out_proj_kernel(x_ref, yc_ref, ya_ref, yg_ref, w_ref, g_ref, x1_ref, h_ref):
    acc = _dot(yc_ref[...], w_ref[0:CONV_WIDTH, :])
    acc = acc + _dot(ya_ref[...], w_ref[CONV_WIDTH:CONV_WIDTH + ATTN_WIDTH, :])
    acc = acc + _dot(yg_ref[...], w_ref[CONV_WIDTH + ATTN_WIDTH:, :])
    x1 = x_ref[...] + acc
    x1_ref[...] = x1
    ms = jnp.mean(x1 * x1, axis=-1, keepdims=True)
    h_ref[...] = (x1 * lax.rsqrt(ms + EPS) * g_ref[...]).astype(BF16)


def _out_proj(x2, yc, ya, yg, w, g, tm=512):
    m = x2.shape[0]
    return pl.pallas_call(
        _out_proj_kernel,
        grid=(m // tm,),
        in_specs=[
            pl.BlockSpec((tm, D_MODEL), lambda i: (i, 0)),
            pl.BlockSpec((tm, CONV_WIDTH), lambda i: (i, 0)),
            pl.BlockSpec((tm, ATTN_WIDTH), lambda i: (i, 0)),
            pl.BlockSpec((tm, GDN_WIDTH), lambda i: (i, 0)),
            pl.BlockSpec((D_MODEL, D_MODEL), lambda i: (0, 0)),
            pl.BlockSpec((1, D_MODEL), lambda i: (0, 0)),
        ],
        out_specs=[pl.BlockSpec((tm, D_MODEL), lambda i: (i, 0)),
                   pl.BlockSpec((tm, D_MODEL), lambda i: (i, 0))],
        out_shape=[jax.ShapeDtypeStruct((m, D_MODEL), F32),
                   jax.ShapeDtypeStruct((m, D_MODEL), BF16)],
        compiler_params=pltpu.CompilerParams(
            dimension_semantics=("parallel",), vmem_limit_bytes=VMEM_LIMIT),
        name="out_proj",
    )(x2, yc, ya, yg, w, g)


def _mlp_kernel(x1_ref, h_ref, wu_ref, wd_ref, o_ref):
    @pl.when(pl.program_id(1) == 0)
    def _():
        o_ref[...] = x1_ref[...]

    hid = jnp.maximum(_dot(h_ref[...], wu_ref[...]), 0.0)
    hid = (hid * hid).astype(BF16)
    o_ref[...] += _dot(hid, wd_ref[...])


def _mlp(x1, h, w_up, w_down, tm=512, tf=1024):
    m = x1.shape[0]
    return pl.pallas_call(
        _mlp_kernel,
        grid=(m // tm, D_FF // tf),
        in_specs=[
            pl.BlockSpec((tm, D_MODEL), lambda i, f: (i, 0)),
            pl.BlockSpec((tm, D_MODEL), lambda i, f: (i, 0)),
            pl.BlockSpec((D_MODEL, tf), lambda i, f: (0, f)),
            pl.BlockSpec((tf, D_MODEL), lambda i, f: (f, 0)),
        ],
        out_specs=pl.BlockSpec((tm, D_MODEL), lambda i, f: (i, 0)),
        out_shape=jax.ShapeDtypeStruct((m, D_MODEL), F32),
        compiler_params=pltpu.CompilerParams(
            dimension_semantics=("parallel", "arbitrary"), vmem_limit_bytes=VMEM_LIMIT),
        name="mlp",
    )(x1, h, w_up, w_down)


def _regroup_w_in(w):
    n_conv = 3 * CONV_WIDTH
    n_attn = ATTN_WIDTH + 2 * ATTN_KV_WIDTH
    n_gdn = 4 * GDN_WIDTH
    conv = w[:, :n_conv]
    attn = w[:, n_conv:n_conv + n_attn]
    gdn = w[:, n_conv + n_attn:n_conv + n_attn + n_gdn]
    gate = w[:, n_conv + n_attn + n_gdn:]
    pad = jnp.zeros((w.shape[0], GATE_PAD - gate.shape[1]), w.dtype)
    return jnp.concatenate([gdn, conv, attn, gate, pad], axis=1).astype(BF16)


def _decay_lanes(p):
    return jnp.zeros((1, LANES), F32).at[0, GDN_HEADS:2 * GDN_HEADS].set(p)


def _layer(x2, b, s, norm1_g, w_in, conv_w, conv_out_g, q_norm_g, k_norm_g, attn_sinks, attn_out_g,
           gdn_conv_w, gdn_A_log, gdn_dt_bias, gdn_norm_g, w_out, norm2_g, w_up, w_down):
    proj = _in_proj(x2, norm1_g[None, :], _regroup_w_in(w_in))
    proj3 = proj.reshape(b, s, D_IN_PAD)
    yc = _conv_mixer(proj3, conv_w, conv_out_g[None, :])
    ya = _swa(proj3, q_norm_g[None, :], jnp.tile(k_norm_g, ATTN_KV_HEADS)[None, :], attn_sinks,
              attn_out_g[None, :])
    yg = _gdn(proj3, gdn_conv_w, _decay_lanes(gdn_A_log), _decay_lanes(gdn_dt_bias), gdn_norm_g[None, :])
    m = b * s
    x1, h2 = _out_proj(x2, yc.reshape(m, CONV_WIDTH), ya.reshape(m, ATTN_WIDTH), yg.reshape(m, GDN_WIDTH),
                       w_out.astype(BF16), norm2_g[None, :])
    return _mlp(x1, h2, w_up.astype(BF16), w_down.astype(BF16))


def kernel(x, norm1_g, w_in, conv_w, conv_out_g, q_norm_g, k_norm_g, attn_sinks, attn_out_g, gdn_conv_w,
           gdn_A_log, gdn_dt_bias, gdn_norm_g, w_out, norm2_g, w_up, w_down):
    b, s, d = x.shape
    x2 = x.reshape(b * s, d)
    for l in range(norm1_g.shape[0]):
        x2 = _layer(x2, b, s, norm1_g[l], w_in[l], conv_w[l], conv_out_g[l], q_norm_g[l], k_norm_g[l],
                    attn_sinks[l], attn_out_g[l], gdn_conv_w[l], gdn_A_log[l], gdn_dt_bias[l],
                    gdn_norm_g[l], w_out[l], norm2_g[l], w_up[l], w_down[l])
    return x2.reshape(b, s, d)
```

```python
import functools

import jax
import jax.numpy as jnp
from jax import lax
from jax.experimental import pallas as pl
from jax.experimental.pallas import tpu as pltpu

F32 = jnp.float32
BF16 = jnp.bfloat16
HIGHEST = lax.Precision.HIGHEST

D_MODEL = 2048
CONV_WIDTH = 512
CONV_GROUPS = 4
CONV_GROUP_DIM = 128
ATTN_HEAD_DIM = 64
ATTN_HEADS = 8
ATTN_KV_HEADS = 2
ATTN_WIDTH = 512
ATTN_KV_WIDTH = 128
WINDOW = 128
ATTN_BLOCK = 128
GDN_HEAD_DIM = 128
GDN_WIDTH = 1024
GDN_HEADS = 8
GDN_CONV_K = 4
GDN_CHUNK = 64
D_FF = 8192
EPS = 1e-6

LANES = 128
SUBLANES = 8
VMEM_LIMIT = 56 * 1024 * 1024

GATE_PAD = LANES
IN_PROJ_TN = 13 * LANES
D_IN_PAD = 4 * IN_PROJ_TN
OFF_GQKV = 0
OFF_GZ = 3 * GDN_WIDTH
OFF_CB = 4 * GDN_WIDTH
OFF_CC = OFF_CB + CONV_WIDTH
OFF_CX = OFF_CC + CONV_WIDTH
OFF_AQ = OFF_CX + CONV_WIDTH
OFF_AK = OFF_AQ + ATTN_WIDTH
OFF_AV = OFF_AK + ATTN_KV_WIDTH
OFF_GATE = OFF_AV + ATTN_KV_WIDTH
assert OFF_GATE + GATE_PAD <= D_IN_PAD


def _dot(a, b, precision=None):
    return jnp.dot(a, b, preferred_element_type=F32, precision=precision)


def _dot_nt(a, b, precision=None):
    return lax.dot_general(a, b, (((1,), (1,)), ((), ())), preferred_element_type=F32, precision=precision)


def _dot_tn(a, b, precision=None):
    return lax.dot_general(a, b, (((0,), (0,)), ((), ())), preferred_element_type=F32, precision=precision)


def _sigmoid(x):
    return 1.0 / (1.0 + jnp.exp(-x))


def _silu(x):
    return x * _sigmoid(x)


def _softplus(x):
    return jnp.maximum(x, 0.0) + jnp.log(1.0 + jnp.exp(-jnp.abs(x)))


def _in_proj_kernel(x_ref, g_ref, w_ref, o_ref, h_ref):
    @pl.when(pl.program_id(1) == 0)
    def _():
        x = x_ref[...]
        ms = jnp.mean(x * x, axis=-1, keepdims=True)
        h_ref[...] = (x * lax.rsqrt(ms + EPS) * g_ref[...]).astype(BF16)

    o_ref[...] = _dot(h_ref[...], w_ref[...])


def _in_proj(x2, g, w, tm=1024, tn=IN_PROJ_TN):
    m = x2.shape[0]
    n = w.shape[1]
    return pl.pallas_call(
        _in_proj_kernel,
        grid=(m // tm, n // tn),
        in_specs=[
            pl.BlockSpec((tm, D_MODEL), lambda i, j: (i, 0)),
            pl.BlockSpec((1, D_MODEL), lambda i, j: (0, 0)),
            pl.BlockSpec((D_MODEL, tn), lambda i, j: (0, j)),
        ],
        out_specs=pl.BlockSpec((tm, tn), lambda i, j: (i, j)),
        out_shape=jax.ShapeDtypeStruct((m, n), F32),
        scratch_shapes=[pltpu.VMEM((tm, D_MODEL), BF16)],
        compiler_params=pltpu.CompilerParams(
            dimension_semantics=("parallel", "arbitrary"), vmem_limit_bytes=VMEM_LIMIT),
        name="in_proj",
    )(x2, g, w)


def _conv_kernel(cb_ref, cc_ref, cx_ref, w_ref, g_ref, o_ref):
    z = cc_ref[0] * cx_ref[0]
    row = lax.broadcasted_iota(jnp.int32, z.shape, 0)
    z1 = jnp.where(row >= 1, pltpu.roll(z, 1, axis=0), 0.0)
    z2 = jnp.where(row >= 2, pltpu.roll(z, 2, axis=0), 0.0)
    w = w_ref[...]
    y = w[0:1] * z2 + w[1:2] * z1 + w[2:3] * z
    y = cb_ref[0] * y
    ms = jnp.mean(y * y, axis=-1, keepdims=True)
    o_ref[0] = (y * lax.rsqrt(ms + EPS) * g_ref[...]).astype(o_ref.dtype)


def _conv_mixer(proj3, conv_w, conv_out_g):
    b, s, _ = proj3.shape
    gd = CONV_GROUP_DIM

    def col(off):
        return lambda i, g: (i, 0, off // gd + g)

    return pl.pallas_call(
        _conv_kernel,
        grid=(b, CONV_GROUPS),
        in_specs=[
            pl.BlockSpec((1, s, gd), col(OFF_CB)),
            pl.BlockSpec((1, s, gd), col(OFF_CC)),
            pl.BlockSpec((1, s, gd), col(OFF_CX)),
            pl.BlockSpec((3, gd), lambda i, g: (0, g)),
            pl.BlockSpec((1, gd), lambda i, g: (0, g)),
        ],
        out_specs=pl.BlockSpec((1, s, gd), lambda i, g: (i, 0, g)),
        out_shape=jax.ShapeDtypeStruct((b, s, CONV_WIDTH), BF16),
        compiler_params=pltpu.CompilerParams(
            dimension_semantics=("parallel", "parallel"), vmem_limit_bytes=VMEM_LIMIT),
        name="conv_mixer",
    )(proj3, proj3, proj3, conv_w, conv_out_g)


def _swa_kernel(q_ref, k_ref, v_ref, qg_ref, kg_ref, sink_ref, og_ref, o_ref, kk_ref, vv_ref):
    s_len = q_ref.shape[1]
    blk = ATTN_BLOCK
    hd = ATTN_HEAD_DIM
    nh = ATTN_HEADS
    group = nh // ATTN_KV_HEADS
    pair_w = 2 * hd

    lane = lax.broadcasted_iota(jnp.int32, (1, pair_w), 1)
    lo = lane < hd
    half_mask = (lo.astype(F32), 1.0 - lo.astype(F32))
    avg = jnp.where(lax.broadcasted_iota(jnp.int32, (pair_w, pair_w), 0) // hd
                    == lax.broadcasted_iota(jnp.int32, (pair_w, pair_w), 1) // hd, 1.0 / hd, 0.0).astype(BF16)

    def half_mean_sq(x):
        x2 = x * x
        hi = x2.astype(BF16)
        rem = (x2 - hi.astype(F32)).astype(BF16)
        return _dot(hi, avg) + _dot(rem, avg)

    k = k_ref[0]
    kn = k * lax.rsqrt(half_mean_sq(k) + EPS) * kg_ref[...]
    ksw = pltpu.roll(kn, hd, axis=1)
    v = v_ref[0]
    vsw = pltpu.roll(v, hd, axis=1)
    zeros = jnp.zeros((blk, pair_w), BF16)
    for j in range(ATTN_KV_HEADS):
        kk_ref[j, 0:blk, :] = zeros
        vv_ref[j, 0:blk, :] = zeros
    kk_ref[0, blk:, :] = jnp.where(lo, kn, ksw).astype(BF16)
    kk_ref[1, blk:, :] = jnp.where(lo, ksw, kn).astype(BF16)
    vv_ref[0, blk:, :] = jnp.where(lo, v, vsw).astype(BF16)
    vv_ref[1, blk:, :] = jnp.where(lo, vsw, v).astype(BF16)

    qi = lax.broadcasted_iota(jnp.int32, (blk, 2 * blk), 0)
    si = lax.broadcasted_iota(jnp.int32, (blk, 2 * blk), 1)
    rel = qi + blk - si
    band = (rel >= 0) & (rel < WINDOW)
    cur = si >= blk
    q_gain = [qg_ref[...] * (hd ** -0.5) * half_mask[i] for i in range(2)]

    def body(n, carry):
        r0 = pl.multiple_of(n * blk, blk)
        valid = band & (cur | (n > 0))
        heads = range(nh)
        qm = []
        for p in range(nh // 2):
            qp = q_ref[0, pl.ds(r0, blk), p * pair_w:(p + 1) * pair_w]
            qn = qp * lax.rsqrt(half_mean_sq(qp) + EPS)
            qm += [(qn * q_gain[0]).astype(BF16), (qn * q_gain[1]).astype(BF16)]
        s = [jnp.where(valid, _dot_nt(qm[h], kk_ref[h // group, pl.ds(r0, 2 * blk), :]), -jnp.inf)
             for h in heads]
        m = [jnp.maximum(jnp.max(s[h], axis=-1, keepdims=True), sink_ref[h]) for h in heads]
        p_ = [jnp.exp(s[h] - m[h]) for h in heads]
        inv = [1.0 / (jnp.sum(p_[h], axis=-1, keepdims=True) + jnp.exp(sink_ref[h] - m[h])) for h in heads]
        o = [_dot(p_[h].astype(BF16), vv_ref[h // group, pl.ds(r0, 2 * blk), :]) * inv[h] for h in heads]
        outs = []
        for p in range(nh // 2):
            op = jnp.where(lo, o[2 * p], o[2 * p + 1])
            outs.append(op * lax.rsqrt(half_mean_sq(op) + EPS) * og_ref[:, p * pair_w:(p + 1) * pair_w])
        o_ref[0, pl.ds(r0, blk), :] = jnp.concatenate(outs, axis=-1).astype(o_ref.dtype)
        return carry

    lax.fori_loop(0, s_len // blk, body, 0)


def _swa(proj3, q_g, k_g2, sinks, out_g):
    b, s, _ = proj3.shape
    return pl.pallas_call(
        _swa_kernel,
        grid=(b,),
        in_specs=[
            pl.BlockSpec((1, s, ATTN_WIDTH), lambda i: (i, 0, OFF_AQ // ATTN_WIDTH)),
            pl.BlockSpec((1, s, ATTN_KV_WIDTH), lambda i: (i, 0, OFF_AK // ATTN_KV_WIDTH)),
            pl.BlockSpec((1, s, ATTN_KV_WIDTH), lambda i: (i, 0, OFF_AV // ATTN_KV_WIDTH)),
            pl.BlockSpec((1, 2 * ATTN_HEAD_DIM), lambda i: (0, 0)),
            pl.BlockSpec((1, ATTN_KV_WIDTH), lambda i: (0, 0)),
            pl.BlockSpec(memory_space=pltpu.SMEM),
            pl.BlockSpec((1, ATTN_WIDTH), lambda i: (0, 0)),
        ],
        out_specs=pl.BlockSpec((1, s, ATTN_WIDTH), lambda i: (i, 0, 0)),
        out_shape=jax.ShapeDtypeStruct((b, s, ATTN_WIDTH), BF16),
        scratch_shapes=[pltpu.VMEM((ATTN_KV_HEADS, s + ATTN_BLOCK, 2 * ATTN_HEAD_DIM), BF16),
                        pltpu.VMEM((ATTN_KV_HEADS, s + ATTN_BLOCK, 2 * ATTN_HEAD_DIM), BF16)],
        compiler_params=pltpu.CompilerParams(
            dimension_semantics=("parallel",), vmem_limit_bytes=VMEM_LIMIT),
        name="swa",
    )(proj3, proj3, proj3, q_g, k_g2, sinks, out_g)


def _gdn_kernel(qkv_ref, z_ref, gate_ref, cw_ref, alog_ref, dtb_ref, ng_ref, o_ref, xpad_ref, state_ref):
    ts = qkv_ref.shape[1]
    c = GDN_CHUNK
    hd = GDN_HEAD_DIM
    nh = GDN_HEADS
    halo = SUBLANES
    t = pl.program_id(1)

    @pl.when(t == 0)
    def _():
        state_ref[...] = jnp.zeros(state_ref.shape, F32)
        xpad_ref[0:halo, :] = jnp.zeros((halo, xpad_ref.shape[1]), F32)

    @pl.when(t > 0)
    def _():
        xpad_ref[0:halo, :] = xpad_ref[ts:ts + halo, :]

    xpad_ref[halo:, :] = qkv_ref[0]

    row = lax.broadcasted_iota(jnp.int32, (c, c), 0)
    col = lax.broadcasted_iota(jnp.int32, (c, c), 1)
    tril = row >= col
    strict = row > col
    blk16 = (row // 16) == (col // 16)
    blk32 = (row // 32) == (col // 32)
    tril_f = tril.astype(F32)
    sel = (lax.broadcasted_iota(jnp.int32, (2 * nh, LANES), 0)
           == lax.broadcasted_iota(jnp.int32, (2 * nh, LANES), 1)).astype(F32)
    is_beta = lax.broadcasted_iota(jnp.int32, (1, LANES), 1) < nh
    neg_a = -jnp.exp(alog_ref[...])
    dtb = dtb_ref[...]

    def chunk(ci, carry):
        t0 = pl.multiple_of(ci * c, c)

        def conv_silu(base, h):
            lo = base + h * hd
            w = cw_ref[:, lo:lo + hd]
            win = xpad_ref[pl.ds(t0, c + halo), lo:lo + hd]
            y = w[GDN_CONV_K - 1:GDN_CONV_K] * win[halo:]
            for k in range(1, GDN_CONV_K):
                y = y + w[GDN_CONV_K - 1 - k:GDN_CONV_K - k] * pltpu.roll(win, k, axis=0)[halo:]
            return _silu(y)

        gl = gate_ref[0, pl.ds(t0, c), :]
        gmix = jnp.where(is_beta, _sigmoid(gl), neg_a * _softplus(gl + dtb))
        gc_all = _dot(tril_f, gmix, HIGHEST)
        rows = _dot_nt(sel, jnp.concatenate([gmix, gc_all], axis=0), HIGHEST)

        qs, ks, vs = [], [], []
        for h in range(nh):
            q = conv_silu(0, h)
            k = conv_silu(GDN_WIDTH, h)
            vs.append(conv_silu(2 * GDN_WIDTH, h))
            qs.append(q * (lax.rsqrt(jnp.sum(q * q, axis=-1, keepdims=True) + EPS) * (hd ** -0.5)))
            ks.append(k * lax.rsqrt(jnp.sum(k * k, axis=-1, keepdims=True) + EPS))

        heads = range(nh)
        kkqk = []
        for h in heads:
            kb = ks[h].astype(BF16)
            kkqk.append(_dot_nt(jnp.concatenate([kb, qs[h].astype(BF16)], axis=0), kb))
        beta = [gmix[:, h:h + 1] for h in heads]
        gc = [gc_all[:, nh + h:nh + h + 1] for h in heads]
        g_last = [gc_all[c - 1:c, nh + h:nh + h + 1] for h in heads]
        es, ps, qkb, off_diag = [], [], [], []
        for h in heads:
            gc_row = rows[nh + h:nh + h + 1, c:]
            decay = jnp.exp(jnp.where(tril, gc[h] - gc_row, 0.0))
            a = jnp.where(strict, kkqk[h][:c] * beta[h] * decay, 0.0)
            qkb.append(jnp.where(tril, kkqk[h][c:] * decay, 0.0).astype(BF16))
            a_d = jnp.where(blk16, a, 0.0)
            es.append(-a_d)
            ps.append(a_d)
            off_diag.append((jnp.where(blk32, a - a_d, 0.0), jnp.where(blk32, 0.0, a)))
        ps = [_dot(ps[h].astype(BF16), ps[h].astype(BF16)) for h in heads]
        for i in range(3):
            if i < 2:
                ep = [_dot(jnp.concatenate([es[h], ps[h]], axis=0).astype(BF16), ps[h].astype(BF16))
                      for h in heads]
                es = [es[h] + ps[h] + ep[h][:c] for h in heads]
                ps = [ep[h][c:] for h in heads]
            else:
                ep = [_dot(es[h].astype(BF16), ps[h].astype(BF16)) for h in heads]
                es = [es[h] + ps[h] + ep[h] for h in heads]
        for level in range(2):
            ls = [off_diag[h][level] for h in heads]
            g1 = [ls[h] + _dot(es[h].astype(BF16), ls[h].astype(BF16)) for h in heads]
            h1 = [g1[h] + _dot(g1[h].astype(BF16), es[h].astype(BF16)) for h in heads]
            es = [es[h] - h1[h] for h in heads]
        egc = [jnp.exp(gc[h]) for h in heads]
        wu = []
        for h in heads:
            kv = jnp.concatenate([ks[h] * egc[h], vs[h]], axis=-1)
            e_b = (es[h] * rows[h:h + 1, :c]).astype(BF16)
            wu.append(kv * beta[h] + _dot(e_b, kv.astype(BF16)))
        st = [state_ref[h] for h in heads]
        ws_qs = [_dot(jnp.concatenate([wu[h][:, :hd], qs[h] * egc[h]], axis=0).astype(BF16), st[h].astype(BF16))
                 for h in heads]
        vb = [(wu[h][:, hd:] - ws_qs[h][:c]).astype(BF16) for h in heads]
        kd = [(ks[h] * jnp.exp(g_last[h] - gc[h])).astype(BF16) for h in heads]
        os_ = [ws_qs[h][c:] + _dot(qkb[h], vb[h]) for h in heads]
        for h in heads:
            state_ref[h] = st[h] * jnp.exp(g_last[h]) + _dot_tn(kd[h], vb[h])
        for h in heads:
            o = os_[h]
            oms = jnp.mean(o * o, axis=-1, keepdims=True)
            zh = z_ref[0, pl.ds(t0, c), h * hd:(h + 1) * hd]
            y = o * lax.rsqrt(oms + EPS) * ng_ref[...] * _silu(zh)
            o_ref[0, pl.ds(t0, c), h * hd:(h + 1) * hd] = y.astype(o_ref.dtype)
        return carry

    lax.fori_loop(0, ts // c, chunk, 0)


def _gdn(proj3, conv_w, alog_b, dtb_b, norm_g, ts=256):
    b, s, _ = proj3.shape
    qkv_w = 3 * GDN_WIDTH
    return pl.pallas_call(
        _gdn_kernel,
        grid=(b, s // ts),
        in_specs=[
            pl.BlockSpec((1, ts, qkv_w), lambda i, t: (i, t, OFF_GQKV // qkv_w)),
            pl.BlockSpec((1, ts, GDN_WIDTH), lambda i, t: (i, t, OFF_GZ // GDN_WIDTH)),
            pl.BlockSpec((1, ts, GATE_PAD), lambda i, t: (i, t, OFF_GATE // GATE_PAD)),
            pl.BlockSpec((GDN_CONV_K, qkv_w), lambda i, t: (0, 0)),
            pl.BlockSpec((1, LANES), lambda i, t: (0, 0)),
            pl.BlockSpec((1, LANES), lambda i, t: (0, 0)),
            pl.BlockSpec((1, GDN_HEAD_DIM), lambda i, t: (0, 0)),
        ],
        out_specs=pl.BlockSpec((1, ts, GDN_WIDTH), lambda i, t: (i, t, 0)),
        out_shape=jax.ShapeDtypeStruct((b, s, GDN_WIDTH), BF16),
        scratch_shapes=[pltpu.VMEM((ts + SUBLANES, qkv_w), F32),
                        pltpu.VMEM((GDN_HEADS, GDN_HEAD_DIM, GDN_HEAD_DIM), F32)],
        compiler_params=pltpu.CompilerParams(
            dimension_semantics=("parallel", "arbitrary"), vmem_limit_bytes=VMEM_LIMIT),
        name="gdn",
    )(proj3, proj3, proj3, conv_w, alog_b, dtb_b, norm_g)


def _out_proj_kernel(x_ref, yc_ref, ya_ref, yg_ref, w_ref, g_ref, x1_ref, h_ref):
    acc = _dot(yc_ref[...], w_ref[0:CONV_WIDTH, :])
    acc = acc + _dot(ya_ref[...], w_ref[CONV_WIDTH:CONV_WIDTH + ATTN_WIDTH, :])
    acc = acc + _dot(yg_ref[...], w_ref[CONV_WIDTH + ATTN_WIDTH:, :])
    x1 = x_ref[...] + acc
    x1_ref[...] = x1
    ms = jnp.mean(x1 * x1, axis=-1, keepdims=True)
    h_ref[...] = (x1 * lax.rsqrt(ms + EPS) * g_ref[...]).astype(BF16)


def _out_proj(x2, yc, ya, yg, w, g, tm=512):
    m = x2.shape[0]
    return pl.pallas_call(
        _out_proj_kernel,
        grid=(m // tm,),
        in_specs=[
            pl.BlockSpec((tm, D_MODEL), lambda i: (i, 0)),
            pl.BlockSpec((tm, CONV_WIDTH), lambda i: (i, 0)),
            pl.BlockSpec((tm, ATTN_WIDTH), lambda i: (i, 0)),
            pl.BlockSpec((tm, GDN_WIDTH), lambda i: (i, 0)),
            pl.BlockSpec((D_MODEL, D_MODEL), lambda i: (0, 0)),
            pl.BlockSpec((1, D_MODEL), lambda i: (0, 0)),
        ],
        out_specs=[pl.BlockSpec((tm, D_MODEL), lambda i: (i, 0)),
                   pl.BlockSpec((tm, D_MODEL), lambda i: (i, 0))],
        out_shape=[jax.ShapeDtypeStruct((m, D_MODEL), F32),
                   jax.ShapeDtypeStruct((m, D_MODEL), BF16)],
        compiler_params=pltpu.CompilerParams(
            dimension_semantics=("parallel",), vmem_limit_bytes=VMEM_LIMIT),
        name="out_proj",
    )(x2, yc, ya, yg, w, g)


def _mlp_kernel(x1_ref, h_ref, wu_ref, wd_ref, o_ref):
    @pl.when(pl.program_id(1) == 0)
    def _():
        o_ref[...] = x1_ref[...]

    hid = jnp.maximum(_dot(h_ref[...], wu_ref[...]), 0.0)
    hid = (hid * hid).astype(BF16)
    o_ref[...] += _dot(hid, wd_ref[...])


def _mlp(x1, h, w_up, w_down, tm=512, tf=1024):
    m = x1.shape[0]
    return pl.pallas_call(
        _mlp_kernel,
        grid=(m // tm, D_FF // tf),
        in_specs=[
            pl.BlockSpec((tm, D_MODEL), lambda i, f: (i, 0)),
            pl.BlockSpec((tm, D_MODEL), lambda i, f: (i, 0)),
            pl.BlockSpec((D_MODEL, tf), lambda i, f: (0, f)),
            pl.BlockSpec((tf, D_MODEL), lambda i, f: (f, 0)),
        ],
        out_specs=pl.BlockSpec((tm, D_MODEL), lambda i, f: (i, 0)),
        out_shape=jax.ShapeDtypeStruct((m, D_MODEL), F32),
        compiler_params=pltpu.CompilerParams(
            dimension_semantics=("parallel", "arbitrary"), vmem_limit_bytes=VMEM_LIMIT),
        name="mlp",
    )(x1, h, w_up, w_down)


def _regroup_w_in(w):
    n_conv = 3 * CONV_WIDTH
    n_attn = ATTN_WIDTH + 2 * ATTN_KV_WIDTH
    n_gdn = 4 * GDN_WIDTH
    conv = w[:, :n_conv]
    attn = w[:, n_conv:n_conv + n_attn]
    gdn = w[:, n_conv + n_attn:n_conv + n_attn + n_gdn]
    gate = w[:, n_conv + n_attn + n_gdn:]
    pad = jnp.zeros((w.shape[0], D_IN_PAD - w.shape[1]), w.dtype)
    return jnp.concatenate([gdn, conv, attn, gate, pad], axis=1).astype(BF16)


def _decay_lanes(p):
    return jnp.zeros((1, LANES), F32).at[0, GDN_HEADS:2 * GDN_HEADS].set(p)


def _layer(x2, b, s, norm1_g, w_in, conv_w, conv_out_g, q_norm_g, k_norm_g, attn_sinks, attn_out_g,
           gdn_conv_w, gdn_A_log, gdn_dt_bias, gdn_norm_g, w_out, norm2_g, w_up, w_down):
    proj = _in_proj(x2, norm1_g[None, :], _regroup_w_in(w_in))
    proj3 = proj.reshape(b, s, D_IN_PAD)
    yc = _conv_mixer(proj3, conv_w, conv_out_g[None, :])
    ya = _swa(proj3, jnp.tile(q_norm_g, 2)[None, :], jnp.tile(k_norm_g, ATTN_KV_HEADS)[None, :], attn_sinks,
              attn_out_g[None, :])
    yg = _gdn(proj3, gdn_conv_w, _decay_lanes(gdn_A_log), _decay_lanes(gdn_dt_bias), gdn_norm_g[None, :])
    m = b * s
    x1, h2 = _out_proj(x2, yc.reshape(m, CONV_WIDTH), ya.reshape(m, ATTN_WIDTH), yg.reshape(m, GDN_WIDTH),
                       w_out.astype(BF16), norm2_g[None, :])
    return _mlp(x1, h2, w_up.astype(BF16), w_down.astype(BF16))


def kernel(x, norm1_g, w_in, conv_w, conv_out_g, q_norm_g, k_norm_g, attn_sinks, attn_out_g, gdn_conv_w,
           gdn_A_log, gdn_dt_bias, gdn_norm_g, w_out, norm2_g, w_up, w_down):
    b, s, d = x.shape
    x2 = x.reshape(b * s, d)
    for l in range(norm1_g.shape[0]):
        x2 = _layer(x2, b, s, norm1_g[l], w_in[l], conv_w[l], conv_out_g[l], q_norm_g[l], k_norm_g[l],
                    attn_sinks[l], attn_out_g[l], gdn_conv_w[l], gdn_A_log[l], gdn_dt_bias[l],
                    gdn_norm_g[l], w_out[l], norm2_g[l], w_up[l], w_down[l])
    return x2.reshape(b, s, d)
```

```python
import jax
import jax.numpy as jnp
from jax import lax
from jax.experimental import pallas as pl
from jax.experimental.pallas import tpu as pltpu

F32 = jnp.float32
BF16 = jnp.bfloat16
HIGHEST = lax.Precision.HIGHEST

D_MODEL = 2048
CONV_WIDTH = 512
CONV_GROUPS = 4
CONV_GROUP_DIM = 128
ATTN_HEAD_DIM = 64
ATTN_HEADS = 8
ATTN_KV_HEADS = 2
ATTN_WIDTH = 512
ATTN_KV_WIDTH = 128
WINDOW = 128
ATTN_BLOCK = 128
GDN_HEAD_DIM = 128
GDN_WIDTH = 1024
GDN_HEADS = 8
GDN_CONV_K = 4
GDN_CHUNK = 64
GDN_BATCH = 2
D_FF = 8192
EPS = 1e-6

LANES = 128
SUBLANES = 8
VMEM_LIMIT = 56 * 1024 * 1024

GATE_PAD = LANES
IN_PROJ_TN = 13 * LANES
D_IN_PAD = 4 * IN_PROJ_TN
OFF_GQKV = 0
OFF_GZ = 3 * GDN_WIDTH
OFF_CB = 4 * GDN_WIDTH
OFF_CC = OFF_CB + CONV_WIDTH
OFF_CX = OFF_CC + CONV_WIDTH
OFF_AQ = OFF_CX + CONV_WIDTH
OFF_AK = OFF_AQ + ATTN_WIDTH
OFF_AV = OFF_AK + ATTN_KV_WIDTH
OFF_GATE = OFF_AV + ATTN_KV_WIDTH
assert OFF_GATE + GATE_PAD <= D_IN_PAD


def _dot(a, b, precision=None):
    return jnp.dot(a, b, preferred_element_type=F32, precision=precision)


def _dot_nt(a, b, precision=None):
    return lax.dot_general(a, b, (((1,), (1,)), ((), ())), preferred_element_type=F32, precision=precision)


def _dot_tn(a, b, precision=None):
    return lax.dot_general(a, b, (((0,), (0,)), ((), ())), preferred_element_type=F32, precision=precision)


def _sigmoid(x):
    return 1.0 / (1.0 + jnp.exp(-x))


def _silu(x):
    return x * _sigmoid(x)


def _softplus(x):
    return jnp.maximum(x, 0.0) + jnp.log(1.0 + jnp.exp(-jnp.abs(x)))


def _in_proj_kernel(x_ref, g_ref, w_ref, o_ref, h_ref):
    @pl.when(pl.program_id(1) == 0)
    def _():
        x = x_ref[...]
        ms = jnp.mean(x * x, axis=-1, keepdims=True)
        h_ref[...] = (x * lax.rsqrt(ms + EPS) * g_ref[...]).astype(BF16)

    o_ref[...] = _dot(h_ref[...], w_ref[...])


def _in_proj(x2, g, w_all, layer, tm=1024, tn=IN_PROJ_TN):
    m = x2.shape[0]
    n = w_all.shape[2]
    return pl.pallas_call(
        _in_proj_kernel,
        grid=(m // tm, n // tn),
        in_specs=[
            pl.BlockSpec((tm, D_MODEL), lambda i, j: (i, 0)),
            pl.BlockSpec((1, D_MODEL), lambda i, j: (0, 0)),
            pl.BlockSpec((None, D_MODEL, tn), lambda i, j: (layer, 0, j)),
        ],
        out_specs=pl.BlockSpec((tm, tn), lambda i, j: (i, j)),
        out_shape=jax.ShapeDtypeStruct((m, n), F32),
        scratch_shapes=[pltpu.VMEM((tm, D_MODEL), BF16)],
        compiler_params=pltpu.CompilerParams(
            dimension_semantics=("parallel", "arbitrary"), vmem_limit_bytes=VMEM_LIMIT),
        name="in_proj",
    )(x2, g, w_all)


def _conv_kernel(cb_ref, cc_ref, cx_ref, w_ref, g_ref, o_ref):
    z = cc_ref[0] * cx_ref[0]
    row = lax.broadcasted_iota(jnp.int32, z.shape, 0)
    z1 = jnp.where(row >= 1, pltpu.roll(z, 1, axis=0), 0.0)
    z2 = jnp.where(row >= 2, pltpu.roll(z, 2, axis=0), 0.0)
    w = w_ref[...]
    y = w[0:1] * z2 + w[1:2] * z1 + w[2:3] * z
    y = cb_ref[0] * y
    ms = jnp.mean(y * y, axis=-1, keepdims=True)
    o_ref[0] = (y * lax.rsqrt(ms + EPS) * g_ref[...]).astype(o_ref.dtype)


def _conv_mixer(proj3, conv_w, conv_out_g):
    b, s, _ = proj3.shape
    gd = CONV_GROUP_DIM

    def col(off):
        return lambda i, g: (i, 0, off // gd + g)

    return pl.pallas_call(
        _conv_kernel,
        grid=(b, CONV_GROUPS),
        in_specs=[
            pl.BlockSpec((1, s, gd), col(OFF_CB)),
            pl.BlockSpec((1, s, gd), col(OFF_CC)),
            pl.BlockSpec((1, s, gd), col(OFF_CX)),
            pl.BlockSpec((3, gd), lambda i, g: (0, g)),
            pl.BlockSpec((1, gd), lambda i, g: (0, g)),
        ],
        out_specs=pl.BlockSpec((1, s, gd), lambda i, g: (i, 0, g)),
        out_shape=jax.ShapeDtypeStruct((b, s, CONV_WIDTH), BF16),
        compiler_params=pltpu.CompilerParams(
            dimension_semantics=("parallel", "parallel"), vmem_limit_bytes=VMEM_LIMIT),
        name="conv_mixer",
    )(proj3, proj3, proj3, conv_w, conv_out_g)


def _swa_kernel(q_ref, k_ref, v_ref, qg_ref, kg_ref, sink_ref, og_ref, o_ref, kk_ref, vv_ref):
    s_len = q_ref.shape[1]
    blk = ATTN_BLOCK
    hd = ATTN_HEAD_DIM
    nh = ATTN_HEADS
    group = nh // ATTN_KV_HEADS
    pair_w = 2 * hd

    lane = lax.broadcasted_iota(jnp.int32, (1, pair_w), 1)
    lo = lane < hd
    half_mask = (lo.astype(F32), 1.0 - lo.astype(F32))
    avg = jnp.where(lax.broadcasted_iota(jnp.int32, (pair_w, pair_w), 0) // hd
                    == lax.broadcasted_iota(jnp.int32, (pair_w, pair_w), 1) // hd, 1.0 / hd, 0.0).astype(BF16)

    def half_mean_sq(x):
        x2 = x * x
        hi = x2.astype(BF16)
        rem = (x2 - hi.astype(F32)).astype(BF16)
        return _dot(hi, avg) + _dot(rem, avg)

    k = k_ref[0]
    kn = k * lax.rsqrt(half_mean_sq(k) + EPS) * kg_ref[...]
    ksw = pltpu.roll(kn, hd, axis=1)
    v = v_ref[0]
    vsw = pltpu.roll(v, hd, axis=1)
    zeros = jnp.zeros((blk, pair_w), BF16)
    for j in range(ATTN_KV_HEADS):
        kk_ref[j, 0:blk, :] = zeros
        vv_ref[j, 0:blk, :] = zeros
    kk_ref[0, blk:, :] = jnp.where(lo, kn, ksw).astype(BF16)
    kk_ref[1, blk:, :] = jnp.where(lo, ksw, kn).astype(BF16)
    vv_ref[0, blk:, :] = jnp.where(lo, v, vsw).astype(BF16)
    vv_ref[1, blk:, :] = jnp.where(lo, vsw, v).astype(BF16)

    qi = lax.broadcasted_iota(jnp.int32, (blk, 2 * blk), 0)
    si = lax.broadcasted_iota(jnp.int32, (blk, 2 * blk), 1)
    rel = qi + blk - si
    band = (rel >= 0) & (rel < WINDOW)
    cur = si >= blk
    q_gain = [qg_ref[...] * (hd ** -0.5) * half_mask[i] for i in range(2)]

    def body(n, carry):
        r0 = pl.multiple_of(n * blk, blk)
        valid = band & (cur | (n > 0))
        heads = range(nh)
        qm = []
        for p in range(nh // 2):
            qp = q_ref[0, pl.ds(r0, blk), p * pair_w:(p + 1) * pair_w]
            qn = qp * lax.rsqrt(half_mean_sq(qp) + EPS)
            qm += [(qn * q_gain[0]).astype(BF16), (qn * q_gain[1]).astype(BF16)]
        s = [jnp.where(valid, _dot_nt(qm[h], kk_ref[h // group, pl.ds(r0, 2 * blk), :]), -jnp.inf)
             for h in heads]
        m = [jnp.maximum(jnp.max(s[h], axis=-1, keepdims=True), sink_ref[h]) for h in heads]
        p_ = [jnp.exp(s[h] - m[h]) for h in heads]
        inv = [1.0 / (jnp.sum(p_[h], axis=-1, keepdims=True) + jnp.exp(sink_ref[h] - m[h])) for h in heads]
        o = [_dot(p_[h].astype(BF16), vv_ref[h // group, pl.ds(r0, 2 * blk), :]) * inv[h] for h in heads]
        outs = []
        for p in range(nh // 2):
            op = jnp.where(lo, o[2 * p], o[2 * p + 1])
            outs.append(op * lax.rsqrt(half_mean_sq(op) + EPS) * og_ref[:, p * pair_w:(p + 1) * pair_w])
        o_ref[0, pl.ds(r0, blk), :] = jnp.concatenate(outs, axis=-1).astype(o_ref.dtype)
        return carry

    lax.fori_loop(0, s_len // blk, body, 0)


def _swa(proj3, q_g, k_g2, sinks, out_g):
    b, s, _ = proj3.shape
    return pl.pallas_call(
        _swa_kernel,
        grid=(b,),
        in_specs=[
            pl.BlockSpec((1, s, ATTN_WIDTH), lambda i: (i, 0, OFF_AQ // ATTN_WIDTH)),
            pl.BlockSpec((1, s, ATTN_KV_WIDTH), lambda i: (i, 0, OFF_AK // ATTN_KV_WIDTH)),
            pl.BlockSpec((1, s, ATTN_KV_WIDTH), lambda i: (i, 0, OFF_AV // ATTN_KV_WIDTH)),
            pl.BlockSpec((1, 2 * ATTN_HEAD_DIM), lambda i: (0, 0)),
            pl.BlockSpec((1, ATTN_KV_WIDTH), lambda i: (0, 0)),
            pl.BlockSpec(memory_space=pltpu.SMEM),
            pl.BlockSpec((1, ATTN_WIDTH), lambda i: (0, 0)),
        ],
        out_specs=pl.BlockSpec((1, s, ATTN_WIDTH), lambda i: (i, 0, 0)),
        out_shape=jax.ShapeDtypeStruct((b, s, ATTN_WIDTH), BF16),
        scratch_shapes=[pltpu.VMEM((ATTN_KV_HEADS, s + ATTN_BLOCK, 2 * ATTN_HEAD_DIM), BF16),
                        pltpu.VMEM((ATTN_KV_HEADS, s + ATTN_BLOCK, 2 * ATTN_HEAD_DIM), BF16)],
        compiler_params=pltpu.CompilerParams(
            dimension_semantics=("parallel",), vmem_limit_bytes=VMEM_LIMIT),
        name="swa",
    )(proj3, proj3, proj3, q_g, k_g2, sinks, out_g)


def _gdn_kernel(qkv_ref, z_ref, gate_ref, cw_ref, alog_ref, dtb_ref, ng_ref, o_ref, halo_ref, state_ref):
    nb, ts = qkv_ref.shape[0], qkv_ref.shape[1]
    c = GDN_CHUNK
    hd = GDN_HEAD_DIM
    nh = GDN_HEADS
    halo = SUBLANES
    units = [(bb, h) for bb in range(nb) for h in range(nh)]
    nu = range(len(units))

    @pl.when(pl.program_id(1) == 0)
    def _():
        state_ref[...] = jnp.zeros(state_ref.shape, F32)
        halo_ref[...] = jnp.zeros(halo_ref.shape, F32)

    row = lax.broadcasted_iota(jnp.int32, (c, c), 0)
    col = lax.broadcasted_iota(jnp.int32, (c, c), 1)
    tril = row >= col
    strict = row > col
    blk16 = (row // 16) == (col // 16)
    blk32 = (row // 32) == (col // 32)
    tril_f = tril.astype(F32)
    sel = (lax.broadcasted_iota(jnp.int32, (2 * nh, LANES), 0)
           == lax.broadcasted_iota(jnp.int32, (2 * nh, LANES), 1)).astype(F32)
    is_beta = lax.broadcasted_iota(jnp.int32, (1, LANES), 1) < nh
    neg_a = -jnp.exp(alog_ref[...])
    dtb = dtb_ref[...]

    def chunk(ci, carry):
        t0 = pl.multiple_of(ci * c, c)
        tp = pl.multiple_of(jnp.maximum(t0 - halo, 0), halo)
        first = ci == 0

        def conv_silu(bb, lo):
            w = cw_ref[:, lo:lo + hd]
            prev = jnp.where(first, halo_ref[bb, :, lo:lo + hd], qkv_ref[bb, pl.ds(tp, halo), lo:lo + hd])
            win = jnp.concatenate([prev, qkv_ref[bb, pl.ds(t0, c), lo:lo + hd]], axis=0)
            y = w[GDN_CONV_K - 1:GDN_CONV_K] * win[halo:]
            for k in range(1, GDN_CONV_K):
                y = y + w[GDN_CONV_K - 1 - k:GDN_CONV_K - k] * pltpu.roll(win, k, axis=0)[halo:]
            return _silu(y)

        gmix, gc_all, rows = [], [], []
        for bb in range(nb):
            gl = gate_ref[bb, pl.ds(t0, c), :]
            gm = jnp.where(is_beta, _sigmoid(gl), neg_a * _softplus(gl + dtb))
            ga = _dot(tril_f, gm, HIGHEST)
            gmix.append(gm)
            gc_all.append(ga)
            rows.append(_dot_nt(sel, jnp.concatenate([gm, ga], axis=0), HIGHEST))

        qs, ks, vs = [], [], []
        for bb, h in units:
            q = conv_silu(bb, h * hd)
            k = conv_silu(bb, GDN_WIDTH + h * hd)
            vs.append(conv_silu(bb, 2 * GDN_WIDTH + h * hd))
            qs.append(q * (lax.rsqrt(jnp.sum(q * q, axis=-1, keepdims=True) + EPS) * (hd ** -0.5)))
            ks.append(k * lax.rsqrt(jnp.sum(k * k, axis=-1, keepdims=True) + EPS))

        kkqk = []
        for u in nu:
            kb = ks[u].astype(BF16)
            kkqk.append(_dot_nt(jnp.concatenate([kb, qs[u].astype(BF16)], axis=0), kb))
        beta = [gmix[bb][:, h:h + 1] for bb, h in units]
        gc = [gc_all[bb][:, nh + h:nh + h + 1] for bb, h in units]
        g_last = [gc_all[bb][c - 1:c, nh + h:nh + h + 1] for bb, h in units]
        beta_row = [rows[bb][h:h + 1, :c] for bb, h in units]
        es, ps, qkb, off_diag = [], [], [], []
        for u, (bb, h) in enumerate(units):
            gc_row = rows[bb][nh + h:nh + h + 1, c:]
            decay = jnp.exp(jnp.where(tril, gc[u] - gc_row, 0.0))
            a = jnp.where(strict, kkqk[u][:c] * beta[u] * decay, 0.0)
            qkb.append(jnp.where(tril, kkqk[u][c:] * decay, 0.0).astype(BF16))
            a_d = jnp.where(blk16, a, 0.0)
            es.append(-a_d)
            ps.append(a_d)
            off_diag.append((jnp.where(blk32, a - a_d, 0.0), jnp.where(blk32, 0.0, a)))
        ps = [_dot(ps[u].astype(BF16), ps[u].astype(BF16)) for u in nu]
        for i in range(3):
            if i < 2:
                ep = [_dot(jnp.concatenate([es[u], ps[u]], axis=0).astype(BF16), ps[u].astype(BF16)) for u in nu]
                es = [es[u] + ps[u] + ep[u][:c] for u in nu]
                ps = [ep[u][c:] for u in nu]
            else:
                ep = [_dot(es[u].astype(BF16), ps[u].astype(BF16)) for u in nu]
                es = [es[u] + ps[u] + ep[u] for u in nu]
        for level in range(2):
            ls = [off_diag[u][level] for u in nu]
            g1 = [ls[u] + _dot(es[u].astype(BF16), ls[u].astype(BF16)) for u in nu]
            es = [es[u] - (g1[u] + _dot(g1[u].astype(BF16), es[u].astype(BF16))) for u in nu]
        egc = [jnp.exp(gc[u]) for u in nu]
        wu = []
        for u in nu:
            kv = jnp.concatenate([ks[u] * egc[u], vs[u]], axis=-1)
            e_b = (es[u] * beta_row[u]).astype(BF16)
            wu.append(kv * beta[u] + _dot(e_b, kv.astype(BF16)))
        st = [state_ref[bb, h] for bb, h in units]
        ws_qs = [_dot(jnp.concatenate([wu[u][:, :hd], qs[u] * egc[u]], axis=0).astype(BF16), st[u].astype(BF16))
                 for u in nu]
        vb = [(wu[u][:, hd:] - ws_qs[u][:c]).astype(BF16) for u in nu]
        kd = [(ks[u] * jnp.exp(g_last[u] - gc[u])).astype(BF16) for u in nu]
        os_ = [ws_qs[u][c:] + _dot(qkb[u], vb[u]) for u in nu]
        for u, (bb, h) in enumerate(units):
            state_ref[bb, h] = st[u] * jnp.exp(g_last[u]) + _dot_tn(kd[u], vb[u])
        for u, (bb, h) in enumerate(units):
            o = os_[u]
            oms = jnp.mean(o * o, axis=-1, keepdims=True)
            zh = z_ref[bb, pl.ds(t0, c), h * hd:(h + 1) * hd]
            y = o * lax.rsqrt(oms + EPS) * ng_ref[...] * _silu(zh)
            o_ref[bb, pl.ds(t0, c), h * hd:(h + 1) * hd] = y.astype(o_ref.dtype)
        return carry

    lax.fori_loop(0, ts // c, chunk, 0)
    halo_ref[...] = qkv_ref[:, ts - halo:ts, :]


def _gdn(proj3, conv_w, alog_b, dtb_b, norm_g, ts=256, nb=GDN_BATCH):
    b, s, _ = proj3.shape
    qkv_w = 3 * GDN_WIDTH
    return pl.pallas_call(
        _gdn_kernel,
        grid=(b // nb, s // ts),
        in_specs=[
            pl.BlockSpec((nb, ts, qkv_w), lambda i, t: (i, t, OFF_GQKV // qkv_w)),
            pl.BlockSpec((nb, ts, GDN_WIDTH), lambda i, t: (i, t, OFF_GZ // GDN_WIDTH)),
            pl.BlockSpec((nb, ts, GATE_PAD), lambda i, t: (i, t, OFF_GATE // GATE_PAD)),
            pl.BlockSpec((GDN_CONV_K, qkv_w), lambda i, t: (0, 0)),
            pl.BlockSpec((1, LANES), lambda i, t: (0, 0)),
            pl.BlockSpec((1, LANES), lambda i, t: (0, 0)),
            pl.BlockSpec((1, GDN_HEAD_DIM), lambda i, t: (0, 0)),
        ],
        out_specs=pl.BlockSpec((nb, ts, GDN_WIDTH), lambda i, t: (i, t, 0)),
        out_shape=jax.ShapeDtypeStruct((b, s, GDN_WIDTH), BF16),
        scratch_shapes=[pltpu.VMEM((nb, SUBLANES, qkv_w), F32),
                        pltpu.VMEM((nb, GDN_HEADS, GDN_HEAD_DIM, GDN_HEAD_DIM), F32)],
        compiler_params=pltpu.CompilerParams(
            dimension_semantics=("parallel", "arbitrary"), vmem_limit_bytes=VMEM_LIMIT),
        name="gdn",
    )(proj3, proj3, proj3, conv_w, alog_b, dtb_b, norm_g)


def _out_proj_kernel(x_ref, yc_ref, ya_ref, yg_ref, w_ref, g_ref, x1_ref, h_ref):
    acc = _dot(yc_ref[...], w_ref[0:CONV_WIDTH, :])
    acc = acc + _dot(ya_ref[...], w_ref[CONV_WIDTH:CONV_WIDTH + ATTN_WIDTH, :])
    acc = acc + _dot(yg_ref[...], w_ref[CONV_WIDTH + ATTN_WIDTH:, :])
    x1 = x_ref[...] + acc
    x1_ref[...] = x1
    ms = jnp.mean(x1 * x1, axis=-1, keepdims=True)
    h_ref[...] = (x1 * lax.rsqrt(ms + EPS) * g_ref[...]).astype(BF16)


def _out_proj(x2, yc, ya, yg, w_all, layer, g, tm=512):
    m = x2.shape[0]
    return pl.pallas_call(
        _out_proj_kernel,
        grid=(m // tm,),
        in_specs=[
            pl.BlockSpec((tm, D_MODEL), lambda i: (i, 0)),
            pl.BlockSpec((tm, CONV_WIDTH), lambda i: (i, 0)),
            pl.BlockSpec((tm, ATTN_WIDTH), lambda i: (i, 0)),
            pl.BlockSpec((tm, GDN_WIDTH), lambda i: (i, 0)),
            pl.BlockSpec((None, D_MODEL, D_MODEL), lambda i: (layer, 0, 0)),
            pl.BlockSpec((1, D_MODEL), lambda i: (0, 0)),
        ],
        out_specs=[pl.BlockSpec((tm, D_MODEL), lambda i: (i, 0)),
                   pl.BlockSpec((tm, D_MODEL), lambda i: (i, 0))],
        out_shape=[jax.ShapeDtypeStruct((m, D_MODEL), F32),
                   jax.ShapeDtypeStruct((m, D_MODEL), BF16)],
        compiler_params=pltpu.CompilerParams(
            dimension_semantics=("parallel",), vmem_limit_bytes=VMEM_LIMIT),
        name="out_proj",
    )(x2, yc, ya, yg, w_all, g)


def _mlp_kernel(x1_ref, h_ref, wu_ref, wd_ref, o_ref):
    @pl.when(pl.program_id(1) == 0)
    def _():
        o_ref[...] = x1_ref[...]

    hid = jnp.maximum(_dot(h_ref[...], wu_ref[...]), 0.0)
    hid = (hid * hid).astype(BF16)
    o_ref[...] += _dot(hid, wd_ref[...])


def _mlp(x1, h, w_up_all, w_down_all, layer, tm=512, tf=1024):
    m = x1.shape[0]
    return pl.pallas_call(
        _mlp_kernel,
        grid=(m // tm, D_FF // tf),
        in_specs=[
            pl.BlockSpec((tm, D_MODEL), lambda i, f: (i, 0)),
            pl.BlockSpec((tm, D_MODEL), lambda i, f: (i, 0)),
            pl.BlockSpec((None, D_MODEL, tf), lambda i, f: (layer, 0, f)),
            pl.BlockSpec((None, tf, D_MODEL), lambda i, f: (layer, f, 0)),
        ],
        out_specs=pl.BlockSpec((tm, D_MODEL), lambda i, f: (i, 0)),
        out_shape=jax.ShapeDtypeStruct((m, D_MODEL), F32),
        compiler_params=pltpu.CompilerParams(
            dimension_semantics=("parallel", "arbitrary"), vmem_limit_bytes=VMEM_LIMIT),
        name="mlp",
    )(x1, h, w_up_all, w_down_all)


def _regroup_w_in(w):
    n_conv = 3 * CONV_WIDTH
    n_attn = ATTN_WIDTH + 2 * ATTN_KV_WIDTH
    n_gdn = 4 * GDN_WIDTH
    conv = w[..., :n_conv]
    attn = w[..., n_conv:n_conv + n_attn]
    gdn = w[..., n_conv + n_attn:n_conv + n_attn + n_gdn]
    gate = w[..., n_conv + n_attn + n_gdn:]
    pad = jnp.zeros(w.shape[:-1] + (D_IN_PAD - w.shape[-1],), BF16)
    return jnp.concatenate([gdn.astype(BF16), conv.astype(BF16), attn.astype(BF16), gate.astype(BF16), pad],
                           axis=-1)


def _decay_lanes(p):
    return jnp.zeros((1, LANES), F32).at[0, GDN_HEADS:2 * GDN_HEADS].set(p)


def _layer(x2, b, s, layer, w_in_all, w_out_all, w_up_all, w_down_all, norm1_g, conv_w, conv_out_g, q_norm_g,
           k_norm_g, attn_sinks, attn_out_g, gdn_conv_w, gdn_A_log, gdn_dt_bias, gdn_norm_g, norm2_g):
    proj = _in_proj(x2, norm1_g[None, :], w_in_all, layer)
    proj3 = proj.reshape(b, s, D_IN_PAD)
    yc = _conv_mixer(proj3, conv_w, conv_out_g[None, :])
    ya = _swa(proj3, jnp.tile(q_norm_g, 2)[None, :], jnp.tile(k_norm_g, ATTN_KV_HEADS)[None, :], attn_sinks,
              attn_out_g[None, :])
    yg = _gdn(proj3, gdn_conv_w, _decay_lanes(gdn_A_log), _decay_lanes(gdn_dt_bias), gdn_norm_g[None, :])
    m = b * s
    x1, h2 = _out_proj(x2, yc.reshape(m, CONV_WIDTH), ya.reshape(m, ATTN_WIDTH), yg.reshape(m, GDN_WIDTH),
                       w_out_all, layer, norm2_g[None, :])
    return _mlp(x1, h2, w_up_all, w_down_all, layer)


def kernel(x, norm1_g, w_in, conv_w, conv_out_g, q_norm_g, k_norm_g, attn_sinks, attn_out_g, gdn_conv_w,
           gdn_A_log, gdn_dt_bias, gdn_norm_g, w_out, norm2_g, w_up, w_down):
    b, s, d = x.shape
    x2 = x.reshape(b * s, d)
    w_in_all = _regroup_w_in(w_in)
    w_out_all, w_up_all, w_down_all = w_out.astype(BF16), w_up.astype(BF16), w_down.astype(BF16)
    for l in range(norm1_g.shape[0]):
        x2 = _layer(x2, b, s, l, w_in_all, w_out_all, w_up_all, w_down_all, norm1_g[l], conv_w[l], conv_out_g[l],
                    q_norm_g[l], k_norm_g[l], attn_sinks[l], attn_out_g[l], gdn_conv_w[l], gdn_A_log[l],
                    gdn_dt_bias[l], gdn_norm_g[l], norm2_g[l])
    return x2.reshape(b, s, d)
```

```python
import jax
import jax.numpy as jnp
from jax import lax
from jax.experimental import pallas as pl
from jax.experimental.pallas import tpu as pltpu

F32 = jnp.float32
BF16 = jnp.bfloat16
HIGHEST = lax.Precision.HIGHEST

D_MODEL = 2048
CONV_WIDTH = 512
CONV_GROUPS = 4
CONV_GROUP_DIM = 128
ATTN_HEAD_DIM = 64
ATTN_HEADS = 8
ATTN_KV_HEADS = 2
ATTN_WIDTH = 512
ATTN_KV_WIDTH = 128
WINDOW = 128
ATTN_BLOCK = 128
SWA_BATCH = 2
GDN_HEAD_DIM = 128
GDN_WIDTH = 1024
GDN_HEADS = 8
GDN_CONV_K = 4
GDN_CHUNK = 64
GDN_BATCH = 2
D_FF = 8192
EPS = 1e-6

LANES = 128
SUBLANES = 8
VMEM_LIMIT = 56 * 1024 * 1024

GATE_PAD = LANES
IN_PROJ_TN = 13 * LANES
D_IN_PAD = 4 * IN_PROJ_TN
OFF_GQKV = 0
OFF_GZ = 3 * GDN_WIDTH
OFF_CB = 4 * GDN_WIDTH
OFF_CC = OFF_CB + CONV_WIDTH
OFF_CX = OFF_CC + CONV_WIDTH
OFF_AQ = OFF_CX + CONV_WIDTH
OFF_AK = OFF_AQ + ATTN_WIDTH
OFF_AV = OFF_AK + ATTN_KV_WIDTH
OFF_GATE = OFF_AV + ATTN_KV_WIDTH
assert OFF_GATE + GATE_PAD <= D_IN_PAD


def _dot(a, b, precision=None):
    return jnp.dot(a, b, preferred_element_type=F32, precision=precision)


def _dot_nt(a, b, precision=None):
    return lax.dot_general(a, b, (((1,), (1,)), ((), ())), preferred_element_type=F32, precision=precision)


def _dot_tn(a, b, precision=None):
    return lax.dot_general(a, b, (((0,), (0,)), ((), ())), preferred_element_type=F32, precision=precision)


def _sigmoid(x):
    return 0.5 + 0.5 * jnp.tanh(0.5 * x)


def _silu(x):
    hx = 0.5 * x
    return hx + hx * jnp.tanh(hx)


def _softplus(x):
    return jnp.maximum(x, 0.0) + jnp.log(1.0 + jnp.exp(-jnp.abs(x)))


def _in_proj_kernel(x_ref, g_ref, w_ref, o_ref, h_ref):
    @pl.when(pl.program_id(1) == 0)
    def _():
        x = x_ref[...]
        ms = jnp.mean(x * x, axis=-1, keepdims=True)
        h_ref[...] = (x * lax.rsqrt(ms + EPS) * g_ref[...]).astype(BF16)

    o_ref[...] = _dot(h_ref[...], w_ref[...])


def _in_proj(x2, g, w_all, layer, tm=1024, tn=IN_PROJ_TN):
    m = x2.shape[0]
    n = w_all.shape[2]
    return pl.pallas_call(
        _in_proj_kernel,
        grid=(m // tm, n // tn),
        in_specs=[
            pl.BlockSpec((tm, D_MODEL), lambda i, j: (i, 0)),
            pl.BlockSpec((1, D_MODEL), lambda i, j: (0, 0)),
            pl.BlockSpec((None, D_MODEL, tn), lambda i, j: (layer, 0, j)),
        ],
        out_specs=pl.BlockSpec((tm, tn), lambda i, j: (i, j)),
        out_shape=jax.ShapeDtypeStruct((m, n), F32),
        scratch_shapes=[pltpu.VMEM((tm, D_MODEL), BF16)],
        compiler_params=pltpu.CompilerParams(
            dimension_semantics=("parallel", "arbitrary"), vmem_limit_bytes=VMEM_LIMIT),
        name="in_proj",
    )(x2, g, w_all)


def _conv_kernel(cb_ref, cc_ref, cx_ref, w_ref, g_ref, o_ref):
    z = cc_ref[0] * cx_ref[0]
    row = lax.broadcasted_iota(jnp.int32, z.shape, 0)
    z1 = jnp.where(row >= 1, pltpu.roll(z, 1, axis=0), 0.0)
    z2 = jnp.where(row >= 2, pltpu.roll(z, 2, axis=0), 0.0)
    w = w_ref[...]
    y = w[0:1] * z2 + w[1:2] * z1 + w[2:3] * z
    y = cb_ref[0] * y
    ms = jnp.mean(y * y, axis=-1, keepdims=True)
    o_ref[0] = (y * lax.rsqrt(ms + EPS) * g_ref[...]).astype(o_ref.dtype)


def _conv_mixer(proj3, conv_w, conv_out_g):
    b, s, _ = proj3.shape
    gd = CONV_GROUP_DIM

    def col(off):
        return lambda i, g: (i, 0, off // gd + g)

    return pl.pallas_call(
        _conv_kernel,
        grid=(b, CONV_GROUPS),
        in_specs=[
            pl.BlockSpec((1, s, gd), col(OFF_CB)),
            pl.BlockSpec((1, s, gd), col(OFF_CC)),
            pl.BlockSpec((1, s, gd), col(OFF_CX)),
            pl.BlockSpec((3, gd), lambda i, g: (0, g)),
            pl.BlockSpec((1, gd), lambda i, g: (0, g)),
        ],
        out_specs=pl.BlockSpec((1, s, gd), lambda i, g: (i, 0, g)),
        out_shape=jax.ShapeDtypeStruct((b, s, CONV_WIDTH), BF16),
        compiler_params=pltpu.CompilerParams(
            dimension_semantics=("parallel", "parallel"), vmem_limit_bytes=VMEM_LIMIT),
        name="conv_mixer",
    )(proj3, proj3, proj3, conv_w, conv_out_g)


def _swa_kernel(q_ref, k_ref, v_ref, qg_ref, kg_ref, sink_ref, og_ref, o_ref, kk_ref, vv_ref):
    nb, s_len = q_ref.shape[0], q_ref.shape[1]
    blk = ATTN_BLOCK
    hd = ATTN_HEAD_DIM
    nh = ATTN_HEADS
    group = nh // ATTN_KV_HEADS
    pair_w = 2 * hd
    units = [(bb, h) for bb in range(nb) for h in range(nh)]
    nu = range(len(units))

    lane = lax.broadcasted_iota(jnp.int32, (1, pair_w), 1)
    lo = lane < hd
    half_mask = (lo.astype(F32), 1.0 - lo.astype(F32))
    avg = jnp.where(lax.broadcasted_iota(jnp.int32, (pair_w, pair_w), 0) // hd
                    == lax.broadcasted_iota(jnp.int32, (pair_w, pair_w), 1) // hd, 1.0 / hd, 0.0).astype(BF16)

    def half_mean_sq(x):
        x2 = x * x
        hi = x2.astype(BF16)
        rem = (x2 - hi.astype(F32)).astype(BF16)
        return _dot(hi, avg) + _dot(rem, avg)

    zeros = jnp.zeros((blk, pair_w), BF16)
    for bb in range(nb):
        k = k_ref[bb]
        kn = k * lax.rsqrt(half_mean_sq(k) + EPS) * kg_ref[...]
        ksw = pltpu.roll(kn, hd, axis=1)
        v = v_ref[bb]
        vsw = pltpu.roll(v, hd, axis=1)
        for j in range(ATTN_KV_HEADS):
            kk_ref[bb, j, 0:blk, :] = zeros
            vv_ref[bb, j, 0:blk, :] = zeros
        kk_ref[bb, 0, blk:, :] = jnp.where(lo, kn, ksw).astype(BF16)
        kk_ref[bb, 1, blk:, :] = jnp.where(lo, ksw, kn).astype(BF16)
        vv_ref[bb, 0, blk:, :] = jnp.where(lo, v, vsw).astype(BF16)
        vv_ref[bb, 1, blk:, :] = jnp.where(lo, vsw, v).astype(BF16)

    qi = lax.broadcasted_iota(jnp.int32, (blk, 2 * blk), 0)
    si = lax.broadcasted_iota(jnp.int32, (blk, 2 * blk), 1)
    rel = qi + blk - si
    band = (rel >= 0) & (rel < WINDOW)
    cur = si >= blk
    q_gain = [qg_ref[...] * (hd ** -0.5) * half_mask[i] for i in range(2)]

    def body(n, carry):
        r0 = pl.multiple_of(n * blk, blk)
        valid = band & (cur | (n > 0))
        qm = []
        for bb in range(nb):
            for p in range(nh // 2):
                qp = q_ref[bb, pl.ds(r0, blk), p * pair_w:(p + 1) * pair_w]
                qn = qp * lax.rsqrt(half_mean_sq(qp) + EPS)
                qm += [(qn * q_gain[0]).astype(BF16), (qn * q_gain[1]).astype(BF16)]
        s = [jnp.where(valid, _dot_nt(qm[u], kk_ref[bb, h // group, pl.ds(r0, 2 * blk), :]), -jnp.inf)
             for u, (bb, h) in enumerate(units)]
        m = [jnp.maximum(jnp.max(s[u], axis=-1, keepdims=True), sink_ref[h]) for u, (bb, h) in enumerate(units)]
        p_ = [jnp.exp(s[u] - m[u]) for u in nu]
        inv = [1.0 / (jnp.sum(p_[u], axis=-1, keepdims=True) + jnp.exp(sink_ref[h] - m[u]))
               for u, (bb, h) in enumerate(units)]
        o = [_dot(p_[u].astype(BF16), vv_ref[bb, h // group, pl.ds(r0, 2 * blk), :]) * inv[u]
             for u, (bb, h) in enumerate(units)]
        for bb in range(nb):
            outs = []
            for p in range(nh // 2):
                u = bb * nh + 2 * p
                op = jnp.where(lo, o[u], o[u + 1])
                outs.append(op * lax.rsqrt(half_mean_sq(op) + EPS) * og_ref[:, p * pair_w:(p + 1) * pair_w])
            o_ref[bb, pl.ds(r0, blk), :] = jnp.concatenate(outs, axis=-1).astype(o_ref.dtype)
        return carry

    lax.fori_loop(0, s_len // blk, body, 0)


def _swa(proj3, q_g, k_g2, sinks, out_g, nb=SWA_BATCH):
    b, s, _ = proj3.shape
    return pl.pallas_call(
        _swa_kernel,
        grid=(b // nb,),
        in_specs=[
            pl.BlockSpec((nb, s, ATTN_WIDTH), lambda i: (i, 0, OFF_AQ // ATTN_WIDTH)),
            pl.BlockSpec((nb, s, ATTN_KV_WIDTH), lambda i: (i, 0, OFF_AK // ATTN_KV_WIDTH)),
            pl.BlockSpec((nb, s, ATTN_KV_WIDTH), lambda i: (i, 0, OFF_AV // ATTN_KV_WIDTH)),
            pl.BlockSpec((1, 2 * ATTN_HEAD_DIM), lambda i: (0, 0)),
            pl.BlockSpec((1, ATTN_KV_WIDTH), lambda i: (0, 0)),
            pl.BlockSpec(memory_space=pltpu.SMEM),
            pl.BlockSpec((1, ATTN_WIDTH), lambda i: (0, 0)),
        ],
        out_specs=pl.BlockSpec((nb, s, ATTN_WIDTH), lambda i: (i, 0, 0)),
        out_shape=jax.ShapeDtypeStruct((b, s, ATTN_WIDTH), BF16),
        scratch_shapes=[pltpu.VMEM((nb, ATTN_KV_HEADS, s + ATTN_BLOCK, 2 * ATTN_HEAD_DIM), BF16),
                        pltpu.VMEM((nb, ATTN_KV_HEADS, s + ATTN_BLOCK, 2 * ATTN_HEAD_DIM), BF16)],
        compiler_params=pltpu.CompilerParams(
            dimension_semantics=("parallel",), vmem_limit_bytes=VMEM_LIMIT),
        name="swa",
    )(proj3, proj3, proj3, q_g, k_g2, sinks, out_g)


def _gdn_kernel(qkv_ref, z_ref, gate_ref, cw_ref, alog_ref, dtb_ref, ng_ref, o_ref, halo_ref, state_ref):
    nb, ts = qkv_ref.shape[0], qkv_ref.shape[1]
    c = GDN_CHUNK
    hd = GDN_HEAD_DIM
    nh = GDN_HEADS
    halo = SUBLANES
    units = [(bb, h) for bb in range(nb) for h in range(nh)]
    nu = range(len(units))

    @pl.when(pl.program_id(1) == 0)
    def _():
        state_ref[...] = jnp.zeros(state_ref.shape, F32)
        halo_ref[...] = jnp.zeros(halo_ref.shape, F32)

    row = lax.broadcasted_iota(jnp.int32, (c, c), 0)
    col = lax.broadcasted_iota(jnp.int32, (c, c), 1)
    tril = row >= col
    strict = row > col
    blk16 = (row // 16) == (col // 16)
    blk32 = (row // 32) == (col // 32)
    tril_f = tril.astype(F32)
    sel = (lax.broadcasted_iota(jnp.int32, (2 * nh, LANES), 0)
           == lax.broadcasted_iota(jnp.int32, (2 * nh, LANES), 1)).astype(F32)
    is_beta = lax.broadcasted_iota(jnp.int32, (1, LANES), 1) < nh
    neg_a = -jnp.exp(alog_ref[...])
    dtb = dtb_ref[...]

    def chunk(ci, carry):
        t0 = pl.multiple_of(ci * c, c)
        tp = pl.multiple_of(jnp.maximum(t0 - halo, 0), halo)
        first = ci == 0

        def conv_silu(bb, lo):
            w = cw_ref[:, lo:lo + hd]
            prev = jnp.where(first, halo_ref[bb, :, lo:lo + hd], qkv_ref[bb, pl.ds(tp, halo), lo:lo + hd])
            win = jnp.concatenate([prev, qkv_ref[bb, pl.ds(t0, c), lo:lo + hd]], axis=0)
            y = w[GDN_CONV_K - 1:GDN_CONV_K] * win[halo:]
            for k in range(1, GDN_CONV_K):
                y = y + w[GDN_CONV_K - 1 - k:GDN_CONV_K - k] * pltpu.roll(win, k, axis=0)[halo:]
            return _silu(y)

        gmix, gc_all, rows = [], [], []
        for bb in range(nb):
            gl = gate_ref[bb, pl.ds(t0, c), :]
            gm = jnp.where(is_beta, _sigmoid(gl), neg_a * _softplus(gl + dtb))
            ga = _dot(tril_f, gm, HIGHEST)
            gmix.append(gm)
            gc_all.append(ga)
            rows.append(_dot_nt(sel, jnp.concatenate([gm, ga], axis=0), HIGHEST))

        qs, ks, vs = [], [], []
        for bb, h in units:
            q = conv_silu(bb, h * hd)
            k = conv_silu(bb, GDN_WIDTH + h * hd)
            vs.append(conv_silu(bb, 2 * GDN_WIDTH + h * hd))
            qs.append(q * (lax.rsqrt(jnp.sum(q * q, axis=-1, keepdims=True) + EPS) * (hd ** -0.5)))
            ks.append(k * lax.rsqrt(jnp.sum(k * k, axis=-1, keepdims=True) + EPS))

        kkqk = []
        for u in nu:
            kb = ks[u].astype(BF16)
            kkqk.append(_dot_nt(jnp.concatenate([kb, qs[u].astype(BF16)], axis=0), kb))
        beta = [gmix[bb][:, h:h + 1] for bb, h in units]
        gc = [gc_all[bb][:, nh + h:nh + h + 1] for bb, h in units]
        g_last = [gc_all[bb][c - 1:c, nh + h:nh + h + 1] for bb, h in units]
        beta_row = [rows[bb][h:h + 1, :c] for bb, h in units]
        es, ps, qkb, off_diag = [], [], [], []
        for u, (bb, h) in enumerate(units):
            gc_row = rows[bb][nh + h:nh + h + 1, c:]
            decay = jnp.exp(jnp.where(tril, gc[u] - gc_row, 0.0))
            a = jnp.where(strict, kkqk[u][:c] * beta[u] * decay, 0.0)
            qkb.append(jnp.where(tril, kkqk[u][c:] * decay, 0.0).astype(BF16))
            a_d = jnp.where(blk16, a, 0.0)
            es.append(-a_d)
            ps.append(a_d)
            off_diag.append((jnp.where(blk32, a - a_d, 0.0), jnp.where(blk32, 0.0, a)))
        ps = [_dot(ps[u].astype(BF16), ps[u].astype(BF16)) for u in nu]
        for i in range(3):
            if i < 2:
                ep = [_dot(jnp.concatenate([es[u], ps[u]], axis=0).astype(BF16), ps[u].astype(BF16)) for u in nu]
                es = [es[u] + ps[u] + ep[u][:c] for u in nu]
                ps = [ep[u][c:] for u in nu]
            else:
                ep = [_dot(es[u].astype(BF16), ps[u].astype(BF16)) for u in nu]
                es = [es[u] + ps[u] + ep[u] for u in nu]
        for level in range(2):
            ls = [off_diag[u][level] for u in nu]
            g1 = [ls[u] + _dot(es[u].astype(BF16), ls[u].astype(BF16)) for u in nu]
            es = [es[u] - (g1[u] + _dot(g1[u].astype(BF16), es[u].astype(BF16))) for u in nu]
        egc = [jnp.exp(gc[u]) for u in nu]
        wu = []
        for u in nu:
            kv = jnp.concatenate([ks[u] * egc[u], vs[u]], axis=-1)
            e_b = (es[u] * beta_row[u]).astype(BF16)
            wu.append(kv * beta[u] + _dot(e_b, kv.astype(BF16)))
        st = [state_ref[bb, h] for bb, h in units]
        ws_qs = [_dot(jnp.concatenate([wu[u][:, :hd], qs[u] * egc[u]], axis=0).astype(BF16), st[u].astype(BF16))
                 for u in nu]
        vb = [(wu[u][:, hd:] - ws_qs[u][:c]).astype(BF16) for u in nu]
        kd = [(ks[u] * jnp.exp(g_last[u] - gc[u])).astype(BF16) for u in nu]
        os_ = [ws_qs[u][c:] + _dot(qkb[u], vb[u]) for u in nu]
        for u, (bb, h) in enumerate(units):
            state_ref[bb, h] = st[u] * jnp.exp(g_last[u]) + _dot_tn(kd[u], vb[u])
        for u, (bb, h) in enumerate(units):
            o = os_[u]
            oms = jnp.mean(o * o, axis=-1, keepdims=True)
            zh = z_ref[bb, pl.ds(t0, c), h * hd:(h + 1) * hd]
            y = o * lax.rsqrt(oms + EPS) * ng_ref[...] * _silu(zh)
            o_ref[bb, pl.ds(t0, c), h * hd:(h + 1) * hd] = y.astype(o_ref.dtype)
        return carry

    lax.fori_loop(0, ts // c, chunk, 0)
    halo_ref[...] = qkv_ref[:, ts - halo:ts, :]


def _gdn(proj3, conv_w, alog_b, dtb_b, norm_g, ts=256, nb=GDN_BATCH):
    b, s, _ = proj3.shape
    qkv_w = 3 * GDN_WIDTH
    return pl.pallas_call(
        _gdn_kernel,
        grid=(b // nb, s // ts),
        in_specs=[
            pl.BlockSpec((nb, ts, qkv_w), lambda i, t: (i, t, OFF_GQKV // qkv_w)),
            pl.BlockSpec((nb, ts, GDN_WIDTH), lambda i, t: (i, t, OFF_GZ // GDN_WIDTH)),
            pl.BlockSpec((nb, ts, GATE_PAD), lambda i, t: (i, t, OFF_GATE // GATE_PAD)),
            pl.BlockSpec((GDN_CONV_K, qkv_w), lambda i, t: (0, 0)),
            pl.BlockSpec((1, LANES), lambda i, t: (0, 0)),
            pl.BlockSpec((1, LANES), lambda i, t: (0, 0)),
            pl.BlockSpec((1, GDN_HEAD_DIM), lambda i, t: (0, 0)),
        ],
        out_specs=pl.BlockSpec((nb, ts, GDN_WIDTH), lambda i, t: (i, t, 0)),
        out_shape=jax.ShapeDtypeStruct((b, s, GDN_WIDTH), BF16),
        scratch_shapes=[pltpu.VMEM((nb, SUBLANES, qkv_w), F32),
                        pltpu.VMEM((nb, GDN_HEADS, GDN_HEAD_DIM, GDN_HEAD_DIM), F32)],
        compiler_params=pltpu.CompilerParams(
            dimension_semantics=("parallel", "arbitrary"), vmem_limit_bytes=VMEM_LIMIT),
        name="gdn",
    )(proj3, proj3, proj3, conv_w, alog_b, dtb_b, norm_g)


def _out_proj_kernel(x_ref, yc_ref, ya_ref, yg_ref, w_ref, g_ref, x1_ref, h_ref):
    acc = _dot(yc_ref[...], w_ref[0:CONV_WIDTH, :])
    acc = acc + _dot(ya_ref[...], w_ref[CONV_WIDTH:CONV_WIDTH + ATTN_WIDTH, :])
    acc = acc + _dot(yg_ref[...], w_ref[CONV_WIDTH + ATTN_WIDTH:, :])
    x1 = x_ref[...] + acc
    x1_ref[...] = x1
    ms = jnp.mean(x1 * x1, axis=-1, keepdims=True)
    h_ref[...] = (x1 * lax.rsqrt(ms + EPS) * g_ref[...]).astype(BF16)


def _out_proj(x2, yc, ya, yg, w_all, layer, g, tm=512):
    m = x2.shape[0]
    return pl.pallas_call(
        _out_proj_kernel,
        grid=(m // tm,),
        in_specs=[
            pl.BlockSpec((tm, D_MODEL), lambda i: (i, 0)),
            pl.BlockSpec((tm, CONV_WIDTH), lambda i: (i, 0)),
            pl.BlockSpec((tm, ATTN_WIDTH), lambda i: (i, 0)),
            pl.BlockSpec((tm, GDN_WIDTH), lambda i: (i, 0)),
            pl.BlockSpec((None, D_MODEL, D_MODEL), lambda i: (layer, 0, 0)),
            pl.BlockSpec((1, D_MODEL), lambda i: (0, 0)),
        ],
        out_specs=[pl.BlockSpec((tm, D_MODEL), lambda i: (i, 0)),
                   pl.BlockSpec((tm, D_MODEL), lambda i: (i, 0))],
        out_shape=[jax.ShapeDtypeStruct((m, D_MODEL), F32),
                   jax.ShapeDtypeStruct((m, D_MODEL), BF16)],
        compiler_params=pltpu.CompilerParams(
            dimension_semantics=("parallel",), vmem_limit_bytes=VMEM_LIMIT),
        name="out_proj",
    )(x2, yc, ya, yg, w_all, g)


def _mlp_kernel(x1_ref, h_ref, wu_ref, wd_ref, o_ref):
    @pl.when(pl.program_id(1) == 0)
    def _():
        o_ref[...] = x1_ref[...]

    hid = jnp.maximum(_dot(h_ref[...], wu_ref[...]), 0.0)
    hid = (hid * hid).astype(BF16)
    o_ref[...] += _dot(hid, wd_ref[...])


def _mlp(x1, h, w_up_all, w_down_all, layer, tm=512, tf=2048):
    m = x1.shape[0]
    return pl.pallas_call(
        _mlp_kernel,
        grid=(m // tm, D_FF // tf),
        in_specs=[
            pl.BlockSpec((tm, D_MODEL), lambda i, f: (i, 0)),
            pl.BlockSpec((tm, D_MODEL), lambda i, f: (i, 0)),
            pl.BlockSpec((None, D_MODEL, tf), lambda i, f: (layer, 0, f)),
            pl.BlockSpec((None, tf, D_MODEL), lambda i, f: (layer, f, 0)),
        ],
        out_specs=pl.BlockSpec((tm, D_MODEL), lambda i, f: (i, 0)),
        out_shape=jax.ShapeDtypeStruct((m, D_MODEL), F32),
        compiler_params=pltpu.CompilerParams(
            dimension_semantics=("parallel", "arbitrary"), vmem_limit_bytes=VMEM_LIMIT),
        name="mlp",
    )(x1, h, w_up_all, w_down_all)


def _regroup_w_in(w):
    n_conv = 3 * CONV_WIDTH
    n_attn = ATTN_WIDTH + 2 * ATTN_KV_WIDTH
    n_gdn = 4 * GDN_WIDTH
    w = w.astype(BF16)
    conv = w[..., :n_conv]
    attn = w[..., n_conv:n_conv + n_attn]
    gdn = w[..., n_conv + n_attn:n_conv + n_attn + n_gdn]
    gate = w[..., n_conv + n_attn + n_gdn:]
    pad = jnp.zeros(w.shape[:-1] + (D_IN_PAD - w.shape[-1],), BF16)
    return jnp.concatenate([gdn, conv, attn, gate, pad], axis=-1)


def _decay_lanes(p):
    return jnp.zeros((1, LANES), F32).at[0, GDN_HEADS:2 * GDN_HEADS].set(p)


def _layer(x2, b, s, layer, w_in_all, w_out_all, w_up_all, w_down_all, norm1_g, conv_w, conv_out_g, q_norm_g,
           k_norm_g, attn_sinks, attn_out_g, gdn_conv_w, gdn_A_log, gdn_dt_bias, gdn_norm_g, norm2_g):
    proj = _in_proj(x2, norm1_g[None, :], w_in_all, layer)
    proj3 = proj.reshape(b, s, D_IN_PAD)
    yc = _conv_mixer(proj3, conv_w, conv_out_g[None, :])
    ya = _swa(proj3, jnp.tile(q_norm_g, 2)[None, :], jnp.tile(k_norm_g, ATTN_KV_HEADS)[None, :], attn_sinks,
              attn_out_g[None, :])
    yg = _gdn(proj3, gdn_conv_w, _decay_lanes(gdn_A_log), _decay_lanes(gdn_dt_bias), gdn_norm_g[None, :])
    m = b * s
    x1, h2 = _out_proj(x2, yc.reshape(m, CONV_WIDTH), ya.reshape(m, ATTN_WIDTH), yg.reshape(m, GDN_WIDTH),
                       w_out_all, layer, norm2_g[None, :])
    return _mlp(x1, h2, w_up_all, w_down_all, layer)


def kernel(x, norm1_g, w_in, conv_w, conv_out_g, q_norm_g, k_norm_g, attn_sinks, attn_out_g, gdn_conv_w,
           gdn_A_log, gdn_dt_bias, gdn_norm_g, w_out, norm2_g, w_up, w_down):
    b, s, d = x.shape
    x2 = x.reshape(b * s, d)
    w_in_all = _regroup_w_in(w_in)
    w_out_all, w_up_all, w_down_all = w_out.astype(BF16), w_up.astype(BF16), w_down.astype(BF16)
    for l in range(norm1_g.shape[0]):
        x2 = _layer(x2, b, s, l, w_in_all, w_out_all, w_up_all, w_down_all, norm1_g[l], conv_w[l], conv_out_g[l],
                    q_norm_g[l], k_norm_g[l], attn_sinks[l], attn_out_g[l], gdn_conv_w[l], gdn_A_log[l],
                    gdn_dt_bias[l], gdn_norm_g[l], norm2_g[l])
    return x2.reshape(b, s, d)
```

```python
import jax
import jax.numpy as jnp
from jax import lax
from jax.experimental import pallas as pl
from jax.experimental.pallas import tpu as pltpu

F32 = jnp.float32
BF16 = jnp.bfloat16
HIGHEST = lax.Precision.HIGHEST

D_MODEL = 2048
CONV_WIDTH = 512
CONV_GROUPS = 4
CONV_GROUP_DIM = 128
ATTN_HEAD_DIM = 64
ATTN_HEADS = 8
ATTN_KV_HEADS = 2
ATTN_WIDTH = 512
ATTN_KV_WIDTH = 128
WINDOW = 128
ATTN_BLOCK = 128
SWA_BATCH = 2
GDN_HEAD_DIM = 128
GDN_WIDTH = 1024
GDN_HEADS = 8
GDN_CONV_K = 4
GDN_CHUNK = 64
GDN_BATCH = 2
D_FF = 8192
EPS = 1e-6

LANES = 128
SUBLANES = 8
VMEM_LIMIT = 56 * 1024 * 1024

GATE_PAD = LANES
MXU_WIDTH = 256
REGROUP_BLOCK = MXU_WIDTH
D_IN_MAIN = 4 * GDN_WIDTH + 3 * CONV_WIDTH + ATTN_WIDTH + 2 * ATTN_KV_WIDTH
IN_PROJ_TN = 5 * MXU_WIDTH
assert D_IN_MAIN % IN_PROJ_TN == 0
OFF_GQKV = 0
OFF_GZ = 3 * GDN_WIDTH
OFF_CB = 4 * GDN_WIDTH
OFF_CC = OFF_CB + CONV_WIDTH
OFF_CX = OFF_CC + CONV_WIDTH
OFF_AQ = OFF_CX + CONV_WIDTH
OFF_AK = OFF_AQ + ATTN_WIDTH
OFF_AV = OFF_AK + ATTN_KV_WIDTH
assert OFF_AV + ATTN_KV_WIDTH == D_IN_MAIN


def _dot(a, b, precision=None):
    return jnp.dot(a, b, preferred_element_type=F32, precision=precision)


def _dot_nt(a, b, precision=None):
    return lax.dot_general(a, b, (((1,), (1,)), ((), ())), preferred_element_type=F32, precision=precision)


def _dot_tn(a, b, precision=None):
    return lax.dot_general(a, b, (((0,), (0,)), ((), ())), preferred_element_type=F32, precision=precision)


def _sigmoid(x):
    return 0.5 + 0.5 * jnp.tanh(0.5 * x)


def _silu(x):
    hx = 0.5 * x
    return hx + hx * jnp.tanh(hx)


def _softplus(x):
    return jnp.maximum(x, 0.0) + jnp.log(1.0 + jnp.exp(-jnp.abs(x)))


def _regroup_kernel(w_ref, o_ref):
    o_ref[...] = w_ref[...].astype(o_ref.dtype)


def _regroup_w_in(w_in):
    blk = REGROUP_BLOCK
    n_front = (3 * CONV_WIDTH + ATTN_WIDTH + 2 * ATTN_KV_WIDTH) // blk
    n_gdn = 4 * GDN_WIDTH // blk
    layers = w_in.shape[0]

    def src_block(l, j):
        return (l, 0, jnp.where(j < n_gdn, j + n_front, j - n_gdn))

    return pl.pallas_call(
        _regroup_kernel,
        grid=(layers, D_IN_MAIN // blk),
        in_specs=[pl.BlockSpec((None, D_MODEL, blk), src_block)],
        out_specs=pl.BlockSpec((None, D_MODEL, blk), lambda l, j: (l, 0, j)),
        out_shape=jax.ShapeDtypeStruct((layers, D_MODEL, D_IN_MAIN), BF16),
        compiler_params=pltpu.CompilerParams(
            dimension_semantics=("parallel", "parallel"), vmem_limit_bytes=VMEM_LIMIT),
        name="regroup_w_in",
    )(w_in)


def _in_proj_kernel(x_ref, g_ref, w_ref, wg_ref, o_ref, og_ref, h_ref):
    @pl.when(pl.program_id(1) == 0)
    def _():
        x = x_ref[...]
        ms = jnp.mean(x * x, axis=-1, keepdims=True)
        h = (x * lax.rsqrt(ms + EPS) * g_ref[...]).astype(BF16)
        h_ref[...] = h
        og_ref[...] = _dot(h, wg_ref[...])

    o_ref[...] = _dot(h_ref[...], w_ref[...])


def _in_proj(x2, g, w_all, wg_all, layer, tm=1024, tn=IN_PROJ_TN):
    m = x2.shape[0]
    n = w_all.shape[2]
    return pl.pallas_call(
        _in_proj_kernel,
        grid=(m // tm, n // tn),
        in_specs=[
            pl.BlockSpec((tm, D_MODEL), lambda i, j: (i, 0)),
            pl.BlockSpec((1, D_MODEL), lambda i, j: (0, 0)),
            pl.BlockSpec((None, D_MODEL, tn), lambda i, j: (layer, 0, j)),
            pl.BlockSpec((None, D_MODEL, GATE_PAD), lambda i, j: (layer, 0, 0)),
        ],
        out_specs=[pl.BlockSpec((tm, tn), lambda i, j: (i, j)),
                   pl.BlockSpec((tm, GATE_PAD), lambda i, j: (i, 0))],
        out_shape=[jax.ShapeDtypeStruct((m, n), F32),
                   jax.ShapeDtypeStruct((m, GATE_PAD), F32)],
        scratch_shapes=[pltpu.VMEM((tm, D_MODEL), BF16)],
        compiler_params=pltpu.CompilerParams(
            dimension_semantics=("parallel", "arbitrary"), vmem_limit_bytes=VMEM_LIMIT),
        name="in_proj",
    )(x2, g, w_all, wg_all)


def _conv_kernel(cb_ref, cc_ref, cx_ref, w_ref, g_ref, o_ref):
    z = cc_ref[0] * cx_ref[0]
    row = lax.broadcasted_iota(jnp.int32, z.shape, 0)
    z1 = jnp.where(row >= 1, pltpu.roll(z, 1, axis=0), 0.0)
    z2 = jnp.where(row >= 2, pltpu.roll(z, 2, axis=0), 0.0)
    w = w_ref[...]
    y = w[0:1] * z2 + w[1:2] * z1 + w[2:3] * z
    y = cb_ref[0] * y
    ms = jnp.mean(y * y, axis=-1, keepdims=True)
    o_ref[0] = (y * lax.rsqrt(ms + EPS) * g_ref[...]).astype(o_ref.dtype)


def _conv_mixer(proj3, conv_w, conv_out_g):
    b, s, _ = proj3.shape
    gd = CONV_GROUP_DIM

    def col(off):
        return lambda i, g: (i, 0, off // gd + g)

    return pl.pallas_call(
        _conv_kernel,
        grid=(b, CONV_GROUPS),
        in_specs=[
            pl.BlockSpec((1, s, gd), col(OFF_CB)),
            pl.BlockSpec((1, s, gd), col(OFF_CC)),
            pl.BlockSpec((1, s, gd), col(OFF_CX)),
            pl.BlockSpec((3, gd), lambda i, g: (0, g)),
            pl.BlockSpec((1, gd), lambda i, g: (0, g)),
        ],
        out_specs=pl.BlockSpec((1, s, gd), lambda i, g: (i, 0, g)),
        out_shape=jax.ShapeDtypeStruct((b, s, CONV_WIDTH), BF16),
        compiler_params=pltpu.CompilerParams(
            dimension_semantics=("parallel", "parallel"), vmem_limit_bytes=VMEM_LIMIT),
        name="conv_mixer",
    )(proj3, proj3, proj3, conv_w, conv_out_g)


def _swa_kernel(q_ref, k_ref, v_ref, qg_ref, kg_ref, sink_ref, og_ref, o_ref, kk_ref, vv_ref):
    nb, s_len = q_ref.shape[0], q_ref.shape[1]
    blk = ATTN_BLOCK
    hd = ATTN_HEAD_DIM
    nh = ATTN_HEADS
    group = nh // ATTN_KV_HEADS
    pair_w = 2 * hd
    units = [(bb, h) for bb in range(nb) for h in range(nh)]
    nu = range(len(units))

    lane = lax.broadcasted_iota(jnp.int32, (1, pair_w), 1)
    lo = lane < hd
    half_mask = (lo.astype(F32), 1.0 - lo.astype(F32))
    avg = jnp.where(lax.broadcasted_iota(jnp.int32, (pair_w, pair_w), 0) // hd
                    == lax.broadcasted_iota(jnp.int32, (pair_w, pair_w), 1) // hd, 1.0 / hd, 0.0).astype(BF16)

    def half_mean_sq(x):
        x2 = x * x
        hi = x2.astype(BF16)
        rem = (x2 - hi.astype(F32)).astype(BF16)
        return _dot(hi, avg) + _dot(rem, avg)

    zeros = jnp.zeros((blk, pair_w), BF16)
    for bb in range(nb):
        k = k_ref[bb]
        kn = k * lax.rsqrt(half_mean_sq(k) + EPS) * kg_ref[...]
        ksw = pltpu.roll(kn, hd, axis=1)
        v = v_ref[bb]
        vsw = pltpu.roll(v, hd, axis=1)
        for j in range(ATTN_KV_HEADS):
            kk_ref[bb, j, 0:blk, :] = zeros
            vv_ref[bb, j, 0:blk, :] = zeros
        kk_ref[bb, 0, blk:, :] = jnp.where(lo, kn, ksw).astype(BF16)
        kk_ref[bb, 1, blk:, :] = jnp.where(lo, ksw, kn).astype(BF16)
        vv_ref[bb, 0, blk:, :] = jnp.where(lo, v, vsw).astype(BF16)
        vv_ref[bb, 1, blk:, :] = jnp.where(lo, vsw, v).astype(BF16)

    qi = lax.broadcasted_iota(jnp.int32, (blk, 2 * blk), 0)
    si = lax.broadcasted_iota(jnp.int32, (blk, 2 * blk), 1)
    rel = qi + blk - si
    band = (rel >= 0) & (rel < WINDOW)
    cur = si >= blk
    q_gain = [qg_ref[...] * (hd ** -0.5) * half_mask[i] for i in range(2)]

    def body(n, carry):
        r0 = pl.multiple_of(n * blk, blk)
        valid = band & (cur | (n > 0))
        qm = []
        for bb in range(nb):
            for p in range(nh // 2):
                qp = q_ref[bb, pl.ds(r0, blk), p * pair_w:(p + 1) * pair_w]
                qn = qp * lax.rsqrt(half_mean_sq(qp) + EPS)
                qm += [(qn * q_gain[0]).astype(BF16), (qn * q_gain[1]).astype(BF16)]
        s = [jnp.where(valid, _dot_nt(qm[u], kk_ref[bb, h // group, pl.ds(r0, 2 * blk), :]), -jnp.inf)
             for u, (bb, h) in enumerate(units)]
        m = [jnp.maximum(jnp.max(s[u], axis=-1, keepdims=True), sink_ref[h]) for u, (bb, h) in enumerate(units)]
        p_ = [jnp.exp(s[u] - m[u]) for u in nu]
        inv = [1.0 / (jnp.sum(p_[u], axis=-1, keepdims=True) + jnp.exp(sink_ref[h] - m[u]))
               for u, (bb, h) in enumerate(units)]
        o = [_dot(p_[u].astype(BF16), vv_ref[bb, h // group, pl.ds(r0, 2 * blk), :]) * inv[u]
             for u, (bb, h) in enumerate(units)]
        for bb in range(nb):
            outs = []
            for p in range(nh // 2):
                u = bb * nh + 2 * p
                op = jnp.where(lo, o[u], o[u + 1])
                outs.append(op * lax.rsqrt(half_mean_sq(op) + EPS) * og_ref[:, p * pair_w:(p + 1) * pair_w])
            o_ref[bb, pl.ds(r0, blk), :] = jnp.concatenate(outs, axis=-1).astype(o_ref.dtype)
        return carry

    lax.fori_loop(0, s_len // blk, body, 0)


def _swa(proj3, q_g, k_g2, sinks, out_g, nb=SWA_BATCH):
    b, s, _ = proj3.shape
    return pl.pallas_call(
        _swa_kernel,
        grid=(b // nb,),
        in_specs=[
            pl.BlockSpec((nb, s, ATTN_WIDTH), lambda i: (i, 0, OFF_AQ // ATTN_WIDTH)),
            pl.BlockSpec((nb, s, ATTN_KV_WIDTH), lambda i: (i, 0, OFF_AK // ATTN_KV_WIDTH)),
            pl.BlockSpec((nb, s, ATTN_KV_WIDTH), lambda i: (i, 0, OFF_AV // ATTN_KV_WIDTH)),
            pl.BlockSpec((1, 2 * ATTN_HEAD_DIM), lambda i: (0, 0)),
            pl.BlockSpec((1, ATTN_KV_WIDTH), lambda i: (0, 0)),
            pl.BlockSpec(memory_space=pltpu.SMEM),
            pl.BlockSpec((1, ATTN_WIDTH), lambda i: (0, 0)),
        ],
        out_specs=pl.BlockSpec((nb, s, ATTN_WIDTH), lambda i: (i, 0, 0)),
        out_shape=jax.ShapeDtypeStruct((b, s, ATTN_WIDTH), BF16),
        scratch_shapes=[pltpu.VMEM((nb, ATTN_KV_HEADS, s + ATTN_BLOCK, 2 * ATTN_HEAD_DIM), BF16),
                        pltpu.VMEM((nb, ATTN_KV_HEADS, s + ATTN_BLOCK, 2 * ATTN_HEAD_DIM), BF16)],
        compiler_params=pltpu.CompilerParams(
            dimension_semantics=("parallel",), vmem_limit_bytes=VMEM_LIMIT),
        name="swa",
    )(proj3, proj3, proj3, q_g, k_g2, sinks, out_g)


def _gdn_kernel(qkv_ref, z_ref, gate_ref, cw_ref, alog_ref, dtb_ref, ng_ref, o_ref, halo_ref, state_ref):
    nb, ts = qkv_ref.shape[0], qkv_ref.shape[1]
    c = GDN_CHUNK
    hd = GDN_HEAD_DIM
    nh = GDN_HEADS
    halo = SUBLANES
    units = [(bb, h) for bb in range(nb) for h in range(nh)]
    nu = range(len(units))

    @pl.when(pl.program_id(1) == 0)
    def _():
        state_ref[...] = jnp.zeros(state_ref.shape, F32)
        halo_ref[...] = jnp.zeros(halo_ref.shape, F32)

    row = lax.broadcasted_iota(jnp.int32, (c, c), 0)
    col = lax.broadcasted_iota(jnp.int32, (c, c), 1)
    tril = row >= col
    strict = row > col
    blk16 = (row // 16) == (col // 16)
    blk32 = (row // 32) == (col // 32)
    tril_f = tril.astype(F32)
    sel = (lax.broadcasted_iota(jnp.int32, (2 * nh, LANES), 0)
           == lax.broadcasted_iota(jnp.int32, (2 * nh, LANES), 1)).astype(F32)
    is_beta = lax.broadcasted_iota(jnp.int32, (1, LANES), 1) < nh
    neg_a = -jnp.exp(alog_ref[...])
    dtb = dtb_ref[...]

    def chunk(ci, carry):
        t0 = pl.multiple_of(ci * c, c)
        tp = pl.multiple_of(jnp.maximum(t0 - halo, 0), halo)
        first = ci == 0

        def conv_silu(bb, lo):
            w = cw_ref[:, lo:lo + hd]
            prev = jnp.where(first, halo_ref[bb, :, lo:lo + hd], qkv_ref[bb, pl.ds(tp, halo), lo:lo + hd])
            win = jnp.concatenate([prev, qkv_ref[bb, pl.ds(t0, c), lo:lo + hd]], axis=0)
            y = w[GDN_CONV_K - 1:GDN_CONV_K] * win[halo:]
            for k in range(1, GDN_CONV_K):
                y = y + w[GDN_CONV_K - 1 - k:GDN_CONV_K - k] * pltpu.roll(win, k, axis=0)[halo:]
            return _silu(y)

        gmix, gc_all, rows = [], [], []
        for bb in range(nb):
            gl = gate_ref[bb, pl.ds(t0, c), :]
            gm = jnp.where(is_beta, _sigmoid(gl), neg_a * _softplus(gl + dtb))
            ga = _dot(tril_f, gm, HIGHEST)
            gmix.append(gm)
            gc_all.append(ga)
            rows.append(_dot_nt(sel, jnp.concatenate([gm, ga], axis=0), HIGHEST))

        qs, ks, vs = [], [], []
        for bb, h in units:
            q = conv_silu(bb, h * hd)
            k = conv_silu(bb, GDN_WIDTH + h * hd)
            vs.append(conv_silu(bb, 2 * GDN_WIDTH + h * hd))
            qs.append(q * (lax.rsqrt(jnp.sum(q * q, axis=-1, keepdims=True) + EPS) * (hd ** -0.5)))
            ks.append(k * lax.rsqrt(jnp.sum(k * k, axis=-1, keepdims=True) + EPS))

        kkqk = []
        for u in nu:
            kb = ks[u].astype(BF16)
            kkqk.append(_dot_nt(jnp.concatenate([kb, qs[u].astype(BF16)], axis=0), kb))
        beta = [gmix[bb][:, h:h + 1] for bb, h in units]
        gc = [gc_all[bb][:, nh + h:nh + h + 1] for bb, h in units]
        g_last = [gc_all[bb][c - 1:c, nh + h:nh + h + 1] for bb, h in units]
        beta_row = [rows[bb][h:h + 1, :c] for bb, h in units]
        es, ps, qkb, off_diag = [], [], [], []
        for u, (bb, h) in enumerate(units):
            gc_row = rows[bb][nh + h:nh + h + 1, c:]
            decay = jnp.exp(jnp.where(tril, gc[u] - gc_row, 0.0))
            a = jnp.where(strict, kkqk[u][:c] * beta[u] * decay, 0.0)
            qkb.append(jnp.where(tril, kkqk[u][c:] * decay, 0.0).astype(BF16))
            a_d = jnp.where(blk16, a, 0.0)
            es.append(-a_d)
            ps.append(a_d)
            off_diag.append((jnp.where(blk32, a - a_d, 0.0), jnp.where(blk32, 0.0, a)))
        ps = [_dot(ps[u].astype(BF16), ps[u].astype(BF16)) for u in nu]
        for i in range(3):
            if i < 2:
                ep = [_dot(jnp.concatenate([es[u], ps[u]], axis=0).astype(BF16), ps[u].astype(BF16)) for u in nu]
                es = [es[u] + ps[u] + ep[u][:c] for u in nu]
                ps = [ep[u][c:] for u in nu]
            else:
                ep = [_dot(es[u].astype(BF16), ps[u].astype(BF16)) for u in nu]
                es = [es[u] + ps[u] + ep[u] for u in nu]
        for level in range(2):
            ls = [off_diag[u][level] for u in nu]
            g1 = [ls[u] + _dot(es[u].astype(BF16), ls[u].astype(BF16)) for u in nu]
            es = [es[u] - (g1[u] + _dot(g1[u].astype(BF16), es[u].astype(BF16))) for u in nu]
        egc = [jnp.exp(gc[u]) for u in nu]
        wu = []
        for u in nu:
            kv = jnp.concatenate([ks[u] * egc[u], vs[u]], axis=-1)
            e_b = (es[u] * beta_row[u]).astype(BF16)
            wu.append(kv * beta[u] + _dot(e_b, kv.astype(BF16)))
        st = [state_ref[bb, h] for bb, h in units]
        ws_qs = [_dot(jnp.concatenate([wu[u][:, :hd], qs[u] * egc[u]], axis=0).astype(BF16), st[u].astype(BF16))
                 for u in nu]
        vb = [(wu[u][:, hd:] - ws_qs[u][:c]).astype(BF16) for u in nu]
        kd = [(ks[u] * jnp.exp(g_last[u] - gc[u])).astype(BF16) for u in nu]
        os_ = [ws_qs[u][c:] + _dot(qkb[u], vb[u]) for u in nu]
        for u, (bb, h) in enumerate(units):
            state_ref[bb, h] = st[u] * jnp.exp(g_last[u]) + _dot_tn(kd[u], vb[u])
        for u, (bb, h) in enumerate(units):
            o = os_[u]
            oms = jnp.mean(o * o, axis=-1, keepdims=True)
            zh = z_ref[bb, pl.ds(t0, c), h * hd:(h + 1) * hd]
            y = o * lax.rsqrt(oms + EPS) * ng_ref[...] * _silu(zh)
            o_ref[bb, pl.ds(t0, c), h * hd:(h + 1) * hd] = y.astype(o_ref.dtype)
        return carry

    lax.fori_loop(0, ts // c, chunk, 0)
    halo_ref[...] = qkv_ref[:, ts - halo:ts, :]


def _gdn(proj3, gate3, conv_w, alog_b, dtb_b, norm_g, ts=256, nb=GDN_BATCH):
    b, s, _ = proj3.shape
    qkv_w = 3 * GDN_WIDTH
    return pl.pallas_call(
        _gdn_kernel,
        grid=(b // nb, s // ts),
        in_specs=[
            pl.BlockSpec((nb, ts, qkv_w), lambda i, t: (i, t, OFF_GQKV // qkv_w)),
            pl.BlockSpec((nb, ts, GDN_WIDTH), lambda i, t: (i, t, OFF_GZ // GDN_WIDTH)),
            pl.BlockSpec((nb, ts, GATE_PAD), lambda i, t: (i, t, 0)),
            pl.BlockSpec((GDN_CONV_K, qkv_w), lambda i, t: (0, 0)),
            pl.BlockSpec((1, LANES), lambda i, t: (0, 0)),
            pl.BlockSpec((1, LANES), lambda i, t: (0, 0)),
            pl.BlockSpec((1, GDN_HEAD_DIM), lambda i, t: (0, 0)),
        ],
        out_specs=pl.BlockSpec((nb, ts, GDN_WIDTH), lambda i, t: (i, t, 0)),
        out_shape=jax.ShapeDtypeStruct((b, s, GDN_WIDTH), BF16),
        scratch_shapes=[pltpu.VMEM((nb, SUBLANES, qkv_w), F32),
                        pltpu.VMEM((nb, GDN_HEADS, GDN_HEAD_DIM, GDN_HEAD_DIM), F32)],
        compiler_params=pltpu.CompilerParams(
            dimension_semantics=("parallel", "arbitrary"), vmem_limit_bytes=VMEM_LIMIT),
        name="gdn",
    )(proj3, proj3, gate3, conv_w, alog_b, dtb_b, norm_g)


def _out_proj_kernel(x_ref, yc_ref, ya_ref, yg_ref, w_ref, g_ref, x1_ref, h_ref):
    acc = _dot(yc_ref[...], w_ref[0:CONV_WIDTH, :])
    acc = acc + _dot(ya_ref[...], w_ref[CONV_WIDTH:CONV_WIDTH + ATTN_WIDTH, :])
    acc = acc + _dot(yg_ref[...], w_ref[CONV_WIDTH + ATTN_WIDTH:, :])
    x1 = x_ref[...] + acc
    x1_ref[...] = x1
    ms = jnp.mean(x1 * x1, axis=-1, keepdims=True)
    h_ref[...] = (x1 * lax.rsqrt(ms + EPS) * g_ref[...]).astype(BF16)


def _out_proj(x2, yc, ya, yg, w_all, layer, g, tm=512):
    m = x2.shape[0]
    return pl.pallas_call(
        _out_proj_kernel,
        grid=(m // tm,),
        in_specs=[
            pl.BlockSpec((tm, D_MODEL), lambda i: (i, 0)),
            pl.BlockSpec((tm, CONV_WIDTH), lambda i: (i, 0)),
            pl.BlockSpec((tm, ATTN_WIDTH), lambda i: (i, 0)),
            pl.BlockSpec((tm, GDN_WIDTH), lambda i: (i, 0)),
            pl.BlockSpec((None, D_MODEL, D_MODEL), lambda i: (layer, 0, 0)),
            pl.BlockSpec((1, D_MODEL), lambda i: (0, 0)),
        ],
        out_specs=[pl.BlockSpec((tm, D_MODEL), lambda i: (i, 0)),
                   pl.BlockSpec((tm, D_MODEL), lambda i: (i, 0))],
        out_shape=[jax.ShapeDtypeStruct((m, D_MODEL), F32),
                   jax.ShapeDtypeStruct((m, D_MODEL), BF16)],
        compiler_params=pltpu.CompilerParams(
            dimension_semantics=("parallel",), vmem_limit_bytes=VMEM_LIMIT),
        name="out_proj",
    )(x2, yc, ya, yg, w_all, g)


def _mlp_kernel(x1_ref, h_ref, wu_ref, wd_ref, o_ref):
    @pl.when(pl.program_id(1) == 0)
    def _():
        o_ref[...] = x1_ref[...]

    hid = jnp.maximum(_dot(h_ref[...], wu_ref[...]), 0.0)
    hid = (hid * hid).astype(BF16)
    o_ref[...] += _dot(hid, wd_ref[...])


def _mlp(x1, h, w_up_all, w_down_all, layer, tm=512, tf=2048):
    m = x1.shape[0]
    return pl.pallas_call(
        _mlp_kernel,
        grid=(m // tm, D_FF // tf),
        in_specs=[
            pl.BlockSpec((tm, D_MODEL), lambda i, f: (i, 0)),
            pl.BlockSpec((tm, D_MODEL), lambda i, f: (i, 0)),
            pl.BlockSpec((None, D_MODEL, tf), lambda i, f: (layer, 0, f)),
            pl.BlockSpec((None, tf, D_MODEL), lambda i, f: (layer, f, 0)),
        ],
        out_specs=pl.BlockSpec((tm, D_MODEL), lambda i, f: (i, 0)),
        out_shape=jax.ShapeDtypeStruct((m, D_MODEL), F32),
        compiler_params=pltpu.CompilerParams(
            dimension_semantics=("parallel", "arbitrary"), vmem_limit_bytes=VMEM_LIMIT),
        name="mlp",
    )(x1, h, w_up_all, w_down_all)


def _gate_w_in(w_in):
    gate = w_in[..., D_IN_MAIN:].astype(BF16)
    return jnp.pad(gate, ((0, 0), (0, 0), (0, GATE_PAD - gate.shape[-1])))


def _decay_lanes(p):
    return jnp.zeros((1, LANES), F32).at[0, GDN_HEADS:2 * GDN_HEADS].set(p)


def _layer(x2, b, s, layer, w_in_all, w_gate_all, w_out_all, w_up_all, w_down_all, norm1_g, conv_w, conv_out_g,
           q_norm_g, k_norm_g, attn_sinks, attn_out_g, gdn_conv_w, gdn_A_log, gdn_dt_bias, gdn_norm_g, norm2_g):
    proj, gates = _in_proj(x2, norm1_g[None, :], w_in_all, w_gate_all, layer)
    proj3 = proj.reshape(b, s, D_IN_MAIN)
    yc = _conv_mixer(proj3, conv_w, conv_out_g[None, :])
    ya = _swa(proj3, jnp.tile(q_norm_g, 2)[None, :], jnp.tile(k_norm_g, ATTN_KV_HEADS)[None, :], attn_sinks,
              attn_out_g[None, :])
    yg = _gdn(proj3, gates.reshape(b, s, GATE_PAD), gdn_conv_w, _decay_lanes(gdn_A_log),
              _decay_lanes(gdn_dt_bias), gdn_norm_g[None, :])
    m = b * s
    x1, h2 = _out_proj(x2, yc.reshape(m, CONV_WIDTH), ya.reshape(m, ATTN_WIDTH), yg.reshape(m, GDN_WIDTH),
                       w_out_all, layer, norm2_g[None, :])
    return _mlp(x1, h2, w_up_all, w_down_all, layer)


def kernel(x, norm1_g, w_in, conv_w, conv_out_g, q_norm_g, k_norm_g, attn_sinks, attn_out_g, gdn_conv_w,
           gdn_A_log, gdn_dt_bias, gdn_norm_g, w_out, norm2_g, w_up, w_down):
    b, s, d = x.shape
    x2 = x.reshape(b * s, d)
    w_in_all, w_gate_all = _regroup_w_in(w_in), _gate_w_in(w_in)
    w_out_all, w_up_all, w_down_all = w_out.astype(BF16), w_up.astype(BF16), w_down.astype(BF16)
    for l in range(norm1_g.shape[0]):
        x2 = _layer(x2, b, s, l, w_in_all, w_gate_all, w_out_all, w_up_all, w_down_all, norm1_g[l], conv_w[l],
                    conv_out_g[l], q_norm_g[l], k_norm_g[l], attn_sinks[l], attn_out_g[l], gdn_conv_w[l],
                    gdn_A_log[l], gdn_dt_bias[l], gdn_norm_g[l], norm2_g[l])
    return x2.reshape(b, s, d)
```

```python
import jax
import jax.numpy as jnp
from jax import lax
from jax.experimental import pallas as pl
from jax.experimental.pallas import tpu as pltpu

F32 = jnp.float32
BF16 = jnp.bfloat16
HIGHEST = lax.Precision.HIGHEST

D_MODEL = 2048
CONV_WIDTH = 512
CONV_GROUPS = 4
CONV_GROUP_DIM = 128
ATTN_HEAD_DIM = 64
ATTN_HEADS = 8
ATTN_KV_HEADS = 2
ATTN_WIDTH = 512
ATTN_KV_WIDTH = 128
WINDOW = 128
ATTN_BLOCK = 128
SWA_BATCH = 2
GDN_HEAD_DIM = 128
GDN_WIDTH = 1024
GDN_HEADS = 8
GDN_CONV_K = 4
GDN_CHUNK = 64
GDN_BATCH = 2
D_FF = 8192
EPS = 1e-6

LANES = 128
SUBLANES = 8
VMEM_LIMIT = 56 * 1024 * 1024

GATE_PAD = LANES
MXU_WIDTH = 256
REGROUP_BLOCK = MXU_WIDTH
D_IN_MAIN = 4 * GDN_WIDTH + 3 * CONV_WIDTH + ATTN_WIDTH + 2 * ATTN_KV_WIDTH
IN_PROJ_TN = 5 * MXU_WIDTH
assert D_IN_MAIN % IN_PROJ_TN == 0
OFF_GQKV = 0
OFF_GZ = 3 * GDN_WIDTH
OFF_CB = 4 * GDN_WIDTH
OFF_CC = OFF_CB + CONV_WIDTH
OFF_CX = OFF_CC + CONV_WIDTH
OFF_AQ = OFF_CX + CONV_WIDTH
OFF_AK = OFF_AQ + ATTN_WIDTH
OFF_AV = OFF_AK + ATTN_KV_WIDTH
assert OFF_AV + ATTN_KV_WIDTH == D_IN_MAIN


def _dot(a, b, precision=None):
    return jnp.dot(a, b, preferred_element_type=F32, precision=precision)


def _dot_nt(a, b, precision=None):
    return lax.dot_general(a, b, (((1,), (1,)), ((), ())), preferred_element_type=F32, precision=precision)


def _dot_tn(a, b, precision=None):
    return lax.dot_general(a, b, (((0,), (0,)), ((), ())), preferred_element_type=F32, precision=precision)


def _sigmoid(x):
    return 0.5 + 0.5 * jnp.tanh(0.5 * x)


def _silu(x):
    hx = 0.5 * x
    return hx + hx * jnp.tanh(hx)


def _softplus(x):
    return jnp.maximum(x, 0.0) + jnp.log(1.0 + jnp.exp(-jnp.abs(x)))


def _regroup_kernel(w_ref, o_ref):
    o_ref[...] = w_ref[...].astype(o_ref.dtype)


def _regroup_w_in(w_in_t):
    blk = REGROUP_BLOCK
    n_front = (3 * CONV_WIDTH + ATTN_WIDTH + 2 * ATTN_KV_WIDTH) // blk
    n_gdn = 4 * GDN_WIDTH // blk
    layers = w_in_t.shape[0]

    def src_block(l, j):
        return (l, jnp.where(j < n_gdn, j + n_front, j - n_gdn), 0)

    return pl.pallas_call(
        _regroup_kernel,
        grid=(layers, D_IN_MAIN // blk),
        in_specs=[pl.BlockSpec((None, blk, D_MODEL), src_block)],
        out_specs=pl.BlockSpec((None, blk, D_MODEL), lambda l, j: (l, j, 0)),
        out_shape=jax.ShapeDtypeStruct((layers, D_IN_MAIN, D_MODEL), BF16),
        compiler_params=pltpu.CompilerParams(
            dimension_semantics=("parallel", "parallel"), vmem_limit_bytes=VMEM_LIMIT),
        name="regroup_w_in",
    )(w_in_t)


def _in_proj_kernel(x_ref, g_ref, w_ref, wg_ref, o_ref, og_ref, h_ref):
    @pl.when(pl.program_id(1) == 0)
    def _():
        x = x_ref[...]
        ms = jnp.mean(x * x, axis=-1, keepdims=True)
        h = (x * lax.rsqrt(ms + EPS) * g_ref[...]).astype(BF16)
        h_ref[...] = h
        og_ref[...] = _dot(h, wg_ref[...])

    o_ref[...] = _dot_nt(h_ref[...], w_ref[...])


def _in_proj(x2, g, w_all, wg_all, layer, tm=1024, tn=IN_PROJ_TN):
    m = x2.shape[0]
    n = w_all.shape[1]
    return pl.pallas_call(
        _in_proj_kernel,
        grid=(m // tm, n // tn),
        in_specs=[
            pl.BlockSpec((tm, D_MODEL), lambda i, j: (i, 0)),
            pl.BlockSpec((1, D_MODEL), lambda i, j: (0, 0)),
            pl.BlockSpec((None, tn, D_MODEL), lambda i, j: (layer, j, 0)),
            pl.BlockSpec((None, D_MODEL, GATE_PAD), lambda i, j: (layer, 0, 0)),
        ],
        out_specs=[pl.BlockSpec((tm, tn), lambda i, j: (i, j)),
                   pl.BlockSpec((tm, GATE_PAD), lambda i, j: (i, 0))],
        out_shape=[jax.ShapeDtypeStruct((m, n), F32),
                   jax.ShapeDtypeStruct((m, GATE_PAD), F32)],
        scratch_shapes=[pltpu.VMEM((tm, D_MODEL), BF16)],
        compiler_params=pltpu.CompilerParams(
            dimension_semantics=("parallel", "arbitrary"), vmem_limit_bytes=VMEM_LIMIT),
        name="in_proj",
    )(x2, g, w_all, wg_all)


def _conv_kernel(cb_ref, cc_ref, cx_ref, w_ref, g_ref, o_ref):
    z = cc_ref[0] * cx_ref[0]
    row = lax.broadcasted_iota(jnp.int32, z.shape, 0)
    z1 = jnp.where(row >= 1, pltpu.roll(z, 1, axis=0), 0.0)
    z2 = jnp.where(row >= 2, pltpu.roll(z, 2, axis=0), 0.0)
    w = w_ref[...]
    y = w[0:1] * z2 + w[1:2] * z1 + w[2:3] * z
    y = cb_ref[0] * y
    ms = jnp.mean(y * y, axis=-1, keepdims=True)
    o_ref[0] = (y * lax.rsqrt(ms + EPS) * g_ref[...]).astype(o_ref.dtype)


def _conv_mixer(proj3, conv_w, conv_out_g):
    b, s, _ = proj3.shape
    gd = CONV_GROUP_DIM

    def col(off):
        return lambda i, g: (i, 0, off // gd + g)

    return pl.pallas_call(
        _conv_kernel,
        grid=(b, CONV_GROUPS),
        in_specs=[
            pl.BlockSpec((1, s, gd), col(OFF_CB)),
            pl.BlockSpec((1, s, gd), col(OFF_CC)),
            pl.BlockSpec((1, s, gd), col(OFF_CX)),
            pl.BlockSpec((3, gd), lambda i, g: (0, g)),
            pl.BlockSpec((1, gd), lambda i, g: (0, g)),
        ],
        out_specs=pl.BlockSpec((1, s, gd), lambda i, g: (i, 0, g)),
        out_shape=jax.ShapeDtypeStruct((b, s, CONV_WIDTH), BF16),
        compiler_params=pltpu.CompilerParams(
            dimension_semantics=("parallel", "parallel"), vmem_limit_bytes=VMEM_LIMIT),
        name="conv_mixer",
    )(proj3, proj3, proj3, conv_w, conv_out_g)


def _swa_kernel(q_ref, k_ref, v_ref, qg_ref, kg_ref, sink_ref, og_ref, o_ref, kk_ref, vv_ref):
    nb, s_len = q_ref.shape[0], q_ref.shape[1]
    blk = ATTN_BLOCK
    hd = ATTN_HEAD_DIM
    nh = ATTN_HEADS
    group = nh // ATTN_KV_HEADS
    pair_w = 2 * hd
    units = [(bb, h) for bb in range(nb) for h in range(nh)]
    nu = range(len(units))

    lane = lax.broadcasted_iota(jnp.int32, (1, pair_w), 1)
    lo = lane < hd
    half_mask = (lo.astype(F32), 1.0 - lo.astype(F32))
    avg = jnp.where(lax.broadcasted_iota(jnp.int32, (pair_w, pair_w), 0) // hd
                    == lax.broadcasted_iota(jnp.int32, (pair_w, pair_w), 1) // hd, 1.0 / hd, 0.0).astype(BF16)

    def half_mean_sq(x):
        x2 = x * x
        hi = x2.astype(BF16)
        rem = (x2 - hi.astype(F32)).astype(BF16)
        return _dot(hi, avg) + _dot(rem, avg)

    zeros = jnp.zeros((blk, pair_w), BF16)
    for bb in range(nb):
        k = k_ref[bb]
        kn = k * lax.rsqrt(half_mean_sq(k) + EPS) * kg_ref[...]
        ksw = pltpu.roll(kn, hd, axis=1)
        v = v_ref[bb]
        vsw = pltpu.roll(v, hd, axis=1)
        for j in range(ATTN_KV_HEADS):
            kk_ref[bb, j, 0:blk, :] = zeros
            vv_ref[bb, j, 0:blk, :] = zeros
        kk_ref[bb, 0, blk:, :] = jnp.where(lo, kn, ksw).astype(BF16)
        kk_ref[bb, 1, blk:, :] = jnp.where(lo, ksw, kn).astype(BF16)
        vv_ref[bb, 0, blk:, :] = jnp.where(lo, v, vsw).astype(BF16)
        vv_ref[bb, 1, blk:, :] = jnp.where(lo, vsw, v).astype(BF16)

    qi = lax.broadcasted_iota(jnp.int32, (blk, 2 * blk), 0)
    si = lax.broadcasted_iota(jnp.int32, (blk, 2 * blk), 1)
    rel = qi + blk - si
    band = (rel >= 0) & (rel < WINDOW)
    cur = si >= blk
    q_gain = [qg_ref[...] * (hd ** -0.5) * half_mask[i] for i in range(2)]

    def body(n, carry):
        r0 = pl.multiple_of(n * blk, blk)
        valid = band & (cur | (n > 0))
        qm = []
        for bb in range(nb):
            for p in range(nh // 2):
                qp = q_ref[bb, pl.ds(r0, blk), p * pair_w:(p + 1) * pair_w]
                qn = qp * lax.rsqrt(half_mean_sq(qp) + EPS)
                qm += [(qn * q_gain[0]).astype(BF16), (qn * q_gain[1]).astype(BF16)]
        s = [jnp.where(valid, _dot_nt(qm[u], kk_ref[bb, h // group, pl.ds(r0, 2 * blk), :]), -jnp.inf)
             for u, (bb, h) in enumerate(units)]
        m = [jnp.maximum(jnp.max(s[u], axis=-1, keepdims=True), sink_ref[h]) for u, (bb, h) in enumerate(units)]
        p_ = [jnp.exp(s[u] - m[u]) for u in nu]
        inv = [1.0 / (jnp.sum(p_[u], axis=-1, keepdims=True) + jnp.exp(sink_ref[h] - m[u]))
               for u, (bb, h) in enumerate(units)]
        o = [_dot(p_[u].astype(BF16), vv_ref[bb, h // group, pl.ds(r0, 2 * blk), :]) * inv[u]
             for u, (bb, h) in enumerate(units)]
        for bb in range(nb):
            outs = []
            for p in range(nh // 2):
                u = bb * nh + 2 * p
                op = jnp.where(lo, o[u], o[u + 1])
                outs.append(op * lax.rsqrt(half_mean_sq(op) + EPS) * og_ref[:, p * pair_w:(p + 1) * pair_w])
            o_ref[bb, pl.ds(r0, blk), :] = jnp.concatenate(outs, axis=-1).astype(o_ref.dtype)
        return carry

    lax.fori_loop(0, s_len // blk, body, 0)


def _swa(proj3, q_g, k_g2, sinks, out_g, nb=SWA_BATCH):
    b, s, _ = proj3.shape
    return pl.pallas_call(
        _swa_kernel,
        grid=(b // nb,),
        in_specs=[
            pl.BlockSpec((nb, s, ATTN_WIDTH), lambda i: (i, 0, OFF_AQ // ATTN_WIDTH)),
            pl.BlockSpec((nb, s, ATTN_KV_WIDTH), lambda i: (i, 0, OFF_AK // ATTN_KV_WIDTH)),
            pl.BlockSpec((nb, s, ATTN_KV_WIDTH), lambda i: (i, 0, OFF_AV // ATTN_KV_WIDTH)),
            pl.BlockSpec((1, 2 * ATTN_HEAD_DIM), lambda i: (0, 0)),
            pl.BlockSpec((1, ATTN_KV_WIDTH), lambda i: (0, 0)),
            pl.BlockSpec(memory_space=pltpu.SMEM),
            pl.BlockSpec((1, ATTN_WIDTH), lambda i: (0, 0)),
        ],
        out_specs=pl.BlockSpec((nb, s, ATTN_WIDTH), lambda i: (i, 0, 0)),
        out_shape=jax.ShapeDtypeStruct((b, s, ATTN_WIDTH), BF16),
        scratch_shapes=[pltpu.VMEM((nb, ATTN_KV_HEADS, s + ATTN_BLOCK, 2 * ATTN_HEAD_DIM), BF16),
                        pltpu.VMEM((nb, ATTN_KV_HEADS, s + ATTN_BLOCK, 2 * ATTN_HEAD_DIM), BF16)],
        compiler_params=pltpu.CompilerParams(
            dimension_semantics=("parallel",), vmem_limit_bytes=VMEM_LIMIT),
        name="swa",
    )(proj3, proj3, proj3, q_g, k_g2, sinks, out_g)


def _gdn_kernel(qkv_ref, z_ref, gate_ref, cw_ref, alog_ref, dtb_ref, ng_ref, o_ref, halo_ref, state_ref):
    nb, ts = qkv_ref.shape[0], qkv_ref.shape[1]
    c = GDN_CHUNK
    hd = GDN_HEAD_DIM
    nh = GDN_HEADS
    halo = SUBLANES
    units = [(bb, h) for bb in range(nb) for h in range(nh)]
    nu = range(len(units))

    @pl.when(pl.program_id(1) == 0)
    def _():
        state_ref[...] = jnp.zeros(state_ref.shape, F32)
        halo_ref[...] = jnp.zeros(halo_ref.shape, F32)

    row = lax.broadcasted_iota(jnp.int32, (c, c), 0)
    col = lax.broadcasted_iota(jnp.int32, (c, c), 1)
    tril = row >= col
    strict = row > col
    blk16 = (row // 16) == (col // 16)
    blk32 = (row // 32) == (col // 32)
    tril_f = tril.astype(F32)
    sel = (lax.broadcasted_iota(jnp.int32, (2 * nh, LANES), 0)
           == lax.broadcasted_iota(jnp.int32, (2 * nh, LANES), 1)).astype(F32)
    is_beta = lax.broadcasted_iota(jnp.int32, (1, LANES), 1) < nh
    neg_a = -jnp.exp(alog_ref[...])
    dtb = dtb_ref[...]

    def chunk(ci, carry):
        t0 = pl.multiple_of(ci * c, c)
        tp = pl.multiple_of(jnp.maximum(t0 - halo, 0), halo)
        first = ci == 0

        def conv_silu(bb, lo):
            w = cw_ref[:, lo:lo + hd]
            prev = jnp.where(first, halo_ref[bb, :, lo:lo + hd], qkv_ref[bb, pl.ds(tp, halo), lo:lo + hd])
            win = jnp.concatenate([prev, qkv_ref[bb, pl.ds(t0, c), lo:lo + hd]], axis=0)
            y = w[GDN_CONV_K - 1:GDN_CONV_K] * win[halo:]
            for k in range(1, GDN_CONV_K):
                y = y + w[GDN_CONV_K - 1 - k:GDN_CONV_K - k] * pltpu.roll(win, k, axis=0)[halo:]
            return _silu(y)

        gmix, gc_all, rows = [], [], []
        for bb in range(nb):
            gl = gate_ref[bb, pl.ds(t0, c), :]
            gm = jnp.where(is_beta, _sigmoid(gl), neg_a * _softplus(gl + dtb))
            ga = _dot(tril_f, gm, HIGHEST)
            gmix.append(gm)
            gc_all.append(ga)
            rows.append(_dot_nt(sel, jnp.concatenate([gm, ga], axis=0), HIGHEST))

        qs, ks, vs = [], [], []
        for bb, h in units:
            q = conv_silu(bb, h * hd)
            k = conv_silu(bb, GDN_WIDTH + h * hd)
            vs.append(conv_silu(bb, 2 * GDN_WIDTH + h * hd))
            qs.append(q * (lax.rsqrt(jnp.sum(q * q, axis=-1, keepdims=True) + EPS) * (hd ** -0.5)))
            ks.append(k * lax.rsqrt(jnp.sum(k * k, axis=-1, keepdims=True) + EPS))

        kkqk = []
        for u in nu:
            kb = ks[u].astype(BF16)
            kkqk.append(_dot_nt(jnp.concatenate([kb, qs[u].astype(BF16)], axis=0), kb))
        beta = [gmix[bb][:, h:h + 1] for bb, h in units]
        gc = [gc_all[bb][:, nh + h:nh + h + 1] for bb, h in units]
        g_last = [gc_all[bb][c - 1:c, nh + h:nh + h + 1] for bb, h in units]
        beta_row = [rows[bb][h:h + 1, :c] for bb, h in units]
        es, ps, qkb, off_diag = [], [], [], []
        for u, (bb, h) in enumerate(units):
            gc_row = rows[bb][nh + h:nh + h + 1, c:]
            decay = jnp.exp(jnp.where(tril, gc[u] - gc_row, 0.0))
            a = jnp.where(strict, kkqk[u][:c] * beta[u] * decay, 0.0)
            qkb.append(jnp.where(tril, kkqk[u][c:] * decay, 0.0).astype(BF16))
            a_d = jnp.where(blk16, a, 0.0)
            es.append(-a_d)
            ps.append(a_d)
            off_diag.append((jnp.where(blk32, a - a_d, 0.0), jnp.where(blk32, 0.0, a)))
        ps = [_dot(ps[u].astype(BF16), ps[u].astype(BF16)) for u in nu]
        for i in range(3):
            if i < 2:
                ep = [_dot(jnp.concatenate([es[u], ps[u]], axis=0).astype(BF16), ps[u].astype(BF16)) for u in nu]
                es = [es[u] + ps[u] + ep[u][:c] for u in nu]
                ps = [ep[u][c:] for u in nu]
            else:
                ep = [_dot(es[u].astype(BF16), ps[u].astype(BF16)) for u in nu]
                es = [es[u] + ps[u] + ep[u] for u in nu]
        for level in range(2):
            ls = [off_diag[u][level] for u in nu]
            g1 = [ls[u] + _dot(es[u].astype(BF16), ls[u].astype(BF16)) for u in nu]
            es = [es[u] - (g1[u] + _dot(g1[u].astype(BF16), es[u].astype(BF16))) for u in nu]
        egc = [jnp.exp(gc[u]) for u in nu]
        wu = []
        for u in nu:
            kv = jnp.concatenate([ks[u] * egc[u], vs[u]], axis=-1)
            e_b = (es[u] * beta_row[u]).astype(BF16)
            wu.append(kv * beta[u] + _dot(e_b, kv.astype(BF16)))
        st = [state_ref[bb, h] for bb, h in units]
        ws_qs = [_dot(jnp.concatenate([wu[u][:, :hd], qs[u] * egc[u]], axis=0).astype(BF16), st[u].astype(BF16))
                 for u in nu]
        vb = [(wu[u][:, hd:] - ws_qs[u][:c]).astype(BF16) for u in nu]
        kd = [(ks[u] * jnp.exp(g_last[u] - gc[u])).astype(BF16) for u in nu]
        os_ = [ws_qs[u][c:] + _dot(qkb[u], vb[u]) for u in nu]
        for u, (bb, h) in enumerate(units):
            state_ref[bb, h] = st[u] * jnp.exp(g_last[u]) + _dot_tn(kd[u], vb[u])
        for u, (bb, h) in enumerate(units):
            o = os_[u]
            oms = jnp.mean(o * o, axis=-1, keepdims=True)
            zh = z_ref[bb, pl.ds(t0, c), h * hd:(h + 1) * hd]
            y = o * lax.rsqrt(oms + EPS) * ng_ref[...] * _silu(zh)
            o_ref[bb, pl.ds(t0, c), h * hd:(h + 1) * hd] = y.astype(o_ref.dtype)
        return carry

    lax.fori_loop(0, ts // c, chunk, 0)
    halo_ref[...] = qkv_ref[:, ts - halo:ts, :]


def _gdn(proj3, gate3, conv_w, alog_b, dtb_b, norm_g, ts=256, nb=GDN_BATCH):
    b, s, _ = proj3.shape
    qkv_w = 3 * GDN_WIDTH
    return pl.pallas_call(
        _gdn_kernel,
        grid=(b // nb, s // ts),
        in_specs=[
            pl.BlockSpec((nb, ts, qkv_w), lambda i, t: (i, t, OFF_GQKV // qkv_w)),
            pl.BlockSpec((nb, ts, GDN_WIDTH), lambda i, t: (i, t, OFF_GZ // GDN_WIDTH)),
            pl.BlockSpec((nb, ts, GATE_PAD), lambda i, t: (i, t, 0)),
            pl.BlockSpec((GDN_CONV_K, qkv_w), lambda i, t: (0, 0)),
            pl.BlockSpec((1, LANES), lambda i, t: (0, 0)),
            pl.BlockSpec((1, LANES), lambda i, t: (0, 0)),
            pl.BlockSpec((1, GDN_HEAD_DIM), lambda i, t: (0, 0)),
        ],
        out_specs=pl.BlockSpec((nb, ts, GDN_WIDTH), lambda i, t: (i, t, 0)),
        out_shape=jax.ShapeDtypeStruct((b, s, GDN_WIDTH), BF16),
        scratch_shapes=[pltpu.VMEM((nb, SUBLANES, qkv_w), F32),
                        pltpu.VMEM((nb, GDN_HEADS, GDN_HEAD_DIM, GDN_HEAD_DIM), F32)],
        compiler_params=pltpu.CompilerParams(
            dimension_semantics=("parallel", "arbitrary"), vmem_limit_bytes=VMEM_LIMIT),
        name="gdn",
    )(proj3, proj3, gate3, conv_w, alog_b, dtb_b, norm_g)


def _out_proj_kernel(x_ref, yc_ref, ya_ref, yg_ref, w_ref, g_ref, x1_ref, h_ref):
    acc = _dot(yc_ref[...], w_ref[0:CONV_WIDTH, :])
    acc = acc + _dot(ya_ref[...], w_ref[CONV_WIDTH:CONV_WIDTH + ATTN_WIDTH, :])
    acc = acc + _dot(yg_ref[...], w_ref[CONV_WIDTH + ATTN_WIDTH:, :])
    x1 = x_ref[...] + acc
    x1_ref[...] = x1
    ms = jnp.mean(x1 * x1, axis=-1, keepdims=True)
    h_ref[...] = (x1 * lax.rsqrt(ms + EPS) * g_ref[...]).astype(BF16)


def _out_proj(x2, yc, ya, yg, w_all, layer, g, tm=512):
    m = x2.shape[0]
    return pl.pallas_call(
        _out_proj_kernel,
        grid=(m // tm,),
        in_specs=[
            pl.BlockSpec((tm, D_MODEL), lambda i: (i, 0)),
            pl.BlockSpec((tm, CONV_WIDTH), lambda i: (i, 0)),
            pl.BlockSpec((tm, ATTN_WIDTH), lambda i: (i, 0)),
            pl.BlockSpec((tm, GDN_WIDTH), lambda i: (i, 0)),
            pl.BlockSpec((None, D_MODEL, D_MODEL), lambda i: (layer, 0, 0)),
            pl.BlockSpec((1, D_MODEL), lambda i: (0, 0)),
        ],
        out_specs=[pl.BlockSpec((tm, D_MODEL), lambda i: (i, 0)),
                   pl.BlockSpec((tm, D_MODEL), lambda i: (i, 0))],
        out_shape=[jax.ShapeDtypeStruct((m, D_MODEL), F32),
                   jax.ShapeDtypeStruct((m, D_MODEL), BF16)],
        compiler_params=pltpu.CompilerParams(
            dimension_semantics=("parallel",), vmem_limit_bytes=VMEM_LIMIT),
        name="out_proj",
    )(x2, yc, ya, yg, w_all, g)


def _mlp_kernel(x1_ref, h_ref, wu_ref, wd_ref, o_ref):
    @pl.when(pl.program_id(1) == 0)
    def _():
        o_ref[...] = x1_ref[...]

    hid = jnp.maximum(_dot(h_ref[...], wu_ref[...]), 0.0)
    hid = (hid * hid).astype(BF16)
    o_ref[...] += _dot(hid, wd_ref[...])


def _mlp(x1, h, w_up_all, w_down_all, layer, tm=512, tf=2048):
    m = x1.shape[0]
    return pl.pallas_call(
        _mlp_kernel,
        grid=(m // tm, D_FF // tf),
        in_specs=[
            pl.BlockSpec((tm, D_MODEL), lambda i, f: (i, 0)),
            pl.BlockSpec((tm, D_MODEL), lambda i, f: (i, 0)),
            pl.BlockSpec((None, D_MODEL, tf), lambda i, f: (layer, 0, f)),
            pl.BlockSpec((None, tf, D_MODEL), lambda i, f: (layer, f, 0)),
        ],
        out_specs=pl.BlockSpec((tm, D_MODEL), lambda i, f: (i, 0)),
        out_shape=jax.ShapeDtypeStruct((m, D_MODEL), F32),
        compiler_params=pltpu.CompilerParams(
            dimension_semantics=("parallel", "arbitrary"), vmem_limit_bytes=VMEM_LIMIT),
        name="mlp",
    )(x1, h, w_up_all, w_down_all)


def _gate_w_in(w_in):
    gate = w_in[..., D_IN_MAIN:].astype(BF16)
    return jnp.pad(gate, ((0, 0), (0, 0), (0, GATE_PAD - gate.shape[-1])))


def _decay_lanes(p):
    return jnp.zeros((1, LANES), F32).at[0, GDN_HEADS:2 * GDN_HEADS].set(p)


def _layer(x2, b, s, layer, w_in_all, w_gate_all, w_out_all, w_up_all, w_down_all, norm1_g, conv_w, conv_out_g,
           q_norm_g, k_norm_g, attn_sinks, attn_out_g, gdn_conv_w, gdn_A_log, gdn_dt_bias, gdn_norm_g, norm2_g):
    proj, gates = _in_proj(x2, norm1_g[None, :], w_in_all, w_gate_all, layer)
    proj3 = proj.reshape(b, s, D_IN_MAIN)
    yc = _conv_mixer(proj3, conv_w, conv_out_g[None, :])
    ya = _swa(proj3, jnp.tile(q_norm_g, 2)[None, :], jnp.tile(k_norm_g, ATTN_KV_HEADS)[None, :], attn_sinks,
              attn_out_g[None, :])
    yg = _gdn(proj3, gates.reshape(b, s, GATE_PAD), gdn_conv_w, _decay_lanes(gdn_A_log),
              _decay_lanes(gdn_dt_bias), gdn_norm_g[None, :])
    m = b * s
    x1, h2 = _out_proj(x2, yc.reshape(m, CONV_WIDTH), ya.reshape(m, ATTN_WIDTH), yg.reshape(m, GDN_WIDTH),
                       w_out_all, layer, norm2_g[None, :])
    return _mlp(x1, h2, w_up_all, w_down_all, layer)


def kernel(x, norm1_g, w_in, conv_w, conv_out_g, q_norm_g, k_norm_g, attn_sinks, attn_out_g, gdn_conv_w,
           gdn_A_log, gdn_dt_bias, gdn_norm_g, w_out, norm2_g, w_up, w_down):
    b, s, d = x.shape
    x2 = x.reshape(b * s, d)
    w_in_all, w_gate_all = _regroup_w_in(jnp.swapaxes(w_in, 1, 2)), _gate_w_in(w_in)
    w_out_all, w_up_all, w_down_all = w_out.astype(BF16), w_up.astype(BF16), w_down.astype(BF16)
    for l in range(norm1_g.shape[0]):
        x2 = _layer(x2, b, s, l, w_in_all, w_gate_all, w_out_all, w_up_all, w_down_all, norm1_g[l], conv_w[l],
                    conv_out_g[l], q_norm_g[l], k_norm_g[l], attn_sinks[l], attn_out_g[l], gdn_conv_w[l],
                    gdn_A_log[l], gdn_dt_bias[l], gdn_norm_g[l], norm2_g[l])
    return x2.reshape(b, s, d)
```

```python
import jax
import jax.numpy as jnp
from jax import lax
from jax.experimental import pallas as pl
from jax.experimental.pallas import tpu as pltpu

F32 = jnp.float32
BF16 = jnp.bfloat16
HIGHEST = lax.Precision.HIGHEST

D_MODEL = 2048
CONV_WIDTH = 512
CONV_GROUPS = 4
CONV_GROUP_DIM = 128
ATTN_HEAD_DIM = 64
ATTN_HEADS = 8
ATTN_KV_HEADS = 2
ATTN_WIDTH = 512
ATTN_KV_WIDTH = 128
WINDOW = 128
ATTN_BLOCK = 128
SWA_BATCH = 2
GDN_HEAD_DIM = 128
GDN_WIDTH = 1024
GDN_HEADS = 8
GDN_CONV_K = 4
GDN_CHUNK = 64
GDN_BATCH = 2
D_FF = 8192
EPS = 1e-6

LANES = 128
SUBLANES = 8
VMEM_LIMIT = 56 * 1024 * 1024

GATE_PAD = LANES
MXU_WIDTH = 256
REGROUP_BLOCK = MXU_WIDTH
D_IN_MAIN = 4 * GDN_WIDTH + 3 * CONV_WIDTH + ATTN_WIDTH + 2 * ATTN_KV_WIDTH
IN_PROJ_TN = 5 * MXU_WIDTH
assert D_IN_MAIN % IN_PROJ_TN == 0
OFF_GQKV = 0
OFF_GZ = 3 * GDN_WIDTH
OFF_CB = 4 * GDN_WIDTH
OFF_CC = OFF_CB + CONV_WIDTH
OFF_CX = OFF_CC + CONV_WIDTH
OFF_AQ = OFF_CX + CONV_WIDTH
OFF_AK = OFF_AQ + ATTN_WIDTH
OFF_AV = OFF_AK + ATTN_KV_WIDTH
assert OFF_AV + ATTN_KV_WIDTH == D_IN_MAIN


def _dot(a, b, precision=None):
    return jnp.dot(a, b, preferred_element_type=F32, precision=precision)


def _dot_nt(a, b, precision=None):
    return lax.dot_general(a, b, (((1,), (1,)), ((), ())), preferred_element_type=F32, precision=precision)


def _dot_tn(a, b, precision=None):
    return lax.dot_general(a, b, (((0,), (0,)), ((), ())), preferred_element_type=F32, precision=precision)


def _sigmoid(x):
    return 0.5 + 0.5 * jnp.tanh(0.5 * x)


def _silu(x):
    hx = 0.5 * x
    return hx + hx * jnp.tanh(hx)


def _softplus(x):
    return jnp.maximum(x, 0.0) + jnp.log(1.0 + jnp.exp(-jnp.abs(x)))


def _regroup_kernel(w_ref, o_ref):
    o_ref[...] = w_ref[...].astype(o_ref.dtype)


def _regroup_w_in(w_in_t):
    blk = REGROUP_BLOCK
    n_front = (3 * CONV_WIDTH + ATTN_WIDTH + 2 * ATTN_KV_WIDTH) // blk
    n_gdn = 4 * GDN_WIDTH // blk
    layers = w_in_t.shape[0]

    def src_block(l, j):
        return (l, jnp.where(j < n_gdn, j + n_front, j - n_gdn), 0)

    return pl.pallas_call(
        _regroup_kernel,
        grid=(layers, D_IN_MAIN // blk),
        in_specs=[pl.BlockSpec((None, blk, D_MODEL), src_block)],
        out_specs=pl.BlockSpec((None, blk, D_MODEL), lambda l, j: (l, j, 0)),
        out_shape=jax.ShapeDtypeStruct((layers, D_IN_MAIN, D_MODEL), BF16),
        compiler_params=pltpu.CompilerParams(
            dimension_semantics=("parallel", "parallel"), vmem_limit_bytes=VMEM_LIMIT),
        name="regroup_w_in",
    )(w_in_t)


def _in_proj_kernel(x_ref, g_ref, w_ref, wg_ref, o_ref, og_ref):
    x = x_ref[...]
    ms = jnp.mean(x * x, axis=-1, keepdims=True)
    h = (x * lax.rsqrt(ms + EPS) * g_ref[...]).astype(BF16)
    og_ref[...] = _dot(h, wg_ref[...])
    for c0 in range(0, w_ref.shape[0], IN_PROJ_TN):
        o_ref[:, c0:c0 + IN_PROJ_TN] = _dot_nt(h, w_ref[c0:c0 + IN_PROJ_TN, :]).astype(o_ref.dtype)


def _in_proj(x2, g, w_all, wg_all, layer, tm=512):
    m = x2.shape[0]
    n = w_all.shape[1]
    return pl.pallas_call(
        _in_proj_kernel,
        grid=(m // tm,),
        in_specs=[
            pl.BlockSpec((tm, D_MODEL), lambda i: (i, 0)),
            pl.BlockSpec((1, D_MODEL), lambda i: (0, 0)),
            pl.BlockSpec((None, n, D_MODEL), lambda i: (layer, 0, 0), pipeline_mode=pl.Buffered(1)),
            pl.BlockSpec((None, D_MODEL, GATE_PAD), lambda i: (layer, 0, 0), pipeline_mode=pl.Buffered(1)),
        ],
        out_specs=[pl.BlockSpec((tm, n), lambda i: (i, 0)),
                   pl.BlockSpec((tm, GATE_PAD), lambda i: (i, 0))],
        out_shape=[jax.ShapeDtypeStruct((m, n), BF16),
                   jax.ShapeDtypeStruct((m, GATE_PAD), F32)],
        compiler_params=pltpu.CompilerParams(
            dimension_semantics=("parallel",), vmem_limit_bytes=VMEM_LIMIT),
        name="in_proj",
    )(x2, g, w_all, wg_all)


def _conv_kernel(cb_ref, cc_ref, cx_ref, w_ref, g_ref, o_ref):
    z = cc_ref[0].astype(F32) * cx_ref[0].astype(F32)
    row = lax.broadcasted_iota(jnp.int32, z.shape, 0)
    z1 = jnp.where(row >= 1, pltpu.roll(z, 1, axis=0), 0.0)
    z2 = jnp.where(row >= 2, pltpu.roll(z, 2, axis=0), 0.0)
    w = w_ref[...]
    y = w[0:1] * z2 + w[1:2] * z1 + w[2:3] * z
    y = cb_ref[0].astype(F32) * y
    ms = jnp.mean(y * y, axis=-1, keepdims=True)
    o_ref[0] = (y * lax.rsqrt(ms + EPS) * g_ref[...]).astype(o_ref.dtype)


def _conv_mixer(proj3, conv_w, conv_out_g):
    b, s, _ = proj3.shape
    gd = CONV_GROUP_DIM

    def col(off):
        return lambda i, g: (i, 0, off // gd + g)

    return pl.pallas_call(
        _conv_kernel,
        grid=(b, CONV_GROUPS),
        in_specs=[
            pl.BlockSpec((1, s, gd), col(OFF_CB)),
            pl.BlockSpec((1, s, gd), col(OFF_CC)),
            pl.BlockSpec((1, s, gd), col(OFF_CX)),
            pl.BlockSpec((3, gd), lambda i, g: (0, g)),
            pl.BlockSpec((1, gd), lambda i, g: (0, g)),
        ],
        out_specs=pl.BlockSpec((1, s, gd), lambda i, g: (i, 0, g)),
        out_shape=jax.ShapeDtypeStruct((b, s, CONV_WIDTH), BF16),
        compiler_params=pltpu.CompilerParams(
            dimension_semantics=("parallel", "parallel"), vmem_limit_bytes=VMEM_LIMIT),
        name="conv_mixer",
    )(proj3, proj3, proj3, conv_w, conv_out_g)


def _swa_kernel(q_ref, k_ref, v_ref, qg_ref, kg_ref, sink_ref, og_ref, o_ref, kk_ref, vv_ref):
    nb, s_len = q_ref.shape[0], q_ref.shape[1]
    blk = ATTN_BLOCK
    hd = ATTN_HEAD_DIM
    nh = ATTN_HEADS
    group = nh // ATTN_KV_HEADS
    pair_w = 2 * hd
    units = [(bb, h) for bb in range(nb) for h in range(nh)]
    nu = range(len(units))

    lane = lax.broadcasted_iota(jnp.int32, (1, pair_w), 1)
    lo = lane < hd
    half_mask = (lo.astype(F32), 1.0 - lo.astype(F32))
    avg = jnp.where(lax.broadcasted_iota(jnp.int32, (pair_w, pair_w), 0) // hd
                    == lax.broadcasted_iota(jnp.int32, (pair_w, pair_w), 1) // hd, 1.0 / hd, 0.0).astype(BF16)

    def half_mean_sq(x):
        x2 = x * x
        hi = x2.astype(BF16)
        rem = (x2 - hi.astype(F32)).astype(BF16)
        return _dot(hi, avg) + _dot(rem, avg)

    zeros = jnp.zeros((blk, pair_w), BF16)
    for bb in range(nb):
        k = k_ref[bb].astype(F32)
        kn = k * lax.rsqrt(half_mean_sq(k) + EPS) * kg_ref[...]
        ksw = pltpu.roll(kn, hd, axis=1)
        v = v_ref[bb].astype(F32)
        vsw = pltpu.roll(v, hd, axis=1)
        for j in range(ATTN_KV_HEADS):
            kk_ref[bb, j, 0:blk, :] = zeros
            vv_ref[bb, j, 0:blk, :] = zeros
        kk_ref[bb, 0, blk:, :] = jnp.where(lo, kn, ksw).astype(BF16)
        kk_ref[bb, 1, blk:, :] = jnp.where(lo, ksw, kn).astype(BF16)
        vv_ref[bb, 0, blk:, :] = jnp.where(lo, v, vsw).astype(BF16)
        vv_ref[bb, 1, blk:, :] = jnp.where(lo, vsw, v).astype(BF16)

    qi = lax.broadcasted_iota(jnp.int32, (blk, 2 * blk), 0)
    si = lax.broadcasted_iota(jnp.int32, (blk, 2 * blk), 1)
    rel = qi + blk - si
    band = (rel >= 0) & (rel < WINDOW)
    cur = si >= blk
    q_gain = [qg_ref[...] * (hd ** -0.5) * half_mask[i] for i in range(2)]

    def body(n, carry):
        r0 = pl.multiple_of(n * blk, blk)
        valid = band & (cur | (n > 0))
        qm = []
        for bb in range(nb):
            for p in range(nh // 2):
                qp = q_ref[bb, pl.ds(r0, blk), p * pair_w:(p + 1) * pair_w].astype(F32)
                qn = qp * lax.rsqrt(half_mean_sq(qp) + EPS)
                qm += [(qn * q_gain[0]).astype(BF16), (qn * q_gain[1]).astype(BF16)]
        s = [jnp.where(valid, _dot_nt(qm[u], kk_ref[bb, h // group, pl.ds(r0, 2 * blk), :]), -jnp.inf)
             for u, (bb, h) in enumerate(units)]
        m = [jnp.maximum(jnp.max(s[u], axis=-1, keepdims=True), sink_ref[h]) for u, (bb, h) in enumerate(units)]
        p_ = [jnp.exp(s[u] - m[u]) for u in nu]
        inv = [1.0 / (jnp.sum(p_[u], axis=-1, keepdims=True) + jnp.exp(sink_ref[h] - m[u]))
               for u, (bb, h) in enumerate(units)]
        o = [_dot(p_[u].astype(BF16), vv_ref[bb, h // group, pl.ds(r0, 2 * blk), :]) * inv[u]
             for u, (bb, h) in enumerate(units)]
        for bb in range(nb):
            outs = []
            for p in range(nh // 2):
                u = bb * nh + 2 * p
                op = jnp.where(lo, o[u], o[u + 1])
                outs.append(op * lax.rsqrt(half_mean_sq(op) + EPS) * og_ref[:, p * pair_w:(p + 1) * pair_w])
            o_ref[bb, pl.ds(r0, blk), :] = jnp.concatenate(outs, axis=-1).astype(o_ref.dtype)
        return carry

    lax.fori_loop(0, s_len // blk, body, 0)


def _swa(proj3, q_g, k_g2, sinks, out_g, nb=SWA_BATCH):
    b, s, _ = proj3.shape
    return pl.pallas_call(
        _swa_kernel,
        grid=(b // nb,),
        in_specs=[
            pl.BlockSpec((nb, s, ATTN_WIDTH), lambda i: (i, 0, OFF_AQ // ATTN_WIDTH)),
            pl.BlockSpec((nb, s, ATTN_KV_WIDTH), lambda i: (i, 0, OFF_AK // ATTN_KV_WIDTH)),
            pl.BlockSpec((nb, s, ATTN_KV_WIDTH), lambda i: (i, 0, OFF_AV // ATTN_KV_WIDTH)),
            pl.BlockSpec((1, 2 * ATTN_HEAD_DIM), lambda i: (0, 0)),
            pl.BlockSpec((1, ATTN_KV_WIDTH), lambda i: (0, 0)),
            pl.BlockSpec(memory_space=pltpu.SMEM),
            pl.BlockSpec((1, ATTN_WIDTH), lambda i: (0, 0)),
        ],
        out_specs=pl.BlockSpec((nb, s, ATTN_WIDTH), lambda i: (i, 0, 0)),
        out_shape=jax.ShapeDtypeStruct((b, s, ATTN_WIDTH), BF16),
        scratch_shapes=[pltpu.VMEM((nb, ATTN_KV_HEADS, s + ATTN_BLOCK, 2 * ATTN_HEAD_DIM), BF16),
                        pltpu.VMEM((nb, ATTN_KV_HEADS, s + ATTN_BLOCK, 2 * ATTN_HEAD_DIM), BF16)],
        compiler_params=pltpu.CompilerParams(
            dimension_semantics=("parallel",), vmem_limit_bytes=VMEM_LIMIT),
        name="swa",
    )(proj3, proj3, proj3, q_g, k_g2, sinks, out_g)


def _gdn_kernel(qkv_ref, z_ref, gate_ref, cw_ref, alog_ref, dtb_ref, ng_ref, o_ref, halo_ref, state_ref):
    nb, ts = qkv_ref.shape[0], qkv_ref.shape[1]
    c = GDN_CHUNK
    hd = GDN_HEAD_DIM
    nh = GDN_HEADS
    halo = SUBLANES
    hblk = 2 * SUBLANES
    units = [(bb, h) for bb in range(nb) for h in range(nh)]
    nu = range(len(units))

    @pl.when(pl.program_id(1) == 0)
    def _():
        state_ref[...] = jnp.zeros(state_ref.shape, F32)
        halo_ref[...] = jnp.zeros(halo_ref.shape, halo_ref.dtype)

    row = lax.broadcasted_iota(jnp.int32, (c, c), 0)
    col = lax.broadcasted_iota(jnp.int32, (c, c), 1)
    tril = row >= col
    strict = row > col
    blk16 = (row // 16) == (col // 16)
    blk32 = (row // 32) == (col // 32)
    tril_f = tril.astype(F32)
    sel = (lax.broadcasted_iota(jnp.int32, (2 * nh, LANES), 0)
           == lax.broadcasted_iota(jnp.int32, (2 * nh, LANES), 1)).astype(F32)
    is_beta = lax.broadcasted_iota(jnp.int32, (1, LANES), 1) < nh
    neg_a = -jnp.exp(alog_ref[...])
    dtb = dtb_ref[...]

    def chunk(ci, carry):
        t0 = pl.multiple_of(ci * c, c)
        tp = pl.multiple_of(jnp.maximum(t0 - hblk, 0), hblk)
        first = ci == 0

        def conv_silu(bb, lo):
            w = cw_ref[:, lo:lo + hd]
            prev = jnp.where(first, halo_ref[bb, :, lo:lo + hd], qkv_ref[bb, pl.ds(tp, hblk), lo:lo + hd])
            win = jnp.concatenate([prev, qkv_ref[bb, pl.ds(t0, c), lo:lo + hd]], axis=0).astype(F32)[hblk - halo:]
            y = w[GDN_CONV_K - 1:GDN_CONV_K] * win[halo:]
            for k in range(1, GDN_CONV_K):
                y = y + w[GDN_CONV_K - 1 - k:GDN_CONV_K - k] * pltpu.roll(win, k, axis=0)[halo:]
            return _silu(y)

        gmix, gc_all, rows = [], [], []
        for bb in range(nb):
            gl = gate_ref[bb, pl.ds(t0, c), :]
            gm = jnp.where(is_beta, _sigmoid(gl), neg_a * _softplus(gl + dtb))
            ga = _dot(tril_f, gm, HIGHEST)
            gmix.append(gm)
            gc_all.append(ga)
            rows.append(_dot_nt(sel, jnp.concatenate([gm, ga], axis=0), HIGHEST))

        qs, ks, vs = [], [], []
        for bb, h in units:
            q = conv_silu(bb, h * hd)
            k = conv_silu(bb, GDN_WIDTH + h * hd)
            vs.append(conv_silu(bb, 2 * GDN_WIDTH + h * hd))
            qs.append(q * (lax.rsqrt(jnp.sum(q * q, axis=-1, keepdims=True) + EPS) * (hd ** -0.5)))
            ks.append(k * lax.rsqrt(jnp.sum(k * k, axis=-1, keepdims=True) + EPS))

        kkqk = []
        for u in nu:
            kb = ks[u].astype(BF16)
            kkqk.append(_dot_nt(jnp.concatenate([kb, qs[u].astype(BF16)], axis=0), kb))
        beta = [gmix[bb][:, h:h + 1] for bb, h in units]
        gc = [gc_all[bb][:, nh + h:nh + h + 1] for bb, h in units]
        g_last = [gc_all[bb][c - 1:c, nh + h:nh + h + 1] for bb, h in units]
        beta_row = [rows[bb][h:h + 1, :c] for bb, h in units]
        es, ps, qkb, off_diag = [], [], [], []
        for u, (bb, h) in enumerate(units):
            gc_row = rows[bb][nh + h:nh + h + 1, c:]
            decay = jnp.exp(jnp.where(tril, gc[u] - gc_row, 0.0))
            a = jnp.where(strict, kkqk[u][:c] * beta[u] * decay, 0.0)
            qkb.append(jnp.where(tril, kkqk[u][c:] * decay, 0.0).astype(BF16))
            a_d = jnp.where(blk16, a, 0.0)
            es.append(-a_d)
            ps.append(a_d)
            off_diag.append((jnp.where(blk32, a - a_d, 0.0), jnp.where(blk32, 0.0, a)))
        ps = [_dot(ps[u].astype(BF16), ps[u].astype(BF16)) for u in nu]
        for i in range(3):
            if i < 2:
                ep = [_dot(jnp.concatenate([es[u], ps[u]], axis=0).astype(BF16), ps[u].astype(BF16)) for u in nu]
                es = [es[u] + ps[u] + ep[u][:c] for u in nu]
                ps = [ep[u][c:] for u in nu]
            else:
                ep = [_dot(es[u].astype(BF16), ps[u].astype(BF16)) for u in nu]
                es = [es[u] + ps[u] + ep[u] for u in nu]
        for level in range(2):
            ls = [off_diag[u][level] for u in nu]
            g1 = [ls[u] + _dot(es[u].astype(BF16), ls[u].astype(BF16)) for u in nu]
            es = [es[u] - (g1[u] + _dot(g1[u].astype(BF16), es[u].astype(BF16))) for u in nu]
        egc = [jnp.exp(gc[u]) for u in nu]
        wu = []
        for u in nu:
            kv = jnp.concatenate([ks[u] * egc[u], vs[u]], axis=-1)
            e_b = (es[u] * beta_row[u]).astype(BF16)
            wu.append(kv * beta[u] + _dot(e_b, kv.astype(BF16)))
        st = [state_ref[bb, h] for bb, h in units]
        ws_qs = [_dot(jnp.concatenate([wu[u][:, :hd], qs[u] * egc[u]], axis=0).astype(BF16), st[u].astype(BF16))
                 for u in nu]
        vb = [(wu[u][:, hd:] - ws_qs[u][:c]).astype(BF16) for u in nu]
        kd = [(ks[u] * jnp.exp(g_last[u] - gc[u])).astype(BF16) for u in nu]
        os_ = [ws_qs[u][c:] + _dot(qkb[u], vb[u]) for u in nu]
        for u, (bb, h) in enumerate(units):
            state_ref[bb, h] = st[u] * jnp.exp(g_last[u]) + _dot_tn(kd[u], vb[u])
        for u, (bb, h) in enumerate(units):
            o = os_[u]
            oms = jnp.mean(o * o, axis=-1, keepdims=True)
            zh = z_ref[bb, pl.ds(t0, c), h * hd:(h + 1) * hd].astype(F32)
            y = o * lax.rsqrt(oms + EPS) * ng_ref[...] * _silu(zh)
            o_ref[bb, pl.ds(t0, c), h * hd:(h + 1) * hd] = y.astype(o_ref.dtype)
        return carry

    lax.fori_loop(0, ts // c, chunk, 0)
    halo_ref[...] = qkv_ref[:, ts - hblk:ts, :]


def _gdn(proj3, gate3, conv_w, alog_b, dtb_b, norm_g, ts=256, nb=GDN_BATCH):
    b, s, _ = proj3.shape
    qkv_w = 3 * GDN_WIDTH
    return pl.pallas_call(
        _gdn_kernel,
        grid=(b // nb, s // ts),
        in_specs=[
            pl.BlockSpec((nb, ts, qkv_w), lambda i, t: (i, t, OFF_GQKV // qkv_w)),
            pl.BlockSpec((nb, ts, GDN_WIDTH), lambda i, t: (i, t, OFF_GZ // GDN_WIDTH)),
            pl.BlockSpec((nb, ts, GATE_PAD), lambda i, t: (i, t, 0)),
            pl.BlockSpec((GDN_CONV_K, qkv_w), lambda i, t: (0, 0)),
            pl.BlockSpec((1, LANES), lambda i, t: (0, 0)),
            pl.BlockSpec((1, LANES), lambda i, t: (0, 0)),
            pl.BlockSpec((1, GDN_HEAD_DIM), lambda i, t: (0, 0)),
        ],
        out_specs=pl.BlockSpec((nb, ts, GDN_WIDTH), lambda i, t: (i, t, 0)),
        out_shape=jax.ShapeDtypeStruct((b, s, GDN_WIDTH), BF16),
        scratch_shapes=[pltpu.VMEM((nb, 2 * SUBLANES, qkv_w), BF16),
                        pltpu.VMEM((nb, GDN_HEADS, GDN_HEAD_DIM, GDN_HEAD_DIM), F32)],
        compiler_params=pltpu.CompilerParams(
            dimension_semantics=("parallel", "arbitrary"), vmem_limit_bytes=VMEM_LIMIT),
        name="gdn",
    )(proj3, proj3, gate3, conv_w, alog_b, dtb_b, norm_g)


def _out_proj_kernel(x_ref, yc_ref, ya_ref, yg_ref, w_ref, g_ref, x1_ref, h_ref):
    acc = _dot(yc_ref[...], w_ref[0:CONV_WIDTH, :])
    acc = acc + _dot(ya_ref[...], w_ref[CONV_WIDTH:CONV_WIDTH + ATTN_WIDTH, :])
    acc = acc + _dot(yg_ref[...], w_ref[CONV_WIDTH + ATTN_WIDTH:, :])
    x1 = x_ref[...] + acc
    x1_ref[...] = x1
    ms = jnp.mean(x1 * x1, axis=-1, keepdims=True)
    h_ref[...] = (x1 * lax.rsqrt(ms + EPS) * g_ref[...]).astype(BF16)


def _out_proj(x2, yc, ya, yg, w_all, layer, g, tm=512):
    m = x2.shape[0]
    return pl.pallas_call(
        _out_proj_kernel,
        grid=(m // tm,),
        in_specs=[
            pl.BlockSpec((tm, D_MODEL), lambda i: (i, 0)),
            pl.BlockSpec((tm, CONV_WIDTH), lambda i: (i, 0)),
            pl.BlockSpec((tm, ATTN_WIDTH), lambda i: (i, 0)),
            pl.BlockSpec((tm, GDN_WIDTH), lambda i: (i, 0)),
            pl.BlockSpec((None, D_MODEL, D_MODEL), lambda i: (layer, 0, 0)),
            pl.BlockSpec((1, D_MODEL), lambda i: (0, 0)),
        ],
        out_specs=[pl.BlockSpec((tm, D_MODEL), lambda i: (i, 0)),
                   pl.BlockSpec((tm, D_MODEL), lambda i: (i, 0))],
        out_shape=[jax.ShapeDtypeStruct((m, D_MODEL), F32),
                   jax.ShapeDtypeStruct((m, D_MODEL), BF16)],
        compiler_params=pltpu.CompilerParams(
            dimension_semantics=("parallel",), vmem_limit_bytes=VMEM_LIMIT),
        name="out_proj",
    )(x2, yc, ya, yg, w_all, g)


def _mlp_kernel(x1_ref, h_ref, wu_ref, wd_ref, o_ref):
    @pl.when(pl.program_id(1) == 0)
    def _():
        o_ref[...] = x1_ref[...]

    hid = jnp.maximum(_dot(h_ref[...], wu_ref[...]), 0.0)
    hid = (hid * hid).astype(BF16)
    o_ref[...] += _dot(hid, wd_ref[...])


def _mlp(x1, h, w_up_all, w_down_all, layer, tm=512, tf=2048):
    m = x1.shape[0]
    return pl.pallas_call(
        _mlp_kernel,
        grid=(m // tm, D_FF // tf),
        in_specs=[
            pl.BlockSpec((tm, D_MODEL), lambda i, f: (i, 0)),
            pl.BlockSpec((tm, D_MODEL), lambda i, f: (i, 0)),
            pl.BlockSpec((None, D_MODEL, tf), lambda i, f: (layer, 0, f)),
            pl.BlockSpec((None, tf, D_MODEL), lambda i, f: (layer, f, 0)),
        ],
        out_specs=pl.BlockSpec((tm, D_MODEL), lambda i, f: (i, 0)),
        out_shape=jax.ShapeDtypeStruct((m, D_MODEL), F32),
        compiler_params=pltpu.CompilerParams(
            dimension_semantics=("parallel", "arbitrary"), vmem_limit_bytes=VMEM_LIMIT),
        name="mlp",
    )(x1, h, w_up_all, w_down_all)


def _gate_w_in(w_in):
    gate = w_in[..., D_IN_MAIN:].astype(BF16)
    return jnp.pad(gate, ((0, 0), (0, 0), (0, GATE_PAD - gate.shape[-1])))


def _decay_lanes(p):
    return jnp.zeros((1, LANES), F32).at[0, GDN_HEADS:2 * GDN_HEADS].set(p)


def _layer(x2, b, s, layer, w_in_all, w_gate_all, w_out_all, w_up_all, w_down_all, norm1_g, conv_w, conv_out_g,
           q_norm_g, k_norm_g, attn_sinks, attn_out_g, gdn_conv_w, gdn_A_log, gdn_dt_bias, gdn_norm_g, norm2_g):
    proj, gates = _in_proj(x2, norm1_g[None, :], w_in_all, w_gate_all, layer)
    proj3 = proj.reshape(b, s, D_IN_MAIN)
    yc = _conv_mixer(proj3, conv_w, conv_out_g[None, :])
    ya = _swa(proj3, jnp.tile(q_norm_g, 2)[None, :], jnp.tile(k_norm_g, ATTN_KV_HEADS)[None, :], attn_sinks,
              attn_out_g[None, :])
    yg = _gdn(proj3, gates.reshape(b, s, GATE_PAD), gdn_conv_w, _decay_lanes(gdn_A_log),
              _decay_lanes(gdn_dt_bias), gdn_norm_g[None, :])
    m = b * s
    x1, h2 = _out_proj(x2, yc.reshape(m, CONV_WIDTH), ya.reshape(m, ATTN_WIDTH), yg.reshape(m, GDN_WIDTH),
                       w_out_all, layer, norm2_g[None, :])
    return _mlp(x1, h2, w_up_all, w_down_all, layer)


def kernel(x, norm1_g, w_in, conv_w, conv_out_g, q_norm_g, k_norm_g, attn_sinks, attn_out_g, gdn_conv_w,
           gdn_A_log, gdn_dt_bias, gdn_norm_g, w_out, norm2_g, w_up, w_down):
    b, s, d = x.shape
    x2 = x.reshape(b * s, d)
    w_in_all, w_gate_all = _regroup_w_in(jnp.swapaxes(w_in, 1, 2)), _gate_w_in(w_in)
    w_out_all, w_up_all, w_down_all = w_out.astype(BF16), w_up.astype(BF16), w_down.astype(BF16)
    for l in range(norm1_g.shape[0]):
        x2 = _layer(x2, b, s, l, w_in_all, w_gate_all, w_out_all, w_up_all, w_down_all, norm1_g[l], conv_w[l],
                    conv_out_g[l], q_norm_g[l], k_norm_g[l], attn_sinks[l], attn_out_g[l], gdn_conv_w[l],
                    gdn_A_log[l], gdn_dt_bias[l], gdn_norm_g[l], norm2_g[l])
    return x2.reshape(b, s, d)
```

```python
import jax
import jax.numpy as jnp
from jax import lax
from jax.experimental import pallas as pl
from jax.experimental.pallas import tpu as pltpu

F32 = jnp.float32
BF16 = jnp.bfloat16
HIGHEST = lax.Precision.HIGHEST

D_MODEL = 2048
CONV_WIDTH = 512
CONV_GROUPS = 4
CONV_GROUP_DIM = 128
ATTN_HEAD_DIM = 64
ATTN_HEADS = 8
ATTN_KV_HEADS = 2
ATTN_WIDTH = 512
ATTN_KV_WIDTH = 128
WINDOW = 128
ATTN_BLOCK = 128
SWA_BATCH = 2
GDN_HEAD_DIM = 128
GDN_WIDTH = 1024
GDN_HEADS = 8
GDN_CONV_K = 4
GDN_CHUNK = 64
GDN_BATCH = 2
D_FF = 8192
EPS = 1e-6

LANES = 128
SUBLANES = 8
VMEM_LIMIT = 56 * 1024 * 1024

GATE_PAD = LANES
MXU_WIDTH = 256
REGROUP_BLOCK = MXU_WIDTH
D_IN_MAIN = 4 * GDN_WIDTH + 3 * CONV_WIDTH + ATTN_WIDTH + 2 * ATTN_KV_WIDTH
IN_PROJ_TN = 5 * MXU_WIDTH
assert D_IN_MAIN % IN_PROJ_TN == 0
OFF_GQKV = 0
OFF_GZ = 3 * GDN_WIDTH
OFF_CB = 4 * GDN_WIDTH
OFF_CC = OFF_CB + CONV_WIDTH
OFF_CX = OFF_CC + CONV_WIDTH
OFF_AQ = OFF_CX + CONV_WIDTH
OFF_AK = OFF_AQ + ATTN_WIDTH
OFF_AV = OFF_AK + ATTN_KV_WIDTH
assert OFF_AV + ATTN_KV_WIDTH == D_IN_MAIN


def _dot(a, b, precision=None):
    return jnp.dot(a, b, preferred_element_type=F32, precision=precision)


def _dot_nt(a, b, precision=None):
    return lax.dot_general(a, b, (((1,), (1,)), ((), ())), preferred_element_type=F32, precision=precision)


def _dot_tn(a, b, precision=None):
    return lax.dot_general(a, b, (((0,), (0,)), ((), ())), preferred_element_type=F32, precision=precision)


def _sigmoid(x):
    return 0.5 + 0.5 * jnp.tanh(0.5 * x)


def _silu(x):
    hx = 0.5 * x
    return hx + hx * jnp.tanh(hx)


def _softplus(x):
    return jnp.maximum(x, 0.0) + jnp.log(1.0 + jnp.exp(-jnp.abs(x)))


def _regroup_kernel(w_ref, o_ref):
    o_ref[...] = w_ref[...].astype(o_ref.dtype)


def _regroup_w_in(w_in_t):
    blk = REGROUP_BLOCK
    n_front = (3 * CONV_WIDTH + ATTN_WIDTH + 2 * ATTN_KV_WIDTH) // blk
    n_gdn = 4 * GDN_WIDTH // blk
    layers = w_in_t.shape[0]

    def src_block(l, j):
        return (l, jnp.where(j < n_gdn, j + n_front, j - n_gdn), 0)

    return pl.pallas_call(
        _regroup_kernel,
        grid=(layers, D_IN_MAIN // blk),
        in_specs=[pl.BlockSpec((None, blk, D_MODEL), src_block)],
        out_specs=pl.BlockSpec((None, blk, D_MODEL), lambda l, j: (l, j, 0)),
        out_shape=jax.ShapeDtypeStruct((layers, D_IN_MAIN, D_MODEL), BF16),
        compiler_params=pltpu.CompilerParams(
            dimension_semantics=("parallel", "parallel"), vmem_limit_bytes=VMEM_LIMIT),
        name="regroup_w_in",
    )(w_in_t)


def _in_proj_kernel(x_ref, g_ref, w_ref, wg_ref, o_ref, og_ref):
    x = x_ref[...]
    ms = jnp.mean(x * x, axis=-1, keepdims=True)
    h = (x * lax.rsqrt(ms + EPS) * g_ref[...]).astype(BF16)
    og_ref[...] = _dot(h, wg_ref[...])
    for c0 in range(0, w_ref.shape[0], IN_PROJ_TN):
        o_ref[:, c0:c0 + IN_PROJ_TN] = _dot_nt(h, w_ref[c0:c0 + IN_PROJ_TN, :]).astype(o_ref.dtype)


def _in_proj(x2, g, w_all, wg_all, layer, tm=512):
    m = x2.shape[0]
    n = w_all.shape[1]
    return pl.pallas_call(
        _in_proj_kernel,
        grid=(m // tm,),
        in_specs=[
            pl.BlockSpec((tm, D_MODEL), lambda i: (i, 0)),
            pl.BlockSpec((1, D_MODEL), lambda i: (0, 0)),
            pl.BlockSpec((None, n, D_MODEL), lambda i: (layer, 0, 0), pipeline_mode=pl.Buffered(1)),
            pl.BlockSpec((None, D_MODEL, GATE_PAD), lambda i: (layer, 0, 0), pipeline_mode=pl.Buffered(1)),
        ],
        out_specs=[pl.BlockSpec((tm, n), lambda i: (i, 0)),
                   pl.BlockSpec((tm, GATE_PAD), lambda i: (i, 0))],
        out_shape=[jax.ShapeDtypeStruct((m, n), BF16),
                   jax.ShapeDtypeStruct((m, GATE_PAD), F32)],
        compiler_params=pltpu.CompilerParams(
            dimension_semantics=("parallel",), vmem_limit_bytes=VMEM_LIMIT),
        name="in_proj",
    )(x2, g, w_all, wg_all)


def _conv_kernel(cb_ref, cc_ref, cx_ref, w_ref, g_ref, o_ref):
    z = cc_ref[0].astype(F32) * cx_ref[0].astype(F32)
    row = lax.broadcasted_iota(jnp.int32, z.shape, 0)
    z1 = jnp.where(row >= 1, pltpu.roll(z, 1, axis=0), 0.0)
    z2 = jnp.where(row >= 2, pltpu.roll(z, 2, axis=0), 0.0)
    w = w_ref[...]
    y = w[0:1] * z2 + w[1:2] * z1 + w[2:3] * z
    y = cb_ref[0].astype(F32) * y
    ms = jnp.mean(y * y, axis=-1, keepdims=True)
    o_ref[0] = (y * lax.rsqrt(ms + EPS) * g_ref[...]).astype(o_ref.dtype)


def _conv_mixer(proj3, conv_w, conv_out_g):
    b, s, _ = proj3.shape
    gd = CONV_GROUP_DIM

    def col(off):
        return lambda i, g: (i, 0, off // gd + g)

    return pl.pallas_call(
        _conv_kernel,
        grid=(b, CONV_GROUPS),
        in_specs=[
            pl.BlockSpec((1, s, gd), col(OFF_CB)),
            pl.BlockSpec((1, s, gd), col(OFF_CC)),
            pl.BlockSpec((1, s, gd), col(OFF_CX)),
            pl.BlockSpec((3, gd), lambda i, g: (0, g)),
            pl.BlockSpec((1, gd), lambda i, g: (0, g)),
        ],
        out_specs=pl.BlockSpec((1, s, gd), lambda i, g: (i, 0, g)),
        out_shape=jax.ShapeDtypeStruct((b, s, CONV_WIDTH), BF16),
        compiler_params=pltpu.CompilerParams(
            dimension_semantics=("parallel", "parallel"), vmem_limit_bytes=VMEM_LIMIT),
        name="conv_mixer",
    )(proj3, proj3, proj3, conv_w, conv_out_g)


def _swa_kernel(q_ref, k_ref, v_ref, qg_ref, kg_ref, sink_ref, og_ref, o_ref, kk_ref, vv_ref):
    nb, s_len = q_ref.shape[0], q_ref.shape[1]
    blk = ATTN_BLOCK
    hd = ATTN_HEAD_DIM
    nh = ATTN_HEADS
    group = nh // ATTN_KV_HEADS
    pair_w = 2 * hd
    units = [(bb, h) for bb in range(nb) for h in range(nh)]
    nu = range(len(units))

    lane = lax.broadcasted_iota(jnp.int32, (1, pair_w), 1)
    lo = lane < hd
    half_mask = (lo.astype(F32), 1.0 - lo.astype(F32))
    avg = jnp.where(lax.broadcasted_iota(jnp.int32, (pair_w, pair_w), 0) // hd
                    == lax.broadcasted_iota(jnp.int32, (pair_w, pair_w), 1) // hd, 1.0 / hd, 0.0).astype(BF16)

    def half_mean_sq(x):
        x2 = x * x
        hi = x2.astype(BF16)
        rem = (x2 - hi.astype(F32)).astype(BF16)
        return _dot(hi, avg) + _dot(rem, avg)

    zeros = jnp.zeros((blk, pair_w), BF16)
    for bb in range(nb):
        k = k_ref[bb].astype(F32)
        kn = k * lax.rsqrt(half_mean_sq(k) + EPS) * kg_ref[...]
        ksw = pltpu.roll(kn, hd, axis=1)
        v = v_ref[bb].astype(F32)
        vsw = pltpu.roll(v, hd, axis=1)
        for j in range(ATTN_KV_HEADS):
            kk_ref[bb, j, 0:blk, :] = zeros
            vv_ref[bb, j, 0:blk, :] = zeros
        kk_ref[bb, 0, blk:, :] = jnp.where(lo, kn, ksw).astype(BF16)
        kk_ref[bb, 1, blk:, :] = jnp.where(lo, ksw, kn).astype(BF16)
        vv_ref[bb, 0, blk:, :] = jnp.where(lo, v, vsw).astype(BF16)
        vv_ref[bb, 1, blk:, :] = jnp.where(lo, vsw, v).astype(BF16)

    qi = lax.broadcasted_iota(jnp.int32, (blk, 2 * blk), 0)
    si = lax.broadcasted_iota(jnp.int32, (blk, 2 * blk), 1)
    rel = qi + blk - si
    band = (rel >= 0) & (rel < WINDOW)
    cur = si >= blk
    q_gain = [qg_ref[...] * (hd ** -0.5) * half_mask[i] for i in range(2)]

    def body(n, carry):
        r0 = pl.multiple_of(n * blk, blk)
        valid = band & (cur | (n > 0))
        qm = []
        for bb in range(nb):
            for p in range(nh // 2):
                qp = q_ref[bb, pl.ds(r0, blk), p * pair_w:(p + 1) * pair_w].astype(F32)
                qn = qp * lax.rsqrt(half_mean_sq(qp) + EPS)
                qm += [(qn * q_gain[0]).astype(BF16), (qn * q_gain[1]).astype(BF16)]
        s = [jnp.where(valid, _dot_nt(qm[u], kk_ref[bb, h // group, pl.ds(r0, 2 * blk), :]), -jnp.inf)
             for u, (bb, h) in enumerate(units)]
        m = [jnp.maximum(jnp.max(s[u], axis=-1, keepdims=True), sink_ref[h]) for u, (bb, h) in enumerate(units)]
        p_ = [jnp.exp(s[u] - m[u]) for u in nu]
        inv = [1.0 / (jnp.sum(p_[u], axis=-1, keepdims=True) + jnp.exp(sink_ref[h] - m[u]))
               for u, (bb, h) in enumerate(units)]
        o = [_dot(p_[u].astype(BF16), vv_ref[bb, h // group, pl.ds(r0, 2 * blk), :]) * inv[u]
             for u, (bb, h) in enumerate(units)]
        for bb in range(nb):
            outs = []
            for p in range(nh // 2):
                u = bb * nh + 2 * p
                op = jnp.where(lo, o[u], o[u + 1])
                outs.append(op * lax.rsqrt(half_mean_sq(op) + EPS) * og_ref[:, p * pair_w:(p + 1) * pair_w])
            o_ref[bb, pl.ds(r0, blk), :] = jnp.concatenate(outs, axis=-1).astype(o_ref.dtype)
        return carry

    lax.fori_loop(0, s_len // blk, body, 0)


def _swa(proj3, q_g, k_g2, sinks, out_g, nb=SWA_BATCH):
    b, s, _ = proj3.shape
    return pl.pallas_call(
        _swa_kernel,
        grid=(b // nb,),
        in_specs=[
            pl.BlockSpec((nb, s, ATTN_WIDTH), lambda i: (i, 0, OFF_AQ // ATTN_WIDTH)),
            pl.BlockSpec((nb, s, ATTN_KV_WIDTH), lambda i: (i, 0, OFF_AK // ATTN_KV_WIDTH)),
            pl.BlockSpec((nb, s, ATTN_KV_WIDTH), lambda i: (i, 0, OFF_AV // ATTN_KV_WIDTH)),
            pl.BlockSpec((1, 2 * ATTN_HEAD_DIM), lambda i: (0, 0)),
            pl.BlockSpec((1, ATTN_KV_WIDTH), lambda i: (0, 0)),
            pl.BlockSpec(memory_space=pltpu.SMEM),
            pl.BlockSpec((1, ATTN_WIDTH), lambda i: (0, 0)),
        ],
        out_specs=pl.BlockSpec((nb, s, ATTN_WIDTH), lambda i: (i, 0, 0)),
        out_shape=jax.ShapeDtypeStruct((b, s, ATTN_WIDTH), BF16),
        scratch_shapes=[pltpu.VMEM((nb, ATTN_KV_HEADS, s + ATTN_BLOCK, 2 * ATTN_HEAD_DIM), BF16),
                        pltpu.VMEM((nb, ATTN_KV_HEADS, s + ATTN_BLOCK, 2 * ATTN_HEAD_DIM), BF16)],
        compiler_params=pltpu.CompilerParams(
            dimension_semantics=("parallel",), vmem_limit_bytes=VMEM_LIMIT),
        name="swa",
    )(proj3, proj3, proj3, q_g, k_g2, sinks, out_g)


def _gdn_kernel(qkv_ref, z_ref, gate_ref, cw_ref, alog_ref, dtb_ref, ng_ref, o_ref, halo_ref, state_ref, win_ref):
    nb, ts = qkv_ref.shape[0], qkv_ref.shape[1]
    c = GDN_CHUNK
    hd = GDN_HEAD_DIM
    nh = GDN_HEADS
    hblk = 2 * SUBLANES
    units = [(bb, h) for bb in range(nb) for h in range(nh)]
    nu = range(len(units))

    @pl.when(pl.program_id(1) == 0)
    def _():
        state_ref[...] = jnp.zeros(state_ref.shape, F32)
        halo_ref[...] = jnp.zeros(halo_ref.shape, halo_ref.dtype)

    row = lax.broadcasted_iota(jnp.int32, (c, c), 0)
    col = lax.broadcasted_iota(jnp.int32, (c, c), 1)
    tril = row >= col
    strict = row > col
    blk16 = (row // 16) == (col // 16)
    blk32 = (row // 32) == (col // 32)
    tril_f = tril.astype(F32)
    sel = (lax.broadcasted_iota(jnp.int32, (2 * nh, LANES), 0)
           == lax.broadcasted_iota(jnp.int32, (2 * nh, LANES), 1)).astype(F32)
    is_beta = lax.broadcasted_iota(jnp.int32, (1, LANES), 1) < nh
    neg_a = -jnp.exp(alog_ref[...])
    dtb = dtb_ref[...]

    def chunk(ci, carry):
        t0 = pl.multiple_of(ci * c, c)
        tp = pl.multiple_of(jnp.maximum(t0 - hblk, 0), hblk)
        first = ci == 0

        def conv_silu(bb, lo):
            w = cw_ref[:, lo:lo + hd]
            slot = (bb * qkv_ref.shape[2] + lo) // hd
            prev = jnp.where(first, halo_ref[bb, :, lo:lo + hd], qkv_ref[bb, pl.ds(tp, hblk), lo:lo + hd])
            win_ref[slot, 0:hblk, :] = prev.astype(F32)
            win_ref[slot, hblk:, :] = qkv_ref[bb, pl.ds(t0, c), lo:lo + hd].astype(F32)
            y = w[GDN_CONV_K - 1:GDN_CONV_K] * win_ref[slot, hblk:, :]
            for k in range(1, GDN_CONV_K):
                y = y + w[GDN_CONV_K - 1 - k:GDN_CONV_K - k] * win_ref[slot, hblk - k:hblk - k + c, :]
            return _silu(y)

        gmix, gc_all, rows = [], [], []
        for bb in range(nb):
            gl = gate_ref[bb, pl.ds(t0, c), :]
            gm = jnp.where(is_beta, _sigmoid(gl), neg_a * _softplus(gl + dtb))
            ga = _dot(tril_f, gm, HIGHEST)
            gmix.append(gm)
            gc_all.append(ga)
            rows.append(_dot_nt(sel, jnp.concatenate([gm, ga], axis=0), HIGHEST))

        qs, ks, vs = [], [], []
        for bb, h in units:
            q = conv_silu(bb, h * hd)
            k = conv_silu(bb, GDN_WIDTH + h * hd)
            vs.append(conv_silu(bb, 2 * GDN_WIDTH + h * hd))
            qs.append(q * (lax.rsqrt(jnp.sum(q * q, axis=-1, keepdims=True) + EPS) * (hd ** -0.5)))
            ks.append(k * lax.rsqrt(jnp.sum(k * k, axis=-1, keepdims=True) + EPS))

        kkqk = []
        for u in nu:
            kb = ks[u].astype(BF16)
            kkqk.append(_dot_nt(jnp.concatenate([kb, qs[u].astype(BF16)], axis=0), kb))
        beta = [gmix[bb][:, h:h + 1] for bb, h in units]
        gc = [gc_all[bb][:, nh + h:nh + h + 1] for bb, h in units]
        g_last = [gc_all[bb][c - 1:c, nh + h:nh + h + 1] for bb, h in units]
        beta_row = [rows[bb][h:h + 1, :c] for bb, h in units]
        es, ps, qkb, off_diag = [], [], [], []
        for u, (bb, h) in enumerate(units):
            gc_row = rows[bb][nh + h:nh + h + 1, c:]
            decay = jnp.exp(jnp.where(tril, gc[u] - gc_row, 0.0))
            a = jnp.where(strict, kkqk[u][:c] * beta[u] * decay, 0.0)
            qkb.append(jnp.where(tril, kkqk[u][c:] * decay, 0.0).astype(BF16))
            a_d = jnp.where(blk16, a, 0.0)
            es.append(-a_d)
            ps.append(a_d)
            off_diag.append((jnp.where(blk32, a - a_d, 0.0), jnp.where(blk32, 0.0, a)))
        ps = [_dot(ps[u].astype(BF16), ps[u].astype(BF16)) for u in nu]
        for i in range(3):
            if i < 2:
                ep = [_dot(jnp.concatenate([es[u], ps[u]], axis=0).astype(BF16), ps[u].astype(BF16)) for u in nu]
                es = [es[u] + ps[u] + ep[u][:c] for u in nu]
                ps = [ep[u][c:] for u in nu]
            else:
                ep = [_dot(es[u].astype(BF16), ps[u].astype(BF16)) for u in nu]
                es = [es[u] + ps[u] + ep[u] for u in nu]
        for level in range(2):
            ls = [off_diag[u][level] for u in nu]
            g1 = [ls[u] + _dot(es[u].astype(BF16), ls[u].astype(BF16)) for u in nu]
            es = [es[u] - (g1[u] + _dot(g1[u].astype(BF16), es[u].astype(BF16))) for u in nu]
        egc = [jnp.exp(gc[u]) for u in nu]
        wu = []
        for u in nu:
            kv = jnp.concatenate([ks[u] * egc[u], vs[u]], axis=-1)
            e_b = (es[u] * beta_row[u]).astype(BF16)
            wu.append(kv * beta[u] + _dot(e_b, kv.astype(BF16)))
        st = [state_ref[bb, h] for bb, h in units]
        ws_qs = [_dot(jnp.concatenate([wu[u][:, :hd], qs[u] * egc[u]], axis=0).astype(BF16), st[u].astype(BF16))
                 for u in nu]
        vb = [(wu[u][:, hd:] - ws_qs[u][:c]).astype(BF16) for u in nu]
        kd = [(ks[u] * jnp.exp(g_last[u] - gc[u])).astype(BF16) for u in nu]
        os_ = [ws_qs[u][c:] + _dot(qkb[u], vb[u]) for u in nu]
        for u, (bb, h) in enumerate(units):
            state_ref[bb, h] = st[u] * jnp.exp(g_last[u]) + _dot_tn(kd[u], vb[u])
        for u, (bb, h) in enumerate(units):
            o = os_[u]
            oms = jnp.mean(o * o, axis=-1, keepdims=True)
            zh = z_ref[bb, pl.ds(t0, c), h * hd:(h + 1) * hd].astype(F32)
            y = o * lax.rsqrt(oms + EPS) * ng_ref[...] * _silu(zh)
            o_ref[bb, pl.ds(t0, c), h * hd:(h + 1) * hd] = y.astype(o_ref.dtype)
        return carry

    lax.fori_loop(0, ts // c, chunk, 0)
    halo_ref[...] = qkv_ref[:, ts - hblk:ts, :]


def _gdn(proj3, gate3, conv_w, alog_b, dtb_b, norm_g, ts=256, nb=GDN_BATCH):
    b, s, _ = proj3.shape
    qkv_w = 3 * GDN_WIDTH
    return pl.pallas_call(
        _gdn_kernel,
        grid=(b // nb, s // ts),
        in_specs=[
            pl.BlockSpec((nb, ts, qkv_w), lambda i, t: (i, t, OFF_GQKV // qkv_w)),
            pl.BlockSpec((nb, ts, GDN_WIDTH), lambda i, t: (i, t, OFF_GZ // GDN_WIDTH)),
            pl.BlockSpec((nb, ts, GATE_PAD), lambda i, t: (i, t, 0)),
            pl.BlockSpec((GDN_CONV_K, qkv_w), lambda i, t: (0, 0)),
            pl.BlockSpec((1, LANES), lambda i, t: (0, 0)),
            pl.BlockSpec((1, LANES), lambda i, t: (0, 0)),
            pl.BlockSpec((1, GDN_HEAD_DIM), lambda i, t: (0, 0)),
        ],
        out_specs=pl.BlockSpec((nb, ts, GDN_WIDTH), lambda i, t: (i, t, 0)),
        out_shape=jax.ShapeDtypeStruct((b, s, GDN_WIDTH), BF16),
        scratch_shapes=[pltpu.VMEM((nb, 2 * SUBLANES, qkv_w), BF16),
                        pltpu.VMEM((nb, GDN_HEADS, GDN_HEAD_DIM, GDN_HEAD_DIM), F32),
                        pltpu.VMEM((nb * qkv_w // GDN_HEAD_DIM, 2 * SUBLANES + GDN_CHUNK, GDN_HEAD_DIM), F32)],
        compiler_params=pltpu.CompilerParams(
            dimension_semantics=("parallel", "arbitrary"), vmem_limit_bytes=VMEM_LIMIT),
        name="gdn",
    )(proj3, proj3, gate3, conv_w, alog_b, dtb_b, norm_g)


def _out_proj_kernel(x_ref, yc_ref, ya_ref, yg_ref, w_ref, g_ref, x1_ref, h_ref):
    acc = _dot(yc_ref[...], w_ref[0:CONV_WIDTH, :])
    acc = acc + _dot(ya_ref[...], w_ref[CONV_WIDTH:CONV_WIDTH + ATTN_WIDTH, :])
    acc = acc + _dot(yg_ref[...], w_ref[CONV_WIDTH + ATTN_WIDTH:, :])
    x1 = x_ref[...] + acc
    x1_ref[...] = x1
    ms = jnp.mean(x1 * x1, axis=-1, keepdims=True)
    h_ref[...] = (x1 * lax.rsqrt(ms + EPS) * g_ref[...]).astype(BF16)


def _out_proj(x2, yc, ya, yg, w_all, layer, g, tm=512):
    m = x2.shape[0]
    return pl.pallas_call(
        _out_proj_kernel,
        grid=(m // tm,),
        in_specs=[
            pl.BlockSpec((tm, D_MODEL), lambda i: (i, 0)),
            pl.BlockSpec((tm, CONV_WIDTH), lambda i: (i, 0)),
            pl.BlockSpec((tm, ATTN_WIDTH), lambda i: (i, 0)),
            pl.BlockSpec((tm, GDN_WIDTH), lambda i: (i, 0)),
            pl.BlockSpec((None, D_MODEL, D_MODEL), lambda i: (layer, 0, 0)),
            pl.BlockSpec((1, D_MODEL), lambda i: (0, 0)),
        ],
        out_specs=[pl.BlockSpec((tm, D_MODEL), lambda i: (i, 0)),
                   pl.BlockSpec((tm, D_MODEL), lambda i: (i, 0))],
        out_shape=[jax.ShapeDtypeStruct((m, D_MODEL), F32),
                   jax.ShapeDtypeStruct((m, D_MODEL), BF16)],
        compiler_params=pltpu.CompilerParams(
            dimension_semantics=("parallel",), vmem_limit_bytes=VMEM_LIMIT),
        name="out_proj",
    )(x2, yc, ya, yg, w_all, g)


def _mlp_kernel(x1_ref, h_ref, wu_ref, wd_ref, o_ref):
    @pl.when(pl.program_id(1) == 0)
    def _():
        o_ref[...] = x1_ref[...]

    hid = jnp.maximum(_dot(h_ref[...], wu_ref[...]), 0.0)
    hid = (hid * hid).astype(BF16)
    o_ref[...] += _dot(hid, wd_ref[...])


def _mlp(x1, h, w_up_all, w_down_all, layer, tm=512, tf=2048):
    m = x1.shape[0]
    return pl.pallas_call(
        _mlp_kernel,
        grid=(m // tm, D_FF // tf),
        in_specs=[
            pl.BlockSpec((tm, D_MODEL), lambda i, f: (i, 0)),
            pl.BlockSpec((tm, D_MODEL), lambda i, f: (i, 0)),
            pl.BlockSpec((None, D_MODEL, tf), lambda i, f: (layer, 0, f)),
            pl.BlockSpec((None, tf, D_MODEL), lambda i, f: (layer, f, 0)),
        ],
        out_specs=pl.BlockSpec((tm, D_MODEL), lambda i, f: (i, 0)),
        out_shape=jax.ShapeDtypeStruct((m, D_MODEL), F32),
        compiler_params=pltpu.CompilerParams(
            dimension_semantics=("parallel", "arbitrary"), vmem_limit_bytes=VMEM_LIMIT),
        name="mlp",
    )(x1, h, w_up_all, w_down_all)


def _gate_w_in(w_in):
    gate = w_in[..., D_IN_MAIN:].astype(BF16)
    return jnp.pad(gate, ((0, 0), (0, 0), (0, GATE_PAD - gate.shape[-1])))


def _decay_lanes(p):
    return jnp.zeros((1, LANES), F32).at[0, GDN_HEADS:2 * GDN_HEADS].set(p)


def _layer(x2, b, s, layer, w_in_all, w_gate_all, w_out_all, w_up_all, w_down_all, norm1_g, conv_w, conv_out_g,
           q_norm_g, k_norm_g, attn_sinks, attn_out_g, gdn_conv_w, gdn_A_log, gdn_dt_bias, gdn_norm_g, norm2_g):
    proj, gates = _in_proj(x2, norm1_g[None, :], w_in_all, w_gate_all, layer)
    proj3 = proj.reshape(b, s, D_IN_MAIN)
    yc = _conv_mixer(proj3, conv_w, conv_out_g[None, :])
    ya = _swa(proj3, jnp.tile(q_norm_g, 2)[None, :], jnp.tile(k_norm_g, ATTN_KV_HEADS)[None, :], attn_sinks,
              attn_out_g[None, :])
    yg = _gdn(proj3, gates.reshape(b, s, GATE_PAD), gdn_conv_w, _decay_lanes(gdn_A_log),
              _decay_lanes(gdn_dt_bias), gdn_norm_g[None, :])
    m = b * s
    x1, h2 = _out_proj(x2, yc.reshape(m, CONV_WIDTH), ya.reshape(m, ATTN_WIDTH), yg.reshape(m, GDN_WIDTH),
                       w_out_all, layer, norm2_g[None, :])
    return _mlp(x1, h2, w_up_all, w_down_all, layer)


def kernel(x, norm1_g, w_in, conv_w, conv_out_g, q_norm_g, k_norm_g, attn_sinks, attn_out_g, gdn_conv_w,
           gdn_A_log, gdn_dt_bias, gdn_norm_g, w_out, norm2_g, w_up, w_down):
    b, s, d = x.shape
    x2 = x.reshape(b * s, d)
    w_in_all, w_gate_all = _regroup_w_in(jnp.swapaxes(w_in, 1, 2)), _gate_w_in(w_in)
    w_out_all, w_up_all, w_down_all = w_out.astype(BF16), w_up.astype(BF16), w_down.astype(BF16)
    for l in range(norm1_g.shape[0]):
        x2 = _layer(x2, b, s, l, w_in_all, w_gate_all, w_out_all, w_up_all, w_down_all, norm1_g[l], conv_w[l],
                    conv_out_g[l], q_norm_g[l], k_norm_g[l], attn_sinks[l], attn_out_g[l], gdn_conv_w[l],
                    gdn_A_log[l], gdn_dt_bias[l], gdn_norm_g[l], norm2_g[l])
    return x2.reshape(b, s, d)
```

```python
import jax
import jax.numpy as jnp
from jax import lax
from jax.experimental import pallas as pl
from jax.experimental.pallas import tpu as pltpu

F32 = jnp.float32
BF16 = jnp.bfloat16
HIGHEST = lax.Precision.HIGHEST

D_MODEL = 2048
CONV_WIDTH = 512
CONV_GROUPS = 4
CONV_GROUP_DIM = 128
ATTN_HEAD_DIM = 64
ATTN_HEADS = 8
ATTN_KV_HEADS = 2
ATTN_WIDTH = 512
ATTN_KV_WIDTH = 128
WINDOW = 128
ATTN_BLOCK = 128
SWA_BATCH = 2
GDN_HEAD_DIM = 128
GDN_WIDTH = 1024
GDN_HEADS = 8
GDN_CONV_K = 4
GDN_CHUNK = 64
GDN_BATCH = 2
GDN_INV_BLOCK = 16
assert GDN_CHUNK == 4 * GDN_INV_BLOCK
D_FF = 8192
EPS = 1e-6

LANES = 128
SUBLANES = 8
VMEM_LIMIT = 56 * 1024 * 1024

GATE_PAD = LANES
MXU_WIDTH = 256
REGROUP_BLOCK = MXU_WIDTH
D_IN_MAIN = 4 * GDN_WIDTH + 3 * CONV_WIDTH + ATTN_WIDTH + 2 * ATTN_KV_WIDTH
IN_PROJ_TN = 5 * MXU_WIDTH
assert D_IN_MAIN % IN_PROJ_TN == 0
OFF_GQKV = 0
OFF_GZ = 3 * GDN_WIDTH
OFF_CB = 4 * GDN_WIDTH
OFF_CC = OFF_CB + CONV_WIDTH
OFF_CX = OFF_CC + CONV_WIDTH
OFF_AQ = OFF_CX + CONV_WIDTH
OFF_AK = OFF_AQ + ATTN_WIDTH
OFF_AV = OFF_AK + ATTN_KV_WIDTH
assert OFF_AV + ATTN_KV_WIDTH == D_IN_MAIN


def _dot(a, b, precision=None):
    return jnp.dot(a, b, preferred_element_type=F32, precision=precision)


def _dot_nt(a, b, precision=None):
    return lax.dot_general(a, b, (((1,), (1,)), ((), ())), preferred_element_type=F32, precision=precision)


def _dot_tn(a, b, precision=None):
    return lax.dot_general(a, b, (((0,), (0,)), ((), ())), preferred_element_type=F32, precision=precision)


def _sigmoid(x):
    return 0.5 + 0.5 * jnp.tanh(0.5 * x)


def _silu(x):
    hx = 0.5 * x
    return hx + hx * jnp.tanh(hx)


def _softplus(x):
    return jnp.maximum(x, 0.0) + jnp.log(1.0 + jnp.exp(-jnp.abs(x)))


def _regroup_kernel(w_ref, o_ref):
    o_ref[...] = w_ref[...].astype(o_ref.dtype)


def _regroup_w_in(w_in_t):
    blk = REGROUP_BLOCK
    n_front = (3 * CONV_WIDTH + ATTN_WIDTH + 2 * ATTN_KV_WIDTH) // blk
    n_gdn = 4 * GDN_WIDTH // blk
    layers = w_in_t.shape[0]

    def src_block(l, j):
        return (l, jnp.where(j < n_gdn, j + n_front, j - n_gdn), 0)

    return pl.pallas_call(
        _regroup_kernel,
        grid=(layers, D_IN_MAIN // blk),
        in_specs=[pl.BlockSpec((None, blk, D_MODEL), src_block)],
        out_specs=pl.BlockSpec((None, blk, D_MODEL), lambda l, j: (l, j, 0)),
        out_shape=jax.ShapeDtypeStruct((layers, D_IN_MAIN, D_MODEL), BF16),
        compiler_params=pltpu.CompilerParams(
            dimension_semantics=("parallel", "parallel"), vmem_limit_bytes=VMEM_LIMIT),
        name="regroup_w_in",
    )(w_in_t)


def _in_proj_kernel(x_ref, g_ref, w_ref, wg_ref, o_ref, og_ref):
    x = x_ref[...]
    ms = jnp.mean(x * x, axis=-1, keepdims=True)
    h = (x * lax.rsqrt(ms + EPS) * g_ref[...]).astype(BF16)
    og_ref[...] = _dot(h, wg_ref[...])
    for c0 in range(0, w_ref.shape[0], IN_PROJ_TN):
        o_ref[:, c0:c0 + IN_PROJ_TN] = _dot_nt(h, w_ref[c0:c0 + IN_PROJ_TN, :]).astype(o_ref.dtype)


def _in_proj(x2, g, w_all, wg_all, layer, tm=512):
    m = x2.shape[0]
    n = w_all.shape[1]
    return pl.pallas_call(
        _in_proj_kernel,
        grid=(m // tm,),
        in_specs=[
            pl.BlockSpec((tm, D_MODEL), lambda i: (i, 0)),
            pl.BlockSpec((1, D_MODEL), lambda i: (0, 0)),
            pl.BlockSpec((None, n, D_MODEL), lambda i: (layer, 0, 0), pipeline_mode=pl.Buffered(1)),
            pl.BlockSpec((None, D_MODEL, GATE_PAD), lambda i: (layer, 0, 0), pipeline_mode=pl.Buffered(1)),
        ],
        out_specs=[pl.BlockSpec((tm, n), lambda i: (i, 0)),
                   pl.BlockSpec((tm, GATE_PAD), lambda i: (i, 0))],
        out_shape=[jax.ShapeDtypeStruct((m, n), BF16),
                   jax.ShapeDtypeStruct((m, GATE_PAD), F32)],
        compiler_params=pltpu.CompilerParams(
            dimension_semantics=("parallel",), vmem_limit_bytes=VMEM_LIMIT),
        name="in_proj",
    )(x2, g, w_all, wg_all)


def _conv_kernel(cb_ref, cc_ref, cx_ref, w_ref, g_ref, o_ref, zp_ref):
    s_len = cc_ref.shape[1]
    zp_ref[0:SUBLANES, :] = jnp.zeros((SUBLANES, zp_ref.shape[1]), F32)
    zp_ref[SUBLANES:, :] = cc_ref[0].astype(F32) * cx_ref[0].astype(F32)
    w = w_ref[...]
    y = (w[0:1] * zp_ref[SUBLANES - 2:SUBLANES - 2 + s_len, :] + w[1:2] * zp_ref[SUBLANES - 1:SUBLANES - 1 + s_len, :]
         + w[2:3] * zp_ref[SUBLANES:, :])
    y = cb_ref[0].astype(F32) * y
    ms = jnp.mean(y * y, axis=-1, keepdims=True)
    o_ref[0] = (y * lax.rsqrt(ms + EPS) * g_ref[...]).astype(o_ref.dtype)


def _conv_mixer(proj3, conv_w, conv_out_g):
    b, s, _ = proj3.shape
    gd = CONV_GROUP_DIM

    def col(off):
        return lambda i, g: (i, 0, off // gd + g)

    return pl.pallas_call(
        _conv_kernel,
        grid=(b, CONV_GROUPS),
        in_specs=[
            pl.BlockSpec((1, s, gd), col(OFF_CB)),
            pl.BlockSpec((1, s, gd), col(OFF_CC)),
            pl.BlockSpec((1, s, gd), col(OFF_CX)),
            pl.BlockSpec((3, gd), lambda i, g: (0, g)),
            pl.BlockSpec((1, gd), lambda i, g: (0, g)),
        ],
        out_specs=pl.BlockSpec((1, s, gd), lambda i, g: (i, 0, g)),
        out_shape=jax.ShapeDtypeStruct((b, s, CONV_WIDTH), BF16),
        scratch_shapes=[pltpu.VMEM((SUBLANES + s, gd), F32)],
        compiler_params=pltpu.CompilerParams(
            dimension_semantics=("parallel", "parallel"), vmem_limit_bytes=VMEM_LIMIT),
        name="conv_mixer",
    )(proj3, proj3, proj3, conv_w, conv_out_g)


def _swa_kernel(q_ref, k_ref, v_ref, qg_ref, kg_ref, sink_ref, og_ref, o_ref, kk_ref, vv_ref):
    nb, s_len = q_ref.shape[0], q_ref.shape[1]
    blk = ATTN_BLOCK
    hd = ATTN_HEAD_DIM
    nh = ATTN_HEADS
    group = nh // ATTN_KV_HEADS
    pair_w = 2 * hd
    units = [(bb, h) for bb in range(nb) for h in range(nh)]
    nu = range(len(units))

    lane = lax.broadcasted_iota(jnp.int32, (1, pair_w), 1)
    lo = lane < hd
    half_mask = (lo.astype(F32), 1.0 - lo.astype(F32))
    avg = jnp.where(lax.broadcasted_iota(jnp.int32, (pair_w, pair_w), 0) // hd
                    == lax.broadcasted_iota(jnp.int32, (pair_w, pair_w), 1) // hd, 1.0 / hd, 0.0).astype(BF16)

    def half_mean_sq(x):
        x2 = x * x
        hi = x2.astype(BF16)
        rem = (x2 - hi.astype(F32)).astype(BF16)
        return _dot(hi, avg) + _dot(rem, avg)

    zeros = jnp.zeros((blk, pair_w), BF16)
    for bb in range(nb):
        k = k_ref[bb].astype(F32)
        kn = k * lax.rsqrt(half_mean_sq(k) + EPS) * kg_ref[...]
        ksw = pltpu.roll(kn, hd, axis=1)
        v = v_ref[bb].astype(F32)
        vsw = pltpu.roll(v, hd, axis=1)
        for j in range(ATTN_KV_HEADS):
            kk_ref[bb, j, 0:blk, :] = zeros
            vv_ref[bb, j, 0:blk, :] = zeros
        kk_ref[bb, 0, blk:, :] = jnp.where(lo, kn, ksw).astype(BF16)
        kk_ref[bb, 1, blk:, :] = jnp.where(lo, ksw, kn).astype(BF16)
        vv_ref[bb, 0, blk:, :] = jnp.where(lo, v, vsw).astype(BF16)
        vv_ref[bb, 1, blk:, :] = jnp.where(lo, vsw, v).astype(BF16)

    qi = lax.broadcasted_iota(jnp.int32, (blk, 2 * blk), 0)
    si = lax.broadcasted_iota(jnp.int32, (blk, 2 * blk), 1)
    rel = qi + blk - si
    band = (rel >= 0) & (rel < WINDOW)
    cur = si >= blk
    q_gain = [qg_ref[...] * (hd ** -0.5) * half_mask[i] for i in range(2)]

    def body(n, carry):
        r0 = pl.multiple_of(n * blk, blk)
        valid = band & (cur | (n > 0))
        qm = []
        for bb in range(nb):
            for p in range(nh // 2):
                qp = q_ref[bb, pl.ds(r0, blk), p * pair_w:(p + 1) * pair_w].astype(F32)
                qn = qp * lax.rsqrt(half_mean_sq(qp) + EPS)
                qm += [(qn * q_gain[0]).astype(BF16), (qn * q_gain[1]).astype(BF16)]
        s = [jnp.where(valid, _dot_nt(qm[u], kk_ref[bb, h // group, pl.ds(r0, 2 * blk), :]), -jnp.inf)
             for u, (bb, h) in enumerate(units)]
        m = [jnp.maximum(jnp.max(s[u], axis=-1, keepdims=True), sink_ref[h]) for u, (bb, h) in enumerate(units)]
        p_ = [jnp.exp(s[u] - m[u]) for u in nu]
        inv = [1.0 / (jnp.sum(p_[u], axis=-1, keepdims=True) + jnp.exp(sink_ref[h] - m[u]))
               for u, (bb, h) in enumerate(units)]
        o = [_dot(p_[u].astype(BF16), vv_ref[bb, h // group, pl.ds(r0, 2 * blk), :]) * inv[u]
             for u, (bb, h) in enumerate(units)]
        for bb in range(nb):
            outs = []
            for p in range(nh // 2):
                u = bb * nh + 2 * p
                op = jnp.where(lo, o[u], o[u + 1])
                outs.append(op * lax.rsqrt(half_mean_sq(op) + EPS) * og_ref[:, p * pair_w:(p + 1) * pair_w])
            o_ref[bb, pl.ds(r0, blk), :] = jnp.concatenate(outs, axis=-1).astype(o_ref.dtype)
        return carry

    lax.fori_loop(0, s_len // blk, body, 0)


def _swa(proj3, q_g, k_g2, sinks, out_g, nb=SWA_BATCH):
    b, s, _ = proj3.shape
    return pl.pallas_call(
        _swa_kernel,
        grid=(b // nb,),
        in_specs=[
            pl.BlockSpec((nb, s, ATTN_WIDTH), lambda i: (i, 0, OFF_AQ // ATTN_WIDTH)),
            pl.BlockSpec((nb, s, ATTN_KV_WIDTH), lambda i: (i, 0, OFF_AK // ATTN_KV_WIDTH)),
            pl.BlockSpec((nb, s, ATTN_KV_WIDTH), lambda i: (i, 0, OFF_AV // ATTN_KV_WIDTH)),
            pl.BlockSpec((1, 2 * ATTN_HEAD_DIM), lambda i: (0, 0)),
            pl.BlockSpec((1, ATTN_KV_WIDTH), lambda i: (0, 0)),
            pl.BlockSpec(memory_space=pltpu.SMEM),
            pl.BlockSpec((1, ATTN_WIDTH), lambda i: (0, 0)),
        ],
        out_specs=pl.BlockSpec((nb, s, ATTN_WIDTH), lambda i: (i, 0, 0)),
        out_shape=jax.ShapeDtypeStruct((b, s, ATTN_WIDTH), BF16),
        scratch_shapes=[pltpu.VMEM((nb, ATTN_KV_HEADS, s + ATTN_BLOCK, 2 * ATTN_HEAD_DIM), BF16),
                        pltpu.VMEM((nb, ATTN_KV_HEADS, s + ATTN_BLOCK, 2 * ATTN_HEAD_DIM), BF16)],
        compiler_params=pltpu.CompilerParams(
            dimension_semantics=("parallel",), vmem_limit_bytes=VMEM_LIMIT),
        name="swa",
    )(proj3, proj3, proj3, q_g, k_g2, sinks, out_g)


def _gdn_kernel(qkv_ref, z_ref, gate_ref, cw_ref, alog_ref, dtb_ref, ng_ref, o_ref, halo_ref, state_ref, win_ref):
    nb, ts = qkv_ref.shape[0], qkv_ref.shape[1]
    c = GDN_CHUNK
    hd = GDN_HEAD_DIM
    nh = GDN_HEADS
    hblk = 2 * SUBLANES
    units = [(bb, h) for bb in range(nb) for h in range(nh)]
    nu = range(len(units))

    @pl.when(pl.program_id(1) == 0)
    def _():
        state_ref[...] = jnp.zeros(state_ref.shape, F32)
        halo_ref[...] = jnp.zeros(halo_ref.shape, halo_ref.dtype)

    row = lax.broadcasted_iota(jnp.int32, (c, c), 0)
    col = lax.broadcasted_iota(jnp.int32, (c, c), 1)
    tril = row >= col
    strict = row > col
    blk16 = (row // GDN_INV_BLOCK) == (col // GDN_INV_BLOCK)
    blk32 = (row // (2 * GDN_INV_BLOCK)) == (col // (2 * GDN_INV_BLOCK))
    tril_f = tril.astype(F32)
    sel = (lax.broadcasted_iota(jnp.int32, (2 * nh, LANES), 0)
           == lax.broadcasted_iota(jnp.int32, (2 * nh, LANES), 1)).astype(F32)
    is_beta = lax.broadcasted_iota(jnp.int32, (1, LANES), 1) < nh
    neg_a = -jnp.exp(alog_ref[...])
    dtb = dtb_ref[...]

    def chunk(ci, carry):
        t0 = pl.multiple_of(ci * c, c)
        tp = pl.multiple_of(jnp.maximum(t0 - hblk, 0), hblk)
        first = ci == 0

        for bb in range(nb):
            for lo in range(0, qkv_ref.shape[2], hd):
                slot = (bb * qkv_ref.shape[2] + lo) // hd
                prev = jnp.where(first, halo_ref[bb, :, lo:lo + hd], qkv_ref[bb, pl.ds(tp, hblk), lo:lo + hd])
                win_ref[slot, 0:hblk, :] = prev.astype(F32)
                win_ref[slot, hblk:, :] = qkv_ref[bb, pl.ds(t0, c), lo:lo + hd].astype(F32)

        def conv_silu(bb, lo):
            w = cw_ref[:, lo:lo + hd]
            slot = (bb * qkv_ref.shape[2] + lo) // hd
            y = w[GDN_CONV_K - 1:GDN_CONV_K] * win_ref[slot, hblk:, :]
            for k in range(1, GDN_CONV_K):
                y = y + w[GDN_CONV_K - 1 - k:GDN_CONV_K - k] * win_ref[slot, hblk - k:hblk - k + c, :]
            return _silu(y)

        gmix, gc_all, rows = [], [], []
        for bb in range(nb):
            gl = gate_ref[bb, pl.ds(t0, c), :]
            gm = jnp.where(is_beta, _sigmoid(gl), neg_a * _softplus(gl + dtb))
            ga = _dot(tril_f, gm, HIGHEST)
            gmix.append(gm)
            gc_all.append(ga)
            rows.append(_dot_nt(sel, jnp.concatenate([gm, ga], axis=0), HIGHEST))

        qs, ks, vs = [], [], []
        for bb, h in units:
            q = conv_silu(bb, h * hd)
            k = conv_silu(bb, GDN_WIDTH + h * hd)
            vs.append(conv_silu(bb, 2 * GDN_WIDTH + h * hd))
            qs.append(q * (lax.rsqrt(jnp.sum(q * q, axis=-1, keepdims=True) + EPS) * (hd ** -0.5)))
            ks.append(k * lax.rsqrt(jnp.sum(k * k, axis=-1, keepdims=True) + EPS))

        kkqk = []
        for u in nu:
            kb = ks[u].astype(BF16)
            kkqk.append(_dot_nt(jnp.concatenate([kb, qs[u].astype(BF16)], axis=0), kb))
        beta = [gmix[bb][:, h:h + 1] for bb, h in units]
        gc = [gc_all[bb][:, nh + h:nh + h + 1] for bb, h in units]
        g_last = [gc_all[bb][c - 1:c, nh + h:nh + h + 1] for bb, h in units]
        beta_row = [rows[bb][h:h + 1, :c] for bb, h in units]
        es, ps, qkb, off_diag = [], [], [], []
        for u, (bb, h) in enumerate(units):
            gc_row = rows[bb][nh + h:nh + h + 1, c:]
            decay = jnp.exp(jnp.where(tril, gc[u] - gc_row, 0.0))
            a = jnp.where(strict, kkqk[u][:c] * beta[u] * decay, 0.0)
            qkb.append(jnp.where(tril, kkqk[u][c:] * decay, 0.0).astype(BF16))
            a_d = jnp.where(blk16, a, 0.0)
            es.append(-a_d)
            ps.append(a_d)
            off_diag.append((jnp.where(blk32, a - a_d, 0.0), jnp.where(blk32, 0.0, a)))
        ps = [_dot(ps[u].astype(BF16), ps[u].astype(BF16)) for u in nu]
        for i in range(3):
            if i < 2:
                ep = [_dot(jnp.concatenate([es[u], ps[u]], axis=0).astype(BF16), ps[u].astype(BF16)) for u in nu]
                es = [es[u] + ps[u] + ep[u][:c] for u in nu]
                ps = [ep[u][c:] for u in nu]
            else:
                ep = [_dot(es[u].astype(BF16), ps[u].astype(BF16)) for u in nu]
                es = [es[u] + ps[u] + ep[u] for u in nu]
        for level in range(2):
            ls = [off_diag[u][level] for u in nu]
            g1 = [ls[u] + _dot(es[u].astype(BF16), ls[u].astype(BF16)) for u in nu]
            es = [es[u] - (g1[u] + _dot(g1[u].astype(BF16), es[u].astype(BF16))) for u in nu]
        egc = [jnp.exp(gc[u]) for u in nu]
        wu = []
        for u in nu:
            kv = jnp.concatenate([ks[u] * egc[u], vs[u]], axis=-1)
            e_b = (es[u] * beta_row[u]).astype(BF16)
            wu.append(kv * beta[u] + _dot(e_b, kv.astype(BF16)))
        st = [state_ref[bb, h] for bb, h in units]
        ws_qs = [_dot(jnp.concatenate([wu[u][:, :hd], qs[u] * egc[u]], axis=0).astype(BF16), st[u].astype(BF16))
                 for u in nu]
        vb = [(wu[u][:, hd:] - ws_qs[u][:c]).astype(BF16) for u in nu]
        kd = [(ks[u] * jnp.exp(g_last[u] - gc[u])).astype(BF16) for u in nu]
        os_ = [ws_qs[u][c:] + _dot(qkb[u], vb[u]) for u in nu]
        for u, (bb, h) in enumerate(units):
            state_ref[bb, h] = st[u] * jnp.exp(g_last[u]) + _dot_tn(kd[u], vb[u])
        for u, (bb, h) in enumerate(units):
            o = os_[u]
            oms = jnp.mean(o * o, axis=-1, keepdims=True)
            zh = z_ref[bb, pl.ds(t0, c), h * hd:(h + 1) * hd].astype(F32)
            y = o * lax.rsqrt(oms + EPS) * ng_ref[...] * _silu(zh)
            o_ref[bb, pl.ds(t0, c), h * hd:(h + 1) * hd] = y.astype(o_ref.dtype)
        return carry

    lax.fori_loop(0, ts // c, chunk, 0)
    halo_ref[...] = qkv_ref[:, ts - hblk:ts, :]


def _gdn(proj3, gate3, conv_w, alog_b, dtb_b, norm_g, ts=256, nb=GDN_BATCH):
    b, s, _ = proj3.shape
    qkv_w = 3 * GDN_WIDTH
    return pl.pallas_call(
        _gdn_kernel,
        grid=(b // nb, s // ts),
        in_specs=[
            pl.BlockSpec((nb, ts, qkv_w), lambda i, t: (i, t, OFF_GQKV // qkv_w)),
            pl.BlockSpec((nb, ts, GDN_WIDTH), lambda i, t: (i, t, OFF_GZ // GDN_WIDTH)),
            pl.BlockSpec((nb, ts, GATE_PAD), lambda i, t: (i, t, 0)),
            pl.BlockSpec((GDN_CONV_K, qkv_w), lambda i, t: (0, 0)),
            pl.BlockSpec((1, LANES), lambda i, t: (0, 0)),
            pl.BlockSpec((1, LANES), lambda i, t: (0, 0)),
            pl.BlockSpec((1, GDN_HEAD_DIM), lambda i, t: (0, 0)),
        ],
        out_specs=pl.BlockSpec((nb, ts, GDN_WIDTH), lambda i, t: (i, t, 0)),
        out_shape=jax.ShapeDtypeStruct((b, s, GDN_WIDTH), BF16),
        scratch_shapes=[pltpu.VMEM((nb, 2 * SUBLANES, qkv_w), BF16),
                        pltpu.VMEM((nb, GDN_HEADS, GDN_HEAD_DIM, GDN_HEAD_DIM), F32),
                        pltpu.VMEM((nb * qkv_w // GDN_HEAD_DIM, 2 * SUBLANES + GDN_CHUNK, GDN_HEAD_DIM), F32)],
        compiler_params=pltpu.CompilerParams(
            dimension_semantics=("parallel", "arbitrary"), vmem_limit_bytes=VMEM_LIMIT),
        name="gdn",
    )(proj3, proj3, gate3, conv_w, alog_b, dtb_b, norm_g)


def _out_proj_kernel(x_ref, yc_ref, ya_ref, yg_ref, w_ref, g_ref, x1_ref, h_ref):
    acc = _dot(yc_ref[...], w_ref[0:CONV_WIDTH, :])
    acc = acc + _dot(ya_ref[...], w_ref[CONV_WIDTH:CONV_WIDTH + ATTN_WIDTH, :])
    acc = acc + _dot(yg_ref[...], w_ref[CONV_WIDTH + ATTN_WIDTH:, :])
    x1 = x_ref[...] + acc
    x1_ref[...] = x1
    ms = jnp.mean(x1 * x1, axis=-1, keepdims=True)
    h_ref[...] = (x1 * lax.rsqrt(ms + EPS) * g_ref[...]).astype(BF16)


def _out_proj(x2, yc, ya, yg, w_all, layer, g, tm=512):
    m = x2.shape[0]
    return pl.pallas_call(
        _out_proj_kernel,
        grid=(m // tm,),
        in_specs=[
            pl.BlockSpec((tm, D_MODEL), lambda i: (i, 0)),
            pl.BlockSpec((tm, CONV_WIDTH), lambda i: (i, 0)),
            pl.BlockSpec((tm, ATTN_WIDTH), lambda i: (i, 0)),
            pl.BlockSpec((tm, GDN_WIDTH), lambda i: (i, 0)),
            pl.BlockSpec((None, D_MODEL, D_MODEL), lambda i: (layer, 0, 0)),
            pl.BlockSpec((1, D_MODEL), lambda i: (0, 0)),
        ],
        out_specs=[pl.BlockSpec((tm, D_MODEL), lambda i: (i, 0)),
                   pl.BlockSpec((tm, D_MODEL), lambda i: (i, 0))],
        out_shape=[jax.ShapeDtypeStruct((m, D_MODEL), F32),
                   jax.ShapeDtypeStruct((m, D_MODEL), BF16)],
        compiler_params=pltpu.CompilerParams(
            dimension_semantics=("parallel",), vmem_limit_bytes=VMEM_LIMIT),
        name="out_proj",
    )(x2, yc, ya, yg, w_all, g)


def _mlp_kernel(x1_ref, h_ref, wu_ref, wd_ref, o_ref):
    @pl.when(pl.program_id(1) == 0)
    def _():
        o_ref[...] = x1_ref[...]

    hid = jnp.maximum(_dot(h_ref[...], wu_ref[...]), 0.0)
    hid = (hid * hid).astype(BF16)
    o_ref[...] += _dot(hid, wd_ref[...])


def _mlp(x1, h, w_up_all, w_down_all, layer, tm=512, tf=2048):
    m = x1.shape[0]
    return pl.pallas_call(
        _mlp_kernel,
        grid=(m // tm, D_FF // tf),
        in_specs=[
            pl.BlockSpec((tm, D_MODEL), lambda i, f: (i, 0)),
            pl.BlockSpec((tm, D_MODEL), lambda i, f: (i, 0)),
            pl.BlockSpec((None, D_MODEL, tf), lambda i, f: (layer, 0, f)),
            pl.BlockSpec((None, tf, D_MODEL), lambda i, f: (layer, f, 0)),
        ],
        out_specs=pl.BlockSpec((tm, D_MODEL), lambda i, f: (i, 0)),
        out_shape=jax.ShapeDtypeStruct((m, D_MODEL), F32),
        compiler_params=pltpu.CompilerParams(
            dimension_semantics=("parallel", "arbitrary"), vmem_limit_bytes=VMEM_LIMIT),
        name="mlp",
    )(x1, h, w_up_all, w_down_all)


def _gate_w_in(w_in):
    gate = w_in[..., D_IN_MAIN:].astype(BF16)
    return jnp.pad(gate, ((0, 0), (0, 0), (0, GATE_PAD - gate.shape[-1])))


def _decay_lanes(p):
    return jnp.zeros((1, LANES), F32).at[0, GDN_HEADS:2 * GDN_HEADS].set(p)


def _layer(x2, b, s, layer, w_in_all, w_gate_all, w_out_all, w_up_all, w_down_all, norm1_g, conv_w, conv_out_g,
           q_norm_g, k_norm_g, attn_sinks, attn_out_g, gdn_conv_w, gdn_A_log, gdn_dt_bias, gdn_norm_g, norm2_g):
    proj, gates = _in_proj(x2, norm1_g[None, :], w_in_all, w_gate_all, layer)
    proj3 = proj.reshape(b, s, D_IN_MAIN)
    yc = _conv_mixer(proj3, conv_w, conv_out_g[None, :])
    ya = _swa(proj3, jnp.tile(q_norm_g, 2)[None, :], jnp.tile(k_norm_g, ATTN_KV_HEADS)[None, :], attn_sinks,
              attn_out_g[None, :])
    yg = _gdn(proj3, gates.reshape(b, s, GATE_PAD), gdn_conv_w, _decay_lanes(gdn_A_log),
              _decay_lanes(gdn_dt_bias), gdn_norm_g[None, :])
    m = b * s
    x1, h2 = _out_proj(x2, yc.reshape(m, CONV_WIDTH), ya.reshape(m, ATTN_WIDTH), yg.reshape(m, GDN_WIDTH),
                       w_out_all, layer, norm2_g[None, :])
    return _mlp(x1, h2, w_up_all, w_down_all, layer)


def kernel(x, norm1_g, w_in, conv_w, conv_out_g, q_norm_g, k_norm_g, attn_sinks, attn_out_g, gdn_conv_w,
           gdn_A_log, gdn_dt_bias, gdn_norm_g, w_out, norm2_g, w_up, w_down):
    b, s, d = x.shape
    x2 = x.reshape(b * s, d)
    w_in_all, w_gate_all = _regroup_w_in(jnp.swapaxes(w_in, 1, 2)), _gate_w_in(w_in)
    w_out_all, w_up_all, w_down_all = w_out.astype(BF16), w_up.astype(BF16), w_down.astype(BF16)
    for l in range(norm1_g.shape[0]):
        x2 = _layer(x2, b, s, l, w_in_all, w_gate_all, w_out_all, w_up_all, w_down_all, norm1_g[l], conv_w[l],
                    conv_out_g[l], q_norm_g[l], k_norm_g[l], attn_sinks[l], attn_out_g[l], gdn_conv_w[l],
                    gdn_A_log[l], gdn_dt_bias[l], gdn_norm_g[l], norm2_g[l])
    return x2.reshape(b, s, d)
```

```python
import jax
import jax.numpy as jnp
from jax import lax
from jax.experimental import pallas as pl
from jax.experimental.pallas import tpu as pltpu

F32 = jnp.float32
BF16 = jnp.bfloat16
HIGHEST = lax.Precision.HIGHEST

D_MODEL = 2048
CONV_WIDTH = 512
CONV_GROUPS = 4
CONV_GROUP_DIM = 128
ATTN_HEAD_DIM = 64
ATTN_HEADS = 8
ATTN_KV_HEADS = 2
ATTN_WIDTH = 512
ATTN_KV_WIDTH = 128
WINDOW = 128
ATTN_BLOCK = 128
SWA_BATCH = 2
GDN_HEAD_DIM = 128
GDN_WIDTH = 1024
GDN_HEADS = 8
GDN_CONV_K = 4
GDN_CHUNK = 64
GDN_BATCH = 2
GDN_INV_BLOCK = 16
assert GDN_CHUNK == 4 * GDN_INV_BLOCK
D_FF = 8192
EPS = 1e-6

LANES = 128
SUBLANES = 8
VMEM_LIMIT = 56 * 1024 * 1024

GATE_PAD = LANES
MXU_WIDTH = 256
REGROUP_BLOCK = MXU_WIDTH
D_IN_MAIN = 4 * GDN_WIDTH + 3 * CONV_WIDTH + ATTN_WIDTH + 2 * ATTN_KV_WIDTH
IN_PROJ_TN = 5 * MXU_WIDTH
assert D_IN_MAIN % IN_PROJ_TN == 0
OFF_GQKV = 0
OFF_GZ = 3 * GDN_WIDTH
OFF_CB = 4 * GDN_WIDTH
OFF_CC = OFF_CB + CONV_WIDTH
OFF_CX = OFF_CC + CONV_WIDTH
OFF_AQ = OFF_CX + CONV_WIDTH
OFF_AK = OFF_AQ + ATTN_WIDTH
OFF_AV = OFF_AK + ATTN_KV_WIDTH
assert OFF_AV + ATTN_KV_WIDTH == D_IN_MAIN


def _dot(a, b, precision=None):
    return jnp.dot(a, b, preferred_element_type=F32, precision=precision)


def _dot_nt(a, b, precision=None):
    return lax.dot_general(a, b, (((1,), (1,)), ((), ())), preferred_element_type=F32, precision=precision)


def _dot_tn(a, b, precision=None):
    return lax.dot_general(a, b, (((0,), (0,)), ((), ())), preferred_element_type=F32, precision=precision)


def _sigmoid(x):
    return 0.5 + 0.5 * jnp.tanh(0.5 * x)


def _silu(x):
    hx = 0.5 * x
    return hx + hx * jnp.tanh(hx)


def _softplus(x):
    return jnp.maximum(x, 0.0) + jnp.log(1.0 + jnp.exp(-jnp.abs(x)))


def _regroup_kernel(w_ref, o_ref):
    o_ref[...] = w_ref[...].astype(o_ref.dtype)


def _regroup_w_in(w_in_t):
    blk = REGROUP_BLOCK
    n_front = (3 * CONV_WIDTH + ATTN_WIDTH + 2 * ATTN_KV_WIDTH) // blk
    n_gdn = 4 * GDN_WIDTH // blk
    layers = w_in_t.shape[0]

    def src_block(l, j):
        return (l, jnp.where(j < n_gdn, j + n_front, j - n_gdn), 0)

    return pl.pallas_call(
        _regroup_kernel,
        grid=(layers, D_IN_MAIN // blk),
        in_specs=[pl.BlockSpec((None, blk, D_MODEL), src_block)],
        out_specs=pl.BlockSpec((None, blk, D_MODEL), lambda l, j: (l, j, 0)),
        out_shape=jax.ShapeDtypeStruct((layers, D_IN_MAIN, D_MODEL), BF16),
        compiler_params=pltpu.CompilerParams(
            dimension_semantics=("parallel", "parallel"), vmem_limit_bytes=VMEM_LIMIT),
        name="regroup_w_in",
    )(w_in_t)


def _in_proj_kernel(x_ref, g_ref, w_ref, wg_ref, o_ref, og_ref):
    x = x_ref[...]
    ms = jnp.mean(x * x, axis=-1, keepdims=True)
    h = (x * lax.rsqrt(ms + EPS) * g_ref[...]).astype(BF16)
    og_ref[...] = _dot(h, wg_ref[...])
    for c0 in range(0, w_ref.shape[0], IN_PROJ_TN):
        o_ref[:, c0:c0 + IN_PROJ_TN] = _dot_nt(h, w_ref[c0:c0 + IN_PROJ_TN, :]).astype(o_ref.dtype)


def _in_proj(x2, g, w_all, wg_all, layer, tm=512):
    m = x2.shape[0]
    n = w_all.shape[1]
    return pl.pallas_call(
        _in_proj_kernel,
        grid=(m // tm,),
        in_specs=[
            pl.BlockSpec((tm, D_MODEL), lambda i: (i, 0)),
            pl.BlockSpec((1, D_MODEL), lambda i: (0, 0)),
            pl.BlockSpec((None, n, D_MODEL), lambda i: (layer, 0, 0), pipeline_mode=pl.Buffered(1)),
            pl.BlockSpec((None, D_MODEL, GATE_PAD), lambda i: (layer, 0, 0), pipeline_mode=pl.Buffered(1)),
        ],
        out_specs=[pl.BlockSpec((tm, n), lambda i: (i, 0)),
                   pl.BlockSpec((tm, GATE_PAD), lambda i: (i, 0))],
        out_shape=[jax.ShapeDtypeStruct((m, n), BF16),
                   jax.ShapeDtypeStruct((m, GATE_PAD), F32)],
        compiler_params=pltpu.CompilerParams(
            dimension_semantics=("parallel",), vmem_limit_bytes=VMEM_LIMIT),
        name="in_proj",
    )(x2, g, w_all, wg_all)


def _conv_kernel(cb_ref, cc_ref, cx_ref, w_ref, g_ref, o_ref, zp_ref):
    s_len = cc_ref.shape[1]
    zp_ref[0:SUBLANES, :] = jnp.zeros((SUBLANES, zp_ref.shape[1]), F32)
    zp_ref[SUBLANES:, :] = cc_ref[0].astype(F32) * cx_ref[0].astype(F32)
    w = w_ref[...]
    y = (w[0:1] * zp_ref[SUBLANES - 2:SUBLANES - 2 + s_len, :] + w[1:2] * zp_ref[SUBLANES - 1:SUBLANES - 1 + s_len, :]
         + w[2:3] * zp_ref[SUBLANES:, :])
    y = cb_ref[0].astype(F32) * y
    ms = jnp.mean(y * y, axis=-1, keepdims=True)
    o_ref[0] = (y * lax.rsqrt(ms + EPS) * g_ref[...]).astype(o_ref.dtype)


def _conv_mixer(proj3, conv_w, conv_out_g):
    b, s, _ = proj3.shape
    gd = CONV_GROUP_DIM

    def col(off):
        return lambda i, g: (i, 0, off // gd + g)

    return pl.pallas_call(
        _conv_kernel,
        grid=(b, CONV_GROUPS),
        in_specs=[
            pl.BlockSpec((1, s, gd), col(OFF_CB)),
            pl.BlockSpec((1, s, gd), col(OFF_CC)),
            pl.BlockSpec((1, s, gd), col(OFF_CX)),
            pl.BlockSpec((3, gd), lambda i, g: (0, g)),
            pl.BlockSpec((1, gd), lambda i, g: (0, g)),
        ],
        out_specs=pl.BlockSpec((1, s, gd), lambda i, g: (i, 0, g)),
        out_shape=jax.ShapeDtypeStruct((b, s, CONV_WIDTH), BF16),
        scratch_shapes=[pltpu.VMEM((SUBLANES + s, gd), F32)],
        compiler_params=pltpu.CompilerParams(
            dimension_semantics=("parallel", "parallel"), vmem_limit_bytes=VMEM_LIMIT),
        name="conv_mixer",
    )(proj3, proj3, proj3, conv_w, conv_out_g)


def _swa_kernel(q_ref, k_ref, v_ref, qg_ref, kg_ref, sink_ref, og_ref, o_ref, kk_ref, vv_ref):
    nb, s_len = q_ref.shape[0], q_ref.shape[1]
    blk = ATTN_BLOCK
    hd = ATTN_HEAD_DIM
    nh = ATTN_HEADS
    group = nh // ATTN_KV_HEADS
    pair_w = 2 * hd
    units = [(bb, h) for bb in range(nb) for h in range(nh)]
    nu = range(len(units))

    lane = lax.broadcasted_iota(jnp.int32, (1, pair_w), 1)
    lo = lane < hd
    half_mask = (lo.astype(F32), 1.0 - lo.astype(F32))
    avg = jnp.where(lax.broadcasted_iota(jnp.int32, (pair_w, pair_w), 0) // hd
                    == lax.broadcasted_iota(jnp.int32, (pair_w, pair_w), 1) // hd, 1.0 / hd, 0.0).astype(BF16)

    def half_mean_sq(x):
        x2 = x * x
        hi = x2.astype(BF16)
        rem = (x2 - hi.astype(F32)).astype(BF16)
        return _dot(hi, avg) + _dot(rem, avg)

    zeros = jnp.zeros((blk, pair_w), BF16)
    for bb in range(nb):
        k = k_ref[bb].astype(F32)
        kn = k * lax.rsqrt(half_mean_sq(k) + EPS) * kg_ref[...]
        ksw = pltpu.roll(kn, hd, axis=1)
        v = v_ref[bb].astype(F32)
        vsw = pltpu.roll(v, hd, axis=1)
        for j in range(ATTN_KV_HEADS):
            kk_ref[bb, j, 0:blk, :] = zeros
            vv_ref[bb, j, 0:blk, :] = zeros
        kk_ref[bb, 0, blk:, :] = jnp.where(lo, kn, ksw).astype(BF16)
        kk_ref[bb, 1, blk:, :] = jnp.where(lo, ksw, kn).astype(BF16)
        vv_ref[bb, 0, blk:, :] = jnp.where(lo, v, vsw).astype(BF16)
        vv_ref[bb, 1, blk:, :] = jnp.where(lo, vsw, v).astype(BF16)

    qi = lax.broadcasted_iota(jnp.int32, (blk, 2 * blk), 0)
    si = lax.broadcasted_iota(jnp.int32, (blk, 2 * blk), 1)
    rel = qi + blk - si
    band = (rel >= 0) & (rel < WINDOW)
    cur = si >= blk
    q_gain = [qg_ref[...] * (hd ** -0.5) * half_mask[i] for i in range(2)]

    def body(n, carry):
        r0 = pl.multiple_of(n * blk, blk)
        valid = band & (cur | (n > 0))
        qm = []
        for bb in range(nb):
            for p in range(nh // 2):
                qp = q_ref[bb, pl.ds(r0, blk), p * pair_w:(p + 1) * pair_w].astype(F32)
                qn = qp * lax.rsqrt(half_mean_sq(qp) + EPS)
                qm += [(qn * q_gain[0]).astype(BF16), (qn * q_gain[1]).astype(BF16)]
        s = [jnp.where(valid, _dot_nt(qm[u], kk_ref[bb, h // group, pl.ds(r0, 2 * blk), :]), -jnp.inf)
             for u, (bb, h) in enumerate(units)]
        m = [jnp.maximum(jnp.max(s[u], axis=-1, keepdims=True), sink_ref[h]) for u, (bb, h) in enumerate(units)]
        p_ = [jnp.exp(s[u] - m[u]) for u in nu]
        inv = [1.0 / (jnp.sum(p_[u], axis=-1, keepdims=True) + jnp.exp(sink_ref[h] - m[u]))
               for u, (bb, h) in enumerate(units)]
        o = [_dot(p_[u].astype(BF16), vv_ref[bb, h // group, pl.ds(r0, 2 * blk), :]) * inv[u]
             for u, (bb, h) in enumerate(units)]
        for bb in range(nb):
            outs = []
            for p in range(nh // 2):
                u = bb * nh + 2 * p
                op = jnp.where(lo, o[u], o[u + 1])
                outs.append(op * lax.rsqrt(half_mean_sq(op) + EPS) * og_ref[:, p * pair_w:(p + 1) * pair_w])
            o_ref[bb, pl.ds(r0, blk), :] = jnp.concatenate(outs, axis=-1).astype(o_ref.dtype)
        return carry

    lax.fori_loop(0, s_len // blk, body, 0)


def _swa(proj3, q_g, k_g2, sinks, out_g, nb=SWA_BATCH):
    b, s, _ = proj3.shape
    return pl.pallas_call(
        _swa_kernel,
        grid=(b // nb,),
        in_specs=[
            pl.BlockSpec((nb, s, ATTN_WIDTH), lambda i: (i, 0, OFF_AQ // ATTN_WIDTH)),
            pl.BlockSpec((nb, s, ATTN_KV_WIDTH), lambda i: (i, 0, OFF_AK // ATTN_KV_WIDTH)),
            pl.BlockSpec((nb, s, ATTN_KV_WIDTH), lambda i: (i, 0, OFF_AV // ATTN_KV_WIDTH)),
            pl.BlockSpec((1, 2 * ATTN_HEAD_DIM), lambda i: (0, 0)),
            pl.BlockSpec((1, ATTN_KV_WIDTH), lambda i: (0, 0)),
            pl.BlockSpec(memory_space=pltpu.SMEM),
            pl.BlockSpec((1, ATTN_WIDTH), lambda i: (0, 0)),
        ],
        out_specs=pl.BlockSpec((nb, s, ATTN_WIDTH), lambda i: (i, 0, 0)),
        out_shape=jax.ShapeDtypeStruct((b, s, ATTN_WIDTH), BF16),
        scratch_shapes=[pltpu.VMEM((nb, ATTN_KV_HEADS, s + ATTN_BLOCK, 2 * ATTN_HEAD_DIM), BF16),
                        pltpu.VMEM((nb, ATTN_KV_HEADS, s + ATTN_BLOCK, 2 * ATTN_HEAD_DIM), BF16)],
        compiler_params=pltpu.CompilerParams(
            dimension_semantics=("parallel",), vmem_limit_bytes=VMEM_LIMIT),
        name="swa",
    )(proj3, proj3, proj3, q_g, k_g2, sinks, out_g)


def _gdn_kernel(qkv_ref, z_ref, gate_ref, cw_ref, alog_ref, dtb_ref, ng_ref, wo_ref, wu_ref, wd_ref,
                o_ref, wo_b_ref, wu_b_ref, wd_b_ref, halo_ref, state_ref, win_ref):
    wo_b_ref[...] = wo_ref[...].astype(wo_b_ref.dtype)
    wu_b_ref[...] = wu_ref[...].astype(wu_b_ref.dtype)
    wd_b_ref[...] = wd_ref[...].astype(wd_b_ref.dtype)
    nb, ts = qkv_ref.shape[0], qkv_ref.shape[1]
    c = GDN_CHUNK
    hd = GDN_HEAD_DIM
    nh = GDN_HEADS
    hblk = 2 * SUBLANES
    units = [(bb, h) for bb in range(nb) for h in range(nh)]
    nu = range(len(units))

    @pl.when(pl.program_id(1) == 0)
    def _():
        state_ref[...] = jnp.zeros(state_ref.shape, F32)
        halo_ref[...] = jnp.zeros(halo_ref.shape, halo_ref.dtype)

    row = lax.broadcasted_iota(jnp.int32, (c, c), 0)
    col = lax.broadcasted_iota(jnp.int32, (c, c), 1)
    tril = row >= col
    strict = row > col
    blk16 = (row // GDN_INV_BLOCK) == (col // GDN_INV_BLOCK)
    blk32 = (row // (2 * GDN_INV_BLOCK)) == (col // (2 * GDN_INV_BLOCK))
    tril_f = tril.astype(F32)
    sel = (lax.broadcasted_iota(jnp.int32, (2 * nh, LANES), 0)
           == lax.broadcasted_iota(jnp.int32, (2 * nh, LANES), 1)).astype(F32)
    is_beta = lax.broadcasted_iota(jnp.int32, (1, LANES), 1) < nh
    neg_a = -jnp.exp(alog_ref[...])
    dtb = dtb_ref[...]

    def chunk(ci, carry):
        t0 = pl.multiple_of(ci * c, c)
        tp = pl.multiple_of(jnp.maximum(t0 - hblk, 0), hblk)
        first = ci == 0

        def conv_silu(bb, lo):
            w = cw_ref[:, lo:lo + hd]
            slot = (bb * qkv_ref.shape[2] + lo) // hd
            prev = jnp.where(first, halo_ref[bb, :, lo:lo + hd], qkv_ref[bb, pl.ds(tp, hblk), lo:lo + hd])
            win_ref[slot, 0:hblk, :] = prev.astype(F32)
            win_ref[slot, hblk:, :] = qkv_ref[bb, pl.ds(t0, c), lo:lo + hd].astype(F32)
            y = w[GDN_CONV_K - 1:GDN_CONV_K] * win_ref[slot, hblk:, :]
            for k in range(1, GDN_CONV_K):
                y = y + w[GDN_CONV_K - 1 - k:GDN_CONV_K - k] * win_ref[slot, hblk - k:hblk - k + c, :]
            return _silu(y)

        gmix, gc_all, rows = [], [], []
        for bb in range(nb):
            gl = gate_ref[bb, pl.ds(t0, c), :]
            gm = jnp.where(is_beta, _sigmoid(gl), neg_a * _softplus(gl + dtb))
            ga = _dot(tril_f, gm, HIGHEST)
            gmix.append(gm)
            gc_all.append(ga)
            rows.append(_dot_nt(sel, jnp.concatenate([gm, ga], axis=0), HIGHEST))

        qs, ks, vs = [], [], []
        for bb, h in units:
            q = conv_silu(bb, h * hd)
            k = conv_silu(bb, GDN_WIDTH + h * hd)
            vs.append(conv_silu(bb, 2 * GDN_WIDTH + h * hd))
            qs.append(q * (lax.rsqrt(jnp.sum(q * q, axis=-1, keepdims=True) + EPS) * (hd ** -0.5)))
            ks.append(k * lax.rsqrt(jnp.sum(k * k, axis=-1, keepdims=True) + EPS))

        kkqk = []
        for u in nu:
            kb = ks[u].astype(BF16)
            kkqk.append(_dot_nt(jnp.concatenate([kb, qs[u].astype(BF16)], axis=0), kb))
        beta = [gmix[bb][:, h:h + 1] for bb, h in units]
        gc = [gc_all[bb][:, nh + h:nh + h + 1] for bb, h in units]
        g_last = [gc_all[bb][c - 1:c, nh + h:nh + h + 1] for bb, h in units]
        beta_row = [rows[bb][h:h + 1, :c] for bb, h in units]
        es, ps, qkb, off_diag = [], [], [], []
        for u, (bb, h) in enumerate(units):
            gc_row = rows[bb][nh + h:nh + h + 1, c:]
            decay = jnp.exp(jnp.where(tril, gc[u] - gc_row, 0.0))
            a = jnp.where(strict, kkqk[u][:c] * beta[u] * decay, 0.0)
            qkb.append(jnp.where(tril, kkqk[u][c:] * decay, 0.0).astype(BF16))
            a_d = jnp.where(blk16, a, 0.0)
            es.append(-a_d)
            ps.append(a_d)
            off_diag.append((jnp.where(blk32, a - a_d, 0.0), jnp.where(blk32, 0.0, a)))
        ps = [_dot(ps[u].astype(BF16), ps[u].astype(BF16)) for u in nu]
        for i in range(3):
            if i < 2:
                ep = [_dot(jnp.concatenate([es[u], ps[u]], axis=0).astype(BF16), ps[u].astype(BF16)) for u in nu]
                es = [es[u] + ps[u] + ep[u][:c] for u in nu]
                ps = [ep[u][c:] for u in nu]
            else:
                ep = [_dot(es[u].astype(BF16), ps[u].astype(BF16)) for u in nu]
                es = [es[u] + ps[u] + ep[u] for u in nu]
        for level in range(2):
            ls = [off_diag[u][level] for u in nu]
            g1 = [ls[u] + _dot(es[u].astype(BF16), ls[u].astype(BF16)) for u in nu]
            es = [es[u] - (g1[u] + _dot(g1[u].astype(BF16), es[u].astype(BF16))) for u in nu]
        egc = [jnp.exp(gc[u]) for u in nu]
        wu = []
        for u in nu:
            kv = jnp.concatenate([ks[u] * egc[u], vs[u]], axis=-1)
            e_b = (es[u] * beta_row[u]).astype(BF16)
            wu.append(kv * beta[u] + _dot(e_b, kv.astype(BF16)))
        st = [state_ref[bb, h] for bb, h in units]
        ws_qs = [_dot(jnp.concatenate([wu[u][:, :hd], qs[u] * egc[u]], axis=0).astype(BF16), st[u].astype(BF16))
                 for u in nu]
        vb = [(wu[u][:, hd:] - ws_qs[u][:c]).astype(BF16) for u in nu]
        kd = [(ks[u] * jnp.exp(g_last[u] - gc[u])).astype(BF16) for u in nu]
        os_ = [ws_qs[u][c:] + _dot(qkb[u], vb[u]) for u in nu]
        for u, (bb, h) in enumerate(units):
            state_ref[bb, h] = st[u] * jnp.exp(g_last[u]) + _dot_tn(kd[u], vb[u])
        for u, (bb, h) in enumerate(units):
            o = os_[u]
            oms = jnp.mean(o * o, axis=-1, keepdims=True)
            zh = z_ref[bb, pl.ds(t0, c), h * hd:(h + 1) * hd].astype(F32)
            y = o * lax.rsqrt(oms + EPS) * ng_ref[...] * _silu(zh)
            o_ref[bb, pl.ds(t0, c), h * hd:(h + 1) * hd] = y.astype(o_ref.dtype)
        return carry

    lax.fori_loop(0, ts // c, chunk, 0)
    halo_ref[...] = qkv_ref[:, ts - hblk:ts, :]


def _gdn(proj3, gate3, conv_w, alog_b, dtb_b, norm_g, w_out, w_up, w_down, layer, ts=256, nb=GDN_BATCH):
    b, s, _ = proj3.shape
    qkv_w = 3 * GDN_WIDTH
    nt = s // ts
    steps = (b // nb) * nt
    weights = (w_out, w_up, w_down)
    slab = [(w.shape[1] // steps, w.shape[2]) for w in weights]
    assert all(w.shape[1] % steps == 0 and rows % (2 * SUBLANES) == 0 for w, (rows, _) in zip(weights, slab))
    return pl.pallas_call(
        _gdn_kernel,
        grid=(b // nb, nt),
        in_specs=[
            pl.BlockSpec((nb, ts, qkv_w), lambda i, t: (i, t, OFF_GQKV // qkv_w)),
            pl.BlockSpec((nb, ts, GDN_WIDTH), lambda i, t: (i, t, OFF_GZ // GDN_WIDTH)),
            pl.BlockSpec((nb, ts, GATE_PAD), lambda i, t: (i, t, 0)),
            pl.BlockSpec((GDN_CONV_K, qkv_w), lambda i, t: (0, 0)),
            pl.BlockSpec((1, LANES), lambda i, t: (0, 0)),
            pl.BlockSpec((1, LANES), lambda i, t: (0, 0)),
            pl.BlockSpec((1, GDN_HEAD_DIM), lambda i, t: (0, 0)),
        ] + [pl.BlockSpec((None,) + sl, lambda i, t: (layer, i * nt + t, 0)) for sl in slab],
        out_specs=[pl.BlockSpec((nb, ts, GDN_WIDTH), lambda i, t: (i, t, 0))]
        + [pl.BlockSpec(sl, lambda i, t: (i * nt + t, 0)) for sl in slab],
        out_shape=[jax.ShapeDtypeStruct((b, s, GDN_WIDTH), BF16)]
        + [jax.ShapeDtypeStruct(w.shape[1:], BF16) for w in weights],
        scratch_shapes=[pltpu.VMEM((nb, 2 * SUBLANES, qkv_w), BF16),
                        pltpu.VMEM((nb, GDN_HEADS, GDN_HEAD_DIM, GDN_HEAD_DIM), F32),
                        pltpu.VMEM((nb * qkv_w // GDN_HEAD_DIM, 2 * SUBLANES + GDN_CHUNK, GDN_HEAD_DIM), F32)],
        compiler_params=pltpu.CompilerParams(
            dimension_semantics=("parallel", "arbitrary"), vmem_limit_bytes=VMEM_LIMIT),
        name="gdn",
    )(proj3, proj3, gate3, conv_w, alog_b, dtb_b, norm_g, w_out, w_up, w_down)


def _out_proj_kernel(x_ref, yc_ref, ya_ref, yg_ref, w_ref, g_ref, x1_ref, h_ref):
    acc = _dot(yc_ref[...], w_ref[0:CONV_WIDTH, :])
    acc = acc + _dot(ya_ref[...], w_ref[CONV_WIDTH:CONV_WIDTH + ATTN_WIDTH, :])
    acc = acc + _dot(yg_ref[...], w_ref[CONV_WIDTH + ATTN_WIDTH:, :])
    x1 = x_ref[...] + acc
    x1_ref[...] = x1
    ms = jnp.mean(x1 * x1, axis=-1, keepdims=True)
    h_ref[...] = (x1 * lax.rsqrt(ms + EPS) * g_ref[...]).astype(BF16)


def _out_proj(x2, yc, ya, yg, w, g, tm=512):
    m = x2.shape[0]
    return pl.pallas_call(
        _out_proj_kernel,
        grid=(m // tm,),
        in_specs=[
            pl.BlockSpec((tm, D_MODEL), lambda i: (i, 0)),
            pl.BlockSpec((tm, CONV_WIDTH), lambda i: (i, 0)),
            pl.BlockSpec((tm, ATTN_WIDTH), lambda i: (i, 0)),
            pl.BlockSpec((tm, GDN_WIDTH), lambda i: (i, 0)),
            pl.BlockSpec((D_MODEL, D_MODEL), lambda i: (0, 0)),
            pl.BlockSpec((1, D_MODEL), lambda i: (0, 0)),
        ],
        out_specs=[pl.BlockSpec((tm, D_MODEL), lambda i: (i, 0)),
                   pl.BlockSpec((tm, D_MODEL), lambda i: (i, 0))],
        out_shape=[jax.ShapeDtypeStruct((m, D_MODEL), F32),
                   jax.ShapeDtypeStruct((m, D_MODEL), BF16)],
        compiler_params=pltpu.CompilerParams(
            dimension_semantics=("parallel",), vmem_limit_bytes=VMEM_LIMIT),
        name="out_proj",
    )(x2, yc, ya, yg, w, g)


def _mlp_kernel(x1_ref, h_ref, wu_ref, wd_ref, o_ref):
    @pl.when(pl.program_id(1) == 0)
    def _():
        o_ref[...] = x1_ref[...]

    hid = jnp.maximum(_dot(h_ref[...], wu_ref[...]), 0.0)
    hid = (hid * hid).astype(BF16)
    o_ref[...] += _dot(hid, wd_ref[...])


def _mlp(x1, h, w_up, w_down, tm=512, tf=2048):
    m = x1.shape[0]
    return pl.pallas_call(
        _mlp_kernel,
        grid=(m // tm, D_FF // tf),
        in_specs=[
            pl.BlockSpec((tm, D_MODEL), lambda i, f: (i, 0)),
            pl.BlockSpec((tm, D_MODEL), lambda i, f: (i, 0)),
            pl.BlockSpec((D_MODEL, tf), lambda i, f: (0, f)),
            pl.BlockSpec((tf, D_MODEL), lambda i, f: (f, 0)),
        ],
        out_specs=pl.BlockSpec((tm, D_MODEL), lambda i, f: (i, 0)),
        out_shape=jax.ShapeDtypeStruct((m, D_MODEL), F32),
        compiler_params=pltpu.CompilerParams(
            dimension_semantics=("parallel", "arbitrary"), vmem_limit_bytes=VMEM_LIMIT),
        name="mlp",
    )(x1, h, w_up, w_down)


def _gate_w_in(w_in):
    gate = w_in[..., D_IN_MAIN:].astype(BF16)
    return jnp.pad(gate, ((0, 0), (0, 0), (0, GATE_PAD - gate.shape[-1])))


def _decay_lanes(p):
    return jnp.zeros((1, LANES), F32).at[0, GDN_HEADS:2 * GDN_HEADS].set(p)


def _layer(x2, b, s, layer, w_in_all, w_gate_all, w_out, w_up, w_down, norm1_g, conv_w, conv_out_g,
           q_norm_g, k_norm_g, attn_sinks, attn_out_g, gdn_conv_w, gdn_A_log, gdn_dt_bias, gdn_norm_g, norm2_g):
    proj, gates = _in_proj(x2, norm1_g[None, :], w_in_all, w_gate_all, layer)
    proj3 = proj.reshape(b, s, D_IN_MAIN)
    yc = _conv_mixer(proj3, conv_w, conv_out_g[None, :])
    ya = _swa(proj3, jnp.tile(q_norm_g, 2)[None, :], jnp.tile(k_norm_g, ATTN_KV_HEADS)[None, :], attn_sinks,
              attn_out_g[None, :])
    yg, w_out_b, w_up_b, w_down_b = _gdn(proj3, gates.reshape(b, s, GATE_PAD), gdn_conv_w, _decay_lanes(gdn_A_log),
                                         _decay_lanes(gdn_dt_bias), gdn_norm_g[None, :], w_out, w_up, w_down, layer)
    m = b * s
    x1, h2 = _out_proj(x2, yc.reshape(m, CONV_WIDTH), ya.reshape(m, ATTN_WIDTH), yg.reshape(m, GDN_WIDTH),
                       w_out_b, norm2_g[None, :])
    return _mlp(x1, h2, w_up_b, w_down_b)


def kernel(x, norm1_g, w_in, conv_w, conv_out_g, q_norm_g, k_norm_g, attn_sinks, attn_out_g, gdn_conv_w,
           gdn_A_log, gdn_dt_bias, gdn_norm_g, w_out, norm2_g, w_up, w_down):
    b, s, d = x.shape
    x2 = x.reshape(b * s, d)
    w_in_all, w_gate_all = _regroup_w_in(jnp.swapaxes(w_in, 1, 2)), _gate_w_in(w_in)
    for l in range(norm1_g.shape[0]):
        x2 = _layer(x2, b, s, l, w_in_all, w_gate_all, w_out, w_up, w_down, norm1_g[l], conv_w[l],
                    conv_out_g[l], q_norm_g[l], k_norm_g[l], attn_sinks[l], attn_out_g[l], gdn_conv_w[l],
                    gdn_A_log[l], gdn_dt_bias[l], gdn_norm_g[l], norm2_g[l])
    return x2.reshape(b, s, d)
```

```python
import functools

import jax
import jax.numpy as jnp
from jax import lax
from jax.experimental import pallas as pl
from jax.experimental.pallas import tpu as pltpu

F32 = jnp.float32
BF16 = jnp.bfloat16
HIGHEST = lax.Precision.HIGHEST

D_MODEL = 2048
CONV_WIDTH = 512
CONV_GROUPS = 4
CONV_GROUP_DIM = 128
ATTN_HEAD_DIM = 64
ATTN_HEADS = 8
ATTN_KV_HEADS = 2
ATTN_WIDTH = 512
ATTN_KV_WIDTH = 128
WINDOW = 128
ATTN_BLOCK = 128
SWA_BATCH = 2
CONV_BATCH = 2
GDN_HEAD_DIM = 128
GDN_WIDTH = 1024
GDN_HEADS = 8
GDN_CONV_K = 4
GDN_CHUNK = 64
GDN_BATCH = 2
GDN_INV_BLOCK = 16
assert GDN_CHUNK == 4 * GDN_INV_BLOCK
D_FF = 8192
EPS = 1e-6

LANES = 128
SUBLANES = 8
VMEM_LIMIT = 56 * 1024 * 1024

GATE_PAD = LANES
MXU_WIDTH = 256
REGROUP_BLOCK = MXU_WIDTH
D_IN_MAIN = 4 * GDN_WIDTH + 3 * CONV_WIDTH + ATTN_WIDTH + 2 * ATTN_KV_WIDTH
IN_PROJ_TN = 5 * MXU_WIDTH
assert D_IN_MAIN % IN_PROJ_TN == 0
OFF_GQKV = 0
OFF_GZ = 3 * GDN_WIDTH
OFF_CB = 4 * GDN_WIDTH
OFF_CC = OFF_CB + CONV_WIDTH
OFF_CX = OFF_CC + CONV_WIDTH
OFF_AQ = OFF_CX + CONV_WIDTH
OFF_AK = OFF_AQ + ATTN_WIDTH
OFF_AV = OFF_AK + ATTN_KV_WIDTH
assert OFF_AV + ATTN_KV_WIDTH == D_IN_MAIN


def _dot(a, b, precision=None):
    return jnp.dot(a, b, preferred_element_type=F32, precision=precision)


def _dot_nt(a, b, precision=None):
    return lax.dot_general(a, b, (((1,), (1,)), ((), ())), preferred_element_type=F32, precision=precision)


def _dot_tn(a, b, precision=None):
    return lax.dot_general(a, b, (((0,), (0,)), ((), ())), preferred_element_type=F32, precision=precision)


def _sigmoid(x):
    return 0.5 + 0.5 * jnp.tanh(0.5 * x)


def _silu(x):
    hx = 0.5 * x
    return hx + hx * jnp.tanh(hx)


def _softplus(x):
    return jnp.maximum(x, 0.0) + jnp.log(1.0 + jnp.exp(-jnp.abs(x)))


def _regroup_kernel(w_ref, o_ref):
    o_ref[...] = w_ref[...].astype(o_ref.dtype)


def _regroup_src_block(j):
    n_front = (3 * CONV_WIDTH + ATTN_WIDTH + 2 * ATTN_KV_WIDTH) // REGROUP_BLOCK
    n_gdn = 4 * GDN_WIDTH // REGROUP_BLOCK
    return jnp.where(j < n_gdn, j + n_front, j - n_gdn)


def _regroup_w_in(w_in_t, layer):
    blk = REGROUP_BLOCK
    return pl.pallas_call(
        _regroup_kernel,
        grid=(D_IN_MAIN // blk,),
        in_specs=[pl.BlockSpec((None, blk, D_MODEL), lambda j: (layer, _regroup_src_block(j), 0))],
        out_specs=pl.BlockSpec((blk, D_MODEL), lambda j: (j, 0)),
        out_shape=jax.ShapeDtypeStruct((D_IN_MAIN, D_MODEL), BF16),
        compiler_params=pltpu.CompilerParams(
            dimension_semantics=("parallel",), vmem_limit_bytes=VMEM_LIMIT),
        name="regroup_w_in",
    )(w_in_t)


def _in_proj_kernel(x_ref, g_ref, w_ref, wg_ref, o_ref, og_ref):
    x = x_ref[...]
    ms = jnp.mean(x * x, axis=-1, keepdims=True)
    h = (x * lax.rsqrt(ms + EPS) * g_ref[...]).astype(BF16)
    og_ref[...] = _dot(h, wg_ref[...])
    for c0 in range(0, w_ref.shape[0], IN_PROJ_TN):
        o_ref[:, c0:c0 + IN_PROJ_TN] = _dot_nt(h, w_ref[c0:c0 + IN_PROJ_TN, :]).astype(o_ref.dtype)


def _in_proj(x2, g, w, wg_all, layer, tm=512):
    m = x2.shape[0]
    n = w.shape[0]
    return pl.pallas_call(
        _in_proj_kernel,
        grid=(m // tm,),
        in_specs=[
            pl.BlockSpec((tm, D_MODEL), lambda i: (i, 0)),
            pl.BlockSpec((1, D_MODEL), lambda i: (0, 0)),
            pl.BlockSpec((n, D_MODEL), lambda i: (0, 0), pipeline_mode=pl.Buffered(1)),
            pl.BlockSpec((None, D_MODEL, GATE_PAD), lambda i: (layer, 0, 0), pipeline_mode=pl.Buffered(1)),
        ],
        out_specs=[pl.BlockSpec((tm, n), lambda i: (i, 0)),
                   pl.BlockSpec((tm, GATE_PAD), lambda i: (i, 0))],
        out_shape=[jax.ShapeDtypeStruct((m, n), BF16),
                   jax.ShapeDtypeStruct((m, GATE_PAD), F32)],
        compiler_params=pltpu.CompilerParams(
            dimension_semantics=("parallel",), vmem_limit_bytes=VMEM_LIMIT),
        name="in_proj",
    )(x2, g, w, wg_all)


def _conv_kernel(cb_ref, cc_ref, cx_ref, w_ref, g_ref, o_ref, zp_ref):
    s_len = cc_ref.shape[1]
    w = w_ref[...]
    for bb in range(cc_ref.shape[0]):
        zp_ref[bb, 0:SUBLANES, :] = jnp.zeros((SUBLANES, zp_ref.shape[2]), F32)
        zp_ref[bb, SUBLANES:, :] = cc_ref[bb].astype(F32) * cx_ref[bb].astype(F32)
        y = (w[0:1] * zp_ref[bb, SUBLANES - 2:SUBLANES - 2 + s_len, :]
             + w[1:2] * zp_ref[bb, SUBLANES - 1:SUBLANES - 1 + s_len, :] + w[2:3] * zp_ref[bb, SUBLANES:, :])
        y = cb_ref[bb].astype(F32) * y
        ms = jnp.mean(y * y, axis=-1, keepdims=True)
        o_ref[bb] = (y * lax.rsqrt(ms + EPS) * g_ref[...]).astype(o_ref.dtype)


def _conv_mixer(proj3, conv_w, conv_out_g, nb=CONV_BATCH):
    b, s, _ = proj3.shape
    gd = CONV_GROUP_DIM

    def col(off):
        return lambda i, g: (i, 0, off // gd + g)

    return pl.pallas_call(
        _conv_kernel,
        grid=(b // nb, CONV_GROUPS),
        in_specs=[
            pl.BlockSpec((nb, s, gd), col(OFF_CB)),
            pl.BlockSpec((nb, s, gd), col(OFF_CC)),
            pl.BlockSpec((nb, s, gd), col(OFF_CX)),
            pl.BlockSpec((3, gd), lambda i, g: (0, g)),
            pl.BlockSpec((1, gd), lambda i, g: (0, g)),
        ],
        out_specs=pl.BlockSpec((nb, s, gd), lambda i, g: (i, 0, g)),
        out_shape=jax.ShapeDtypeStruct((b, s, CONV_WIDTH), BF16),
        scratch_shapes=[pltpu.VMEM((nb, SUBLANES + s, gd), F32)],
        compiler_params=pltpu.CompilerParams(
            dimension_semantics=("parallel", "parallel"), vmem_limit_bytes=VMEM_LIMIT),
        name="conv_mixer",
    )(proj3, proj3, proj3, conv_w, conv_out_g)


def _swa_kernel(q_ref, k_ref, v_ref, qg_ref, kg_ref, sink_ref, og_ref, o_ref, kk_ref, vv_ref):
    nb, s_len = q_ref.shape[0], q_ref.shape[1]
    blk = ATTN_BLOCK
    hd = ATTN_HEAD_DIM
    nh = ATTN_HEADS
    group = nh // ATTN_KV_HEADS
    pair_w = 2 * hd
    units = [(bb, h) for bb in range(nb) for h in range(nh)]
    nu = range(len(units))

    lane = lax.broadcasted_iota(jnp.int32, (1, pair_w), 1)
    lo = lane < hd
    half_mask = (lo.astype(F32), 1.0 - lo.astype(F32))
    avg = jnp.where(lax.broadcasted_iota(jnp.int32, (pair_w, pair_w), 0) // hd
                    == lax.broadcasted_iota(jnp.int32, (pair_w, pair_w), 1) // hd, 1.0 / hd, 0.0).astype(BF16)

    def half_mean_sq(x):
        x2 = x * x
        hi = x2.astype(BF16)
        rem = (x2 - hi.astype(F32)).astype(BF16)
        return _dot(hi, avg) + _dot(rem, avg)

    zeros = jnp.zeros((blk, pair_w), BF16)
    for bb in range(nb):
        k = k_ref[bb].astype(F32)
        kn = k * lax.rsqrt(half_mean_sq(k) + EPS) * kg_ref[...]
        ksw = pltpu.roll(kn, hd, axis=1)
        v = v_ref[bb].astype(F32)
        vsw = pltpu.roll(v, hd, axis=1)
        for j in range(ATTN_KV_HEADS):
            kk_ref[bb, j, 0:blk, :] = zeros
            vv_ref[bb, j, 0:blk, :] = zeros
        kk_ref[bb, 0, blk:, :] = jnp.where(lo, kn, ksw).astype(BF16)
        kk_ref[bb, 1, blk:, :] = jnp.where(lo, ksw, kn).astype(BF16)
        vv_ref[bb, 0, blk:, :] = jnp.where(lo, v, vsw).astype(BF16)
        vv_ref[bb, 1, blk:, :] = jnp.where(lo, vsw, v).astype(BF16)

    qi = lax.broadcasted_iota(jnp.int32, (blk, 2 * blk), 0)
    si = lax.broadcasted_iota(jnp.int32, (blk, 2 * blk), 1)
    rel = qi + blk - si
    band = (rel >= 0) & (rel < WINDOW)
    cur = si >= blk
    q_gain = [qg_ref[...] * (hd ** -0.5) * half_mask[i] for i in range(2)]

    def body(n, carry):
        r0 = pl.multiple_of(n * blk, blk)
        valid = band & (cur | (n > 0))
        qm = []
        for bb in range(nb):
            for p in range(nh // 2):
                qp = q_ref[bb, pl.ds(r0, blk), p * pair_w:(p + 1) * pair_w].astype(F32)
                qn = qp * lax.rsqrt(half_mean_sq(qp) + EPS)
                qm += [(qn * q_gain[0]).astype(BF16), (qn * q_gain[1]).astype(BF16)]
        s = [jnp.where(valid, _dot_nt(qm[u], kk_ref[bb, h // group, pl.ds(r0, 2 * blk), :]), -jnp.inf)
             for u, (bb, h) in enumerate(units)]
        m = [jnp.maximum(jnp.max(s[u], axis=-1, keepdims=True), sink_ref[h]) for u, (bb, h) in enumerate(units)]
        p_ = [jnp.exp(s[u] - m[u]) for u in nu]
        inv = [1.0 / (jnp.sum(p_[u], axis=-1, keepdims=True) + jnp.exp(sink_ref[h] - m[u]))
               for u, (bb, h) in enumerate(units)]
        o = [_dot(p_[u].astype(BF16), vv_ref[bb, h // group, pl.ds(r0, 2 * blk), :]) * inv[u]
             for u, (bb, h) in enumerate(units)]
        for bb in range(nb):
            outs = []
            for p in range(nh // 2):
                u = bb * nh + 2 * p
                op = jnp.where(lo, o[u], o[u + 1])
                outs.append(op * lax.rsqrt(half_mean_sq(op) + EPS) * og_ref[:, p * pair_w:(p + 1) * pair_w])
            o_ref[bb, pl.ds(r0, blk), :] = jnp.concatenate(outs, axis=-1).astype(o_ref.dtype)
        return carry

    lax.fori_loop(0, s_len // blk, body, 0)


def _swa(proj3, q_g, k_g2, sinks, out_g, nb=SWA_BATCH):
    b, s, _ = proj3.shape
    return pl.pallas_call(
        _swa_kernel,
        grid=(b // nb,),
        in_specs=[
            pl.BlockSpec((nb, s, ATTN_WIDTH), lambda i: (i, 0, OFF_AQ // ATTN_WIDTH)),
            pl.BlockSpec((nb, s, ATTN_KV_WIDTH), lambda i: (i, 0, OFF_AK // ATTN_KV_WIDTH)),
            pl.BlockSpec((nb, s, ATTN_KV_WIDTH), lambda i: (i, 0, OFF_AV // ATTN_KV_WIDTH)),
            pl.BlockSpec((1, 2 * ATTN_HEAD_DIM), lambda i: (0, 0)),
            pl.BlockSpec((1, ATTN_KV_WIDTH), lambda i: (0, 0)),
            pl.BlockSpec(memory_space=pltpu.SMEM),
            pl.BlockSpec((1, ATTN_WIDTH), lambda i: (0, 0)),
        ],
        out_specs=pl.BlockSpec((nb, s, ATTN_WIDTH), lambda i: (i, 0, 0)),
        out_shape=jax.ShapeDtypeStruct((b, s, ATTN_WIDTH), BF16),
        scratch_shapes=[pltpu.VMEM((nb, ATTN_KV_HEADS, s + ATTN_BLOCK, 2 * ATTN_HEAD_DIM), BF16),
                        pltpu.VMEM((nb, ATTN_KV_HEADS, s + ATTN_BLOCK, 2 * ATTN_HEAD_DIM), BF16)],
        compiler_params=pltpu.CompilerParams(
            dimension_semantics=("parallel",), vmem_limit_bytes=VMEM_LIMIT),
        name="swa",
    )(proj3, proj3, proj3, q_g, k_g2, sinks, out_g)


def _gdn_kernel(n_cast, qkv_ref, z_ref, gate_ref, cw_ref, alog_ref, dtb_ref, ng_ref, *refs):
    cast_src = refs[:n_cast]
    o_ref = refs[n_cast]
    cast_dst = refs[n_cast + 1:2 * n_cast + 1]
    halo_ref, state_ref, win_ref = refs[2 * n_cast + 1:]
    nb, ts = qkv_ref.shape[0], qkv_ref.shape[1]
    c = GDN_CHUNK
    hd = GDN_HEAD_DIM
    nh = GDN_HEADS
    hblk = 2 * SUBLANES
    units = [(bb, h) for bb in range(nb) for h in range(nh)]
    nu = range(len(units))

    @pl.when(pl.program_id(1) == 0)
    def _():
        state_ref[...] = jnp.zeros(state_ref.shape, F32)
        halo_ref[...] = jnp.zeros(halo_ref.shape, halo_ref.dtype)

    row = lax.broadcasted_iota(jnp.int32, (c, c), 0)
    col = lax.broadcasted_iota(jnp.int32, (c, c), 1)
    tril = row >= col
    strict = row > col
    blk16 = (row // GDN_INV_BLOCK) == (col // GDN_INV_BLOCK)
    blk32 = (row // (2 * GDN_INV_BLOCK)) == (col // (2 * GDN_INV_BLOCK))
    tril_f = tril.astype(F32)
    sel = (lax.broadcasted_iota(jnp.int32, (2 * nh, LANES), 0)
           == lax.broadcasted_iota(jnp.int32, (2 * nh, LANES), 1)).astype(F32)
    is_beta = lax.broadcasted_iota(jnp.int32, (1, LANES), 1) < nh
    neg_a = -jnp.exp(alog_ref[...])
    dtb = dtb_ref[...]

    def chunk(ci, carry):
        t0 = pl.multiple_of(ci * c, c)
        tp = pl.multiple_of(jnp.maximum(t0 - hblk, 0), hblk)
        first = ci == 0

        for src, dst in zip(cast_src, cast_dst):
            rows = src.shape[0] // (ts // c)
            r0 = pl.multiple_of(ci * rows, rows)
            dst[pl.ds(r0, rows), :] = src[pl.ds(r0, rows), :].astype(dst.dtype)

        def conv_silu(bb, lo):
            w = cw_ref[:, lo:lo + hd]
            slot = (bb * qkv_ref.shape[2] + lo) // hd
            prev = jnp.where(first, halo_ref[bb, :, lo:lo + hd], qkv_ref[bb, pl.ds(tp, hblk), lo:lo + hd])
            win_ref[slot, 0:hblk, :] = prev.astype(F32)
            win_ref[slot, hblk:, :] = qkv_ref[bb, pl.ds(t0, c), lo:lo + hd].astype(F32)
            y = w[GDN_CONV_K - 1:GDN_CONV_K] * win_ref[slot, hblk:, :]
            for k in range(1, GDN_CONV_K):
                y = y + w[GDN_CONV_K - 1 - k:GDN_CONV_K - k] * win_ref[slot, hblk - k:hblk - k + c, :]
            return _silu(y)

        gmix, gc_all, rows = [], [], []
        for bb in range(nb):
            gl = gate_ref[bb, pl.ds(t0, c), :]
            gm = jnp.where(is_beta, _sigmoid(gl), neg_a * _softplus(gl + dtb))
            ga = _dot(tril_f, gm, HIGHEST)
            gmix.append(gm)
            gc_all.append(ga)
            rows.append(_dot_nt(sel, jnp.concatenate([gm, ga], axis=0), HIGHEST))

        qs, ks, vs = [], [], []
        for bb, h in units:
            q = conv_silu(bb, h * hd)
            k = conv_silu(bb, GDN_WIDTH + h * hd)
            vs.append(conv_silu(bb, 2 * GDN_WIDTH + h * hd))
            qs.append(q * (lax.rsqrt(jnp.sum(q * q, axis=-1, keepdims=True) + EPS) * (hd ** -0.5)))
            ks.append(k * lax.rsqrt(jnp.sum(k * k, axis=-1, keepdims=True) + EPS))

        kkqk = []
        for u in nu:
            kb = ks[u].astype(BF16)
            kkqk.append(_dot_nt(jnp.concatenate([kb, qs[u].astype(BF16)], axis=0), kb))
        beta = [gmix[bb][:, h:h + 1] for bb, h in units]
        gc = [gc_all[bb][:, nh + h:nh + h + 1] for bb, h in units]
        g_last = [gc_all[bb][c - 1:c, nh + h:nh + h + 1] for bb, h in units]
        beta_row = [rows[bb][h:h + 1, :c] for bb, h in units]
        es, ps, qkb, off_diag = [], [], [], []
        for u, (bb, h) in enumerate(units):
            gc_row = rows[bb][nh + h:nh + h + 1, c:]
            decay = jnp.exp(jnp.where(tril, gc[u] - gc_row, 0.0))
            a = jnp.where(strict, kkqk[u][:c] * beta[u] * decay, 0.0)
            qkb.append(jnp.where(tril, kkqk[u][c:] * decay, 0.0).astype(BF16))
            a_d = jnp.where(blk16, a, 0.0)
            es.append(-a_d)
            ps.append(a_d)
            off_diag.append((jnp.where(blk32, a - a_d, 0.0), jnp.where(blk32, 0.0, a)))
        ps = [_dot(ps[u].astype(BF16), ps[u].astype(BF16)) for u in nu]
        for i in range(3):
            if i < 2:
                ep = [_dot(jnp.concatenate([es[u], ps[u]], axis=0).astype(BF16), ps[u].astype(BF16)) for u in nu]
                es = [es[u] + ps[u] + ep[u][:c] for u in nu]
                ps = [ep[u][c:] for u in nu]
            else:
                ep = [_dot(es[u].astype(BF16), ps[u].astype(BF16)) for u in nu]
                es = [es[u] + ps[u] + ep[u] for u in nu]
        for level in range(2):
            ls = [off_diag[u][level] for u in nu]
            g1 = [ls[u] + _dot(es[u].astype(BF16), ls[u].astype(BF16)) for u in nu]
            es = [es[u] - (g1[u] + _dot(g1[u].astype(BF16), es[u].astype(BF16))) for u in nu]
        egc = [jnp.exp(gc[u]) for u in nu]
        wu = []
        for u in nu:
            kv = jnp.concatenate([ks[u] * egc[u], vs[u]], axis=-1)
            e_b = (es[u] * beta_row[u]).astype(BF16)
            wu.append(kv * beta[u] + _dot(e_b, kv.astype(BF16)))
        st = [state_ref[bb, h] for bb, h in units]
        ws_qs = [_dot(jnp.concatenate([wu[u][:, :hd], qs[u] * egc[u]], axis=0).astype(BF16), st[u].astype(BF16))
                 for u in nu]
        vb = [(wu[u][:, hd:] - ws_qs[u][:c]).astype(BF16) for u in nu]
        kd = [(ks[u] * jnp.exp(g_last[u] - gc[u])).astype(BF16) for u in nu]
        os_ = [ws_qs[u][c:] + _dot(qkb[u], vb[u]) for u in nu]
        for u, (bb, h) in enumerate(units):
            state_ref[bb, h] = st[u] * jnp.exp(g_last[u]) + _dot_tn(kd[u], vb[u])
        for u, (bb, h) in enumerate(units):
            o = os_[u]
            oms = jnp.mean(o * o, axis=-1, keepdims=True)
            zh = z_ref[bb, pl.ds(t0, c), h * hd:(h + 1) * hd].astype(F32)
            y = o * lax.rsqrt(oms + EPS) * ng_ref[...] * _silu(zh)
            o_ref[bb, pl.ds(t0, c), h * hd:(h + 1) * hd] = y.astype(o_ref.dtype)
        return carry

    lax.fori_loop(0, ts // c, chunk, 0)
    halo_ref[...] = qkv_ref[:, ts - hblk:ts, :]


def _gdn(proj3, gate3, conv_w, alog_b, dtb_b, norm_g, w_out, w_up, w_down, w_in_t, layer, ts=256, nb=GDN_BATCH):
    b, s, _ = proj3.shape
    qkv_w = 3 * GDN_WIDTH
    nt = s // ts
    steps = (b // nb) * nt
    chunks = ts // GDN_CHUNK
    weights = (w_out, w_up, w_down)
    slab = [(w.shape[1] // steps, w.shape[2]) for w in weights]
    assert all(w.shape[1] % steps == 0 and rows % (2 * SUBLANES * chunks) == 0
               for w, (rows, _) in zip(weights, slab))
    cast_in = [pl.BlockSpec((None,) + sl, lambda i, t: (layer, i * nt + t, 0)) for sl in slab]
    cast_out = [pl.BlockSpec(sl, lambda i, t: (i * nt + t, 0)) for sl in slab]
    cast_shape = [jax.ShapeDtypeStruct(w.shape[1:], BF16) for w in weights]
    cast_args = list(weights)
    if layer + 1 < w_in_t.shape[0]:
        n_blk = D_IN_MAIN // REGROUP_BLOCK
        assert n_blk <= steps

        def dst_block(i, t):
            return jnp.minimum(i * nt + t, n_blk - 1)

        cast_in.append(pl.BlockSpec((None, REGROUP_BLOCK, D_MODEL),
                                    lambda i, t: (layer + 1, _regroup_src_block(dst_block(i, t)), 0)))
        cast_out.append(pl.BlockSpec((REGROUP_BLOCK, D_MODEL), lambda i, t: (dst_block(i, t), 0)))
        cast_shape.append(jax.ShapeDtypeStruct((D_IN_MAIN, D_MODEL), BF16))
        cast_args.append(w_in_t)
    return pl.pallas_call(
        functools.partial(_gdn_kernel, len(cast_args)),
        grid=(b // nb, nt),
        in_specs=[
            pl.BlockSpec((nb, ts, qkv_w), lambda i, t: (i, t, OFF_GQKV // qkv_w)),
            pl.BlockSpec((nb, ts, GDN_WIDTH), lambda i, t: (i, t, OFF_GZ // GDN_WIDTH)),
            pl.BlockSpec((nb, ts, GATE_PAD), lambda i, t: (i, t, 0)),
            pl.BlockSpec((GDN_CONV_K, qkv_w), lambda i, t: (0, 0)),
            pl.BlockSpec((1, LANES), lambda i, t: (0, 0)),
            pl.BlockSpec((1, LANES), lambda i, t: (0, 0)),
            pl.BlockSpec((1, GDN_HEAD_DIM), lambda i, t: (0, 0)),
        ] + cast_in,
        out_specs=[pl.BlockSpec((nb, ts, GDN_WIDTH), lambda i, t: (i, t, 0))] + cast_out,
        out_shape=[jax.ShapeDtypeStruct((b, s, GDN_WIDTH), BF16)] + cast_shape,
        scratch_shapes=[pltpu.VMEM((nb, 2 * SUBLANES, qkv_w), BF16),
                        pltpu.VMEM((nb, GDN_HEADS, GDN_HEAD_DIM, GDN_HEAD_DIM), F32),
                        pltpu.VMEM((nb * qkv_w // GDN_HEAD_DIM, 2 * SUBLANES + GDN_CHUNK, GDN_HEAD_DIM), F32)],
        compiler_params=pltpu.CompilerParams(
            dimension_semantics=("parallel", "arbitrary"), vmem_limit_bytes=VMEM_LIMIT),
        name="gdn",
    )(proj3, proj3, gate3, conv_w, alog_b, dtb_b, norm_g, *cast_args)


def _out_proj_kernel(x_ref, yc_ref, ya_ref, yg_ref, w_ref, g_ref, x1_ref, h_ref):
    acc = _dot(yc_ref[...], w_ref[0:CONV_WIDTH, :])
    acc = acc + _dot(ya_ref[...], w_ref[CONV_WIDTH:CONV_WIDTH + ATTN_WIDTH, :])
    acc = acc + _dot(yg_ref[...], w_ref[CONV_WIDTH + ATTN_WIDTH:, :])
    x1 = x_ref[...] + acc
    x1_ref[...] = x1
    ms = jnp.mean(x1 * x1, axis=-1, keepdims=True)
    h_ref[...] = (x1 * lax.rsqrt(ms + EPS) * g_ref[...]).astype(BF16)


def _out_proj(x2, yc, ya, yg, w, g, tm=512):
    m = x2.shape[0]
    return pl.pallas_call(
        _out_proj_kernel,
        grid=(m // tm,),
        in_specs=[
            pl.BlockSpec((tm, D_MODEL), lambda i: (i, 0)),
            pl.BlockSpec((tm, CONV_WIDTH), lambda i: (i, 0)),
            pl.BlockSpec((tm, ATTN_WIDTH), lambda i: (i, 0)),
            pl.BlockSpec((tm, GDN_WIDTH), lambda i: (i, 0)),
            pl.BlockSpec((D_MODEL, D_MODEL), lambda i: (0, 0)),
            pl.BlockSpec((1, D_MODEL), lambda i: (0, 0)),
        ],
        out_specs=[pl.BlockSpec((tm, D_MODEL), lambda i: (i, 0)),
                   pl.BlockSpec((tm, D_MODEL), lambda i: (i, 0))],
        out_shape=[jax.ShapeDtypeStruct((m, D_MODEL), F32),
                   jax.ShapeDtypeStruct((m, D_MODEL), BF16)],
        compiler_params=pltpu.CompilerParams(
            dimension_semantics=("parallel",), vmem_limit_bytes=VMEM_LIMIT),
        name="out_proj",
    )(x2, yc, ya, yg, w, g)


def _mlp_kernel(x1_ref, h_ref, wu_ref, wd_ref, o_ref):
    @pl.when(pl.program_id(1) == 0)
    def _():
        o_ref[...] = x1_ref[...]

    hid = jnp.maximum(_dot(h_ref[...], wu_ref[...]), 0.0)
    hid = (hid * hid).astype(BF16)
    o_ref[...] += _dot(hid, wd_ref[...])


def _mlp(x1, h, w_up, w_down, tm=512, tf=2048):
    m = x1.shape[0]
    return pl.pallas_call(
        _mlp_kernel,
        grid=(m // tm, D_FF // tf),
        in_specs=[
            pl.BlockSpec((tm, D_MODEL), lambda i, f: (i, 0)),
            pl.BlockSpec((tm, D_MODEL), lambda i, f: (i, 0)),
            pl.BlockSpec((D_MODEL, tf), lambda i, f: (0, f)),
            pl.BlockSpec((tf, D_MODEL), lambda i, f: (f, 0)),
        ],
        out_specs=pl.BlockSpec((tm, D_MODEL), lambda i, f: (i, 0)),
        out_shape=jax.ShapeDtypeStruct((m, D_MODEL), F32),
        compiler_params=pltpu.CompilerParams(
            dimension_semantics=("parallel", "arbitrary"), vmem_limit_bytes=VMEM_LIMIT),
        name="mlp",
    )(x1, h, w_up, w_down)


def _gate_w_in(w_in):
    gate = w_in[..., D_IN_MAIN:].astype(BF16)
    return jnp.pad(gate, ((0, 0), (0, 0), (0, GATE_PAD - gate.shape[-1])))


def _decay_lanes(p):
    return jnp.zeros((1, LANES), F32).at[0, GDN_HEADS:2 * GDN_HEADS].set(p)


def _layer(x2, b, s, layer, w_in_b, w_in_t, w_gate_all, w_out, w_up, w_down, norm1_g, conv_w, conv_out_g,
           q_norm_g, k_norm_g, attn_sinks, attn_out_g, gdn_conv_w, gdn_A_log, gdn_dt_bias, gdn_norm_g, norm2_g):
    proj, gates = _in_proj(x2, norm1_g[None, :], w_in_b, w_gate_all, layer)
    proj3 = proj.reshape(b, s, D_IN_MAIN)
    yc = _conv_mixer(proj3, conv_w, conv_out_g[None, :])
    ya = _swa(proj3, jnp.tile(q_norm_g, 2)[None, :], jnp.tile(k_norm_g, ATTN_KV_HEADS)[None, :], attn_sinks,
              attn_out_g[None, :])
    yg, w_out_b, w_up_b, w_down_b, *w_in_next = _gdn(
        proj3, gates.reshape(b, s, GATE_PAD), gdn_conv_w, _decay_lanes(gdn_A_log), _decay_lanes(gdn_dt_bias),
        gdn_norm_g[None, :], w_out, w_up, w_down, w_in_t, layer)
    m = b * s
    x1, h2 = _out_proj(x2, yc.reshape(m, CONV_WIDTH), ya.reshape(m, ATTN_WIDTH), yg.reshape(m, GDN_WIDTH),
                       w_out_b, norm2_g[None, :])
    return _mlp(x1, h2, w_up_b, w_down_b), (w_in_next[0] if w_in_next else None)


def kernel(x, norm1_g, w_in, conv_w, conv_out_g, q_norm_g, k_norm_g, attn_sinks, attn_out_g, gdn_conv_w,
           gdn_A_log, gdn_dt_bias, gdn_norm_g, w_out, norm2_g, w_up, w_down):
    b, s, d = x.shape
    x2 = x.reshape(b * s, d)
    w_in_t, w_gate_all = jnp.swapaxes(w_in, 1, 2), _gate_w_in(w_in)
    w_in_b = _regroup_w_in(w_in_t, 0)
    for l in range(norm1_g.shape[0]):
        x2, w_in_b = _layer(x2, b, s, l, w_in_b, w_in_t, w_gate_all, w_out, w_up, w_down, norm1_g[l], conv_w[l],
                            conv_out_g[l], q_norm_g[l], k_norm_g[l], attn_sinks[l], attn_out_g[l], gdn_conv_w[l],
                            gdn_A_log[l], gdn_dt_bias[l], gdn_norm_g[l], norm2_g[l])
    return x2.reshape(b, s, d)
```

```python
import functools

import jax
import jax.numpy as jnp
from jax import lax
from jax.experimental import pallas as pl
from jax.experimental.pallas import tpu as pltpu

F32 = jnp.float32
BF16 = jnp.bfloat16
HIGHEST = lax.Precision.HIGHEST

D_MODEL = 2048
CONV_WIDTH = 512
CONV_GROUPS = 4
CONV_GROUP_DIM = 128
ATTN_HEAD_DIM = 64
ATTN_HEADS = 8
ATTN_KV_HEADS = 2
ATTN_WIDTH = 512
ATTN_KV_WIDTH = 128
WINDOW = 128
ATTN_BLOCK = 128
SWA_BATCH = 2
CONV_BATCH = 2
GDN_HEAD_DIM = 128
GDN_WIDTH = 1024
GDN_HEADS = 8
GDN_CONV_K = 4
GDN_CHUNK = 64
GDN_BATCH = 2
GDN_INV_BLOCK = 16
assert GDN_CHUNK == 4 * GDN_INV_BLOCK
D_FF = 8192
EPS = 1e-6
LOG2E = 1.4426950408889634

LANES = 128
SUBLANES = 8
VMEM_LIMIT = 56 * 1024 * 1024

GATE_PAD = LANES
MXU_WIDTH = 256
REGROUP_BLOCK = MXU_WIDTH
D_IN_MAIN = 4 * GDN_WIDTH + 3 * CONV_WIDTH + ATTN_WIDTH + 2 * ATTN_KV_WIDTH
IN_PROJ_TN = 5 * MXU_WIDTH
assert D_IN_MAIN % IN_PROJ_TN == 0
OFF_GQKV = 0
OFF_GZ = 3 * GDN_WIDTH
OFF_CB = 4 * GDN_WIDTH
OFF_CC = OFF_CB + CONV_WIDTH
OFF_CX = OFF_CC + CONV_WIDTH
OFF_AQ = OFF_CX + CONV_WIDTH
OFF_AK = OFF_AQ + ATTN_WIDTH
OFF_AV = OFF_AK + ATTN_KV_WIDTH
assert OFF_AV + ATTN_KV_WIDTH == D_IN_MAIN


def _dot(a, b, precision=None):
    return jnp.dot(a, b, preferred_element_type=F32, precision=precision)


def _dot_nt(a, b, precision=None):
    return lax.dot_general(a, b, (((1,), (1,)), ((), ())), preferred_element_type=F32, precision=precision)


def _dot_tn(a, b, precision=None):
    return lax.dot_general(a, b, (((0,), (0,)), ((), ())), preferred_element_type=F32, precision=precision)


def _sigmoid(x):
    return 0.5 + 0.5 * jnp.tanh(0.5 * x)


def _silu(x):
    hx = 0.5 * x
    return hx + hx * jnp.tanh(hx)


def _softplus(x):
    return jnp.maximum(x, 0.0) + jnp.log(1.0 + jnp.exp(-jnp.abs(x)))


def _regroup_kernel(w_ref, o_ref):
    o_ref[...] = w_ref[...].astype(o_ref.dtype)


def _regroup_src_block(j):
    n_front = (3 * CONV_WIDTH + ATTN_WIDTH + 2 * ATTN_KV_WIDTH) // REGROUP_BLOCK
    n_gdn = 4 * GDN_WIDTH // REGROUP_BLOCK
    return jnp.where(j < n_gdn, j + n_front, j - n_gdn)


def _regroup_w_in(w_in_t, layer):
    blk = REGROUP_BLOCK
    return pl.pallas_call(
        _regroup_kernel,
        grid=(D_IN_MAIN // blk,),
        in_specs=[pl.BlockSpec((None, blk, D_MODEL), lambda j: (layer, _regroup_src_block(j), 0))],
        out_specs=pl.BlockSpec((blk, D_MODEL), lambda j: (j, 0)),
        out_shape=jax.ShapeDtypeStruct((D_IN_MAIN, D_MODEL), BF16),
        compiler_params=pltpu.CompilerParams(
            dimension_semantics=("parallel",), vmem_limit_bytes=VMEM_LIMIT),
        name="regroup_w_in",
    )(w_in_t)


def _in_proj_kernel(x_ref, g_ref, w_ref, wg_ref, o_ref, og_ref):
    x = x_ref[...]
    ms = jnp.mean(x * x, axis=-1, keepdims=True)
    h = (x * lax.rsqrt(ms + EPS) * g_ref[...]).astype(BF16)
    og_ref[...] = _dot(h, wg_ref[...])
    for c0 in range(0, w_ref.shape[0], IN_PROJ_TN):
        o_ref[:, c0:c0 + IN_PROJ_TN] = _dot_nt(h, w_ref[c0:c0 + IN_PROJ_TN, :]).astype(o_ref.dtype)


def _in_proj(x2, g, w, wg_all, layer, tm=512):
    m = x2.shape[0]
    n = w.shape[0]
    return pl.pallas_call(
        _in_proj_kernel,
        grid=(m // tm,),
        in_specs=[
            pl.BlockSpec((tm, D_MODEL), lambda i: (i, 0)),
            pl.BlockSpec((1, D_MODEL), lambda i: (0, 0)),
            pl.BlockSpec((n, D_MODEL), lambda i: (0, 0), pipeline_mode=pl.Buffered(1)),
            pl.BlockSpec((None, D_MODEL, GATE_PAD), lambda i: (layer, 0, 0), pipeline_mode=pl.Buffered(1)),
        ],
        out_specs=[pl.BlockSpec((tm, n), lambda i: (i, 0)),
                   pl.BlockSpec((tm, GATE_PAD), lambda i: (i, 0))],
        out_shape=[jax.ShapeDtypeStruct((m, n), BF16),
                   jax.ShapeDtypeStruct((m, GATE_PAD), F32)],
        compiler_params=pltpu.CompilerParams(
            dimension_semantics=("parallel",), vmem_limit_bytes=VMEM_LIMIT),
        name="in_proj",
    )(x2, g, w, wg_all)


def _conv_kernel(cb_ref, cc_ref, cx_ref, w_ref, g_ref, o_ref, zp_ref):
    s_len = cc_ref.shape[1]
    w = w_ref[...]
    for bb in range(cc_ref.shape[0]):
        zp_ref[bb, 0:SUBLANES, :] = jnp.zeros((SUBLANES, zp_ref.shape[2]), F32)
        zp_ref[bb, SUBLANES:, :] = cc_ref[bb].astype(F32) * cx_ref[bb].astype(F32)
        y = (w[0:1] * zp_ref[bb, SUBLANES - 2:SUBLANES - 2 + s_len, :]
             + w[1:2] * zp_ref[bb, SUBLANES - 1:SUBLANES - 1 + s_len, :] + w[2:3] * zp_ref[bb, SUBLANES:, :])
        y = cb_ref[bb].astype(F32) * y
        ms = jnp.mean(y * y, axis=-1, keepdims=True)
        o_ref[bb] = (y * lax.rsqrt(ms + EPS) * g_ref[...]).astype(o_ref.dtype)


def _conv_mixer(proj3, conv_w, conv_out_g, nb=CONV_BATCH):
    b, s, _ = proj3.shape
    gd = CONV_GROUP_DIM

    def col(off):
        return lambda i, g: (i, 0, off // gd + g)

    return pl.pallas_call(
        _conv_kernel,
        grid=(b // nb, CONV_GROUPS),
        in_specs=[
            pl.BlockSpec((nb, s, gd), col(OFF_CB)),
            pl.BlockSpec((nb, s, gd), col(OFF_CC)),
            pl.BlockSpec((nb, s, gd), col(OFF_CX)),
            pl.BlockSpec((3, gd), lambda i, g: (0, g)),
            pl.BlockSpec((1, gd), lambda i, g: (0, g)),
        ],
        out_specs=pl.BlockSpec((nb, s, gd), lambda i, g: (i, 0, g)),
        out_shape=jax.ShapeDtypeStruct((b, s, CONV_WIDTH), BF16),
        scratch_shapes=[pltpu.VMEM((nb, SUBLANES + s, gd), F32)],
        compiler_params=pltpu.CompilerParams(
            dimension_semantics=("parallel", "parallel"), vmem_limit_bytes=VMEM_LIMIT),
        name="conv_mixer",
    )(proj3, proj3, proj3, conv_w, conv_out_g)


def _swa_kernel(q_ref, k_ref, v_ref, qg_ref, kg_ref, sink_ref, og_ref, o_ref, kk_ref, vv_ref):
    nb, s_len = q_ref.shape[0], q_ref.shape[1]
    blk = ATTN_BLOCK
    hd = ATTN_HEAD_DIM
    nh = ATTN_HEADS
    group = nh // ATTN_KV_HEADS
    pair_w = 2 * hd
    units = [(bb, h) for bb in range(nb) for h in range(nh)]
    nu = range(len(units))

    lane = lax.broadcasted_iota(jnp.int32, (1, pair_w), 1)
    lo = lane < hd
    half_mask = (lo.astype(F32), 1.0 - lo.astype(F32))
    avg = jnp.where(lax.broadcasted_iota(jnp.int32, (pair_w, pair_w), 0) // hd
                    == lax.broadcasted_iota(jnp.int32, (pair_w, pair_w), 1) // hd, 1.0 / hd, 0.0).astype(BF16)

    def half_mean_sq(x):
        x2 = x * x
        hi = x2.astype(BF16)
        rem = (x2 - hi.astype(F32)).astype(BF16)
        return _dot(hi, avg) + _dot(rem, avg)

    zeros = jnp.zeros((blk, pair_w), BF16)
    for bb in range(nb):
        k = k_ref[bb].astype(F32)
        kn = k * lax.rsqrt(half_mean_sq(k) + EPS) * kg_ref[...]
        ksw = pltpu.roll(kn, hd, axis=1)
        v = v_ref[bb].astype(F32)
        vsw = pltpu.roll(v, hd, axis=1)
        for j in range(ATTN_KV_HEADS):
            kk_ref[bb, j, 0:blk, :] = zeros
            vv_ref[bb, j, 0:blk, :] = zeros
        kk_ref[bb, 0, blk:, :] = jnp.where(lo, kn, ksw).astype(BF16)
        kk_ref[bb, 1, blk:, :] = jnp.where(lo, ksw, kn).astype(BF16)
        vv_ref[bb, 0, blk:, :] = jnp.where(lo, v, vsw).astype(BF16)
        vv_ref[bb, 1, blk:, :] = jnp.where(lo, vsw, v).astype(BF16)

    qi = lax.broadcasted_iota(jnp.int32, (blk, 2 * blk), 0)
    si = lax.broadcasted_iota(jnp.int32, (blk, 2 * blk), 1)
    rel = qi + blk - si
    band = (rel >= 0) & (rel < WINDOW)
    cur = si >= blk
    q_gain = [qg_ref[...] * (hd ** -0.5 * LOG2E) * half_mask[i] for i in range(2)]

    def body(n, carry):
        r0 = pl.multiple_of(n * blk, blk)
        valid = band & (cur | (n > 0))
        qm = []
        for bb in range(nb):
            for p in range(nh // 2):
                qp = q_ref[bb, pl.ds(r0, blk), p * pair_w:(p + 1) * pair_w].astype(F32)
                qn = qp * lax.rsqrt(half_mean_sq(qp) + EPS)
                qm += [(qn * q_gain[0]).astype(BF16), (qn * q_gain[1]).astype(BF16)]
        s = [jnp.where(valid, _dot_nt(qm[u], kk_ref[bb, h // group, pl.ds(r0, 2 * blk), :]), -jnp.inf)
             for u, (bb, h) in enumerate(units)]
        sink = [sink_ref[h] * LOG2E for bb, h in units]
        m = [jnp.maximum(jnp.max(s[u], axis=-1, keepdims=True), sink[u]) for u in nu]
        p_ = [jnp.exp2(s[u] - m[u]) for u in nu]
        inv = [1.0 / (jnp.sum(p_[u], axis=-1, keepdims=True) + jnp.exp2(sink[u] - m[u])) for u in nu]
        o = [_dot(p_[u].astype(BF16), vv_ref[bb, h // group, pl.ds(r0, 2 * blk), :]) * inv[u]
             for u, (bb, h) in enumerate(units)]
        for bb in range(nb):
            outs = []
            for p in range(nh // 2):
                u = bb * nh + 2 * p
                op = jnp.where(lo, o[u], o[u + 1])
                outs.append(op * lax.rsqrt(half_mean_sq(op) + EPS) * og_ref[:, p * pair_w:(p + 1) * pair_w])
            o_ref[bb, pl.ds(r0, blk), :] = jnp.concatenate(outs, axis=-1).astype(o_ref.dtype)
        return carry

    lax.fori_loop(0, s_len // blk, body, 0)


def _swa(proj3, q_g, k_g2, sinks, out_g, nb=SWA_BATCH):
    b, s, _ = proj3.shape
    return pl.pallas_call(
        _swa_kernel,
        grid=(b // nb,),
        in_specs=[
            pl.BlockSpec((nb, s, ATTN_WIDTH), lambda i: (i, 0, OFF_AQ // ATTN_WIDTH)),
            pl.BlockSpec((nb, s, ATTN_KV_WIDTH), lambda i: (i, 0, OFF_AK // ATTN_KV_WIDTH)),
            pl.BlockSpec((nb, s, ATTN_KV_WIDTH), lambda i: (i, 0, OFF_AV // ATTN_KV_WIDTH)),
            pl.BlockSpec((1, 2 * ATTN_HEAD_DIM), lambda i: (0, 0)),
            pl.BlockSpec((1, ATTN_KV_WIDTH), lambda i: (0, 0)),
            pl.BlockSpec(memory_space=pltpu.SMEM),
            pl.BlockSpec((1, ATTN_WIDTH), lambda i: (0, 0)),
        ],
        out_specs=pl.BlockSpec((nb, s, ATTN_WIDTH), lambda i: (i, 0, 0)),
        out_shape=jax.ShapeDtypeStruct((b, s, ATTN_WIDTH), BF16),
        scratch_shapes=[pltpu.VMEM((nb, ATTN_KV_HEADS, s + ATTN_BLOCK, 2 * ATTN_HEAD_DIM), BF16),
                        pltpu.VMEM((nb, ATTN_KV_HEADS, s + ATTN_BLOCK, 2 * ATTN_HEAD_DIM), BF16)],
        compiler_params=pltpu.CompilerParams(
            dimension_semantics=("parallel",), vmem_limit_bytes=VMEM_LIMIT),
        name="swa",
    )(proj3, proj3, proj3, q_g, k_g2, sinks, out_g)


def _gdn_kernel(n_cast, qkv_ref, z_ref, gate_ref, cw_ref, alog_ref, dtb_ref, ng_ref, *refs):
    cast_src = refs[:n_cast]
    o_ref = refs[n_cast]
    cast_dst = refs[n_cast + 1:2 * n_cast + 1]
    halo_ref, state_ref, win_ref = refs[2 * n_cast + 1:]
    for src, dst in zip(cast_src, cast_dst):
        dst[...] = src[...].astype(dst.dtype)
    nb, ts = qkv_ref.shape[0], qkv_ref.shape[1]
    c = GDN_CHUNK
    hd = GDN_HEAD_DIM
    nh = GDN_HEADS
    hblk = 2 * SUBLANES
    units = [(bb, h) for bb in range(nb) for h in range(nh)]
    nu = range(len(units))

    @pl.when(pl.program_id(1) == 0)
    def _():
        state_ref[...] = jnp.zeros(state_ref.shape, F32)
        halo_ref[...] = jnp.zeros(halo_ref.shape, halo_ref.dtype)

    row = lax.broadcasted_iota(jnp.int32, (c, c), 0)
    col = lax.broadcasted_iota(jnp.int32, (c, c), 1)
    tril = row >= col
    strict = row > col
    blk16 = (row // GDN_INV_BLOCK) == (col // GDN_INV_BLOCK)
    blk32 = (row // (2 * GDN_INV_BLOCK)) == (col // (2 * GDN_INV_BLOCK))
    tril_f = tril.astype(F32)
    sel = (lax.broadcasted_iota(jnp.int32, (2 * nh, LANES), 0)
           == lax.broadcasted_iota(jnp.int32, (2 * nh, LANES), 1)).astype(F32)
    is_beta = lax.broadcasted_iota(jnp.int32, (1, LANES), 1) < nh
    neg_a = -jnp.exp(alog_ref[...])
    dtb = dtb_ref[...]

    def chunk(ci, carry):
        t0 = pl.multiple_of(ci * c, c)
        tp = pl.multiple_of(jnp.maximum(t0 - hblk, 0), hblk)
        first = ci == 0

        def conv_silu(bb, lo):
            w = cw_ref[:, lo:lo + hd]
            slot = (bb * qkv_ref.shape[2] + lo) // hd
            prev = jnp.where(first, halo_ref[bb, :, lo:lo + hd], qkv_ref[bb, pl.ds(tp, hblk), lo:lo + hd])
            win_ref[slot, 0:hblk, :] = prev.astype(F32)
            win_ref[slot, hblk:, :] = qkv_ref[bb, pl.ds(t0, c), lo:lo + hd].astype(F32)
            y = w[GDN_CONV_K - 1:GDN_CONV_K] * win_ref[slot, hblk:, :]
            for k in range(1, GDN_CONV_K):
                y = y + w[GDN_CONV_K - 1 - k:GDN_CONV_K - k] * win_ref[slot, hblk - k:hblk - k + c, :]
            return _silu(y)

        gmix, gc_all, rows = [], [], []
        for bb in range(nb):
            gl = gate_ref[bb, pl.ds(t0, c), :]
            gm = jnp.where(is_beta, _sigmoid(gl), neg_a * _softplus(gl + dtb))
            ga = _dot(tril_f, gm, HIGHEST)
            gmix.append(gm)
            gc_all.append(ga)
            rows.append(_dot_nt(sel, jnp.concatenate([gm, ga], axis=0), HIGHEST))

        qs, ks, vs = [], [], []
        for bb, h in units:
            q = conv_silu(bb, h * hd)
            k = conv_silu(bb, GDN_WIDTH + h * hd)
            vs.append(conv_silu(bb, 2 * GDN_WIDTH + h * hd))
            qs.append(q * (lax.rsqrt(jnp.sum(q * q, axis=-1, keepdims=True) + EPS) * (hd ** -0.5)))
            ks.append(k * lax.rsqrt(jnp.sum(k * k, axis=-1, keepdims=True) + EPS))

        kkqk = []
        for u in nu:
            kb = ks[u].astype(BF16)
            kkqk.append(_dot_nt(jnp.concatenate([kb, qs[u].astype(BF16)], axis=0), kb))
        beta = [gmix[bb][:, h:h + 1] for bb, h in units]
        gc = [gc_all[bb][:, nh + h:nh + h + 1] for bb, h in units]
        g_last = [gc_all[bb][c - 1:c, nh + h:nh + h + 1] for bb, h in units]
        beta_row = [rows[bb][h:h + 1, :c] for bb, h in units]
        es, ps, qkb, off_diag = [], [], [], []
        for u, (bb, h) in enumerate(units):
            gc_row = rows[bb][nh + h:nh + h + 1, c:]
            decay = jnp.exp(jnp.where(tril, gc[u] - gc_row, 0.0))
            a = jnp.where(strict, kkqk[u][:c] * beta[u] * decay, 0.0)
            qkb.append(jnp.where(tril, kkqk[u][c:] * decay, 0.0).astype(BF16))
            a_d = jnp.where(blk16, a, 0.0)
            es.append(-a_d)
            ps.append(a_d)
            off_diag.append((jnp.where(blk32, a - a_d, 0.0), jnp.where(blk32, 0.0, a)))
        ps = [_dot(ps[u].astype(BF16), ps[u].astype(BF16)) for u in nu]
        for i in range(3):
            if i < 2:
                ep = [_dot(jnp.concatenate([es[u], ps[u]], axis=0).astype(BF16), ps[u].astype(BF16)) for u in nu]
                es = [es[u] + ps[u] + ep[u][:c] for u in nu]
                ps = [ep[u][c:] for u in nu]
            else:
                ep = [_dot(es[u].astype(BF16), ps[u].astype(BF16)) for u in nu]
                es = [es[u] + ps[u] + ep[u] for u in nu]
        for level in range(2):
            ls = [off_diag[u][level] for u in nu]
            g1 = [ls[u] + _dot(es[u].astype(BF16), ls[u].astype(BF16)) for u in nu]
            es = [es[u] - (g1[u] + _dot(g1[u].astype(BF16), es[u].astype(BF16))) for u in nu]
        egc = [jnp.exp(gc[u]) for u in nu]
        wu = []
        for u in nu:
            kv = jnp.concatenate([ks[u] * egc[u], vs[u]], axis=-1)
            e_b = (es[u] * beta_row[u]).astype(BF16)
            wu.append(kv * beta[u] + _dot(e_b, kv.astype(BF16)))
        st = [state_ref[bb, h] for bb, h in units]
        ws_qs = [_dot(jnp.concatenate([wu[u][:, :hd], qs[u] * egc[u]], axis=0).astype(BF16), st[u].astype(BF16))
                 for u in nu]
        vb = [(wu[u][:, hd:] - ws_qs[u][:c]).astype(BF16) for u in nu]
        kd = [(ks[u] * jnp.exp(g_last[u] - gc[u])).astype(BF16) for u in nu]
        os_ = [ws_qs[u][c:] + _dot(qkb[u], vb[u]) for u in nu]
        for u, (bb, h) in enumerate(units):
            state_ref[bb, h] = st[u] * jnp.exp(g_last[u]) + _dot_tn(kd[u], vb[u])
        for u, (bb, h) in enumerate(units):
            o = os_[u]
            oms = jnp.mean(o * o, axis=-1, keepdims=True)
            zh = z_ref[bb, pl.ds(t0, c), h * hd:(h + 1) * hd].astype(F32)
            y = o * lax.rsqrt(oms + EPS) * ng_ref[...] * _silu(zh)
            o_ref[bb, pl.ds(t0, c), h * hd:(h + 1) * hd] = y.astype(o_ref.dtype)
        return carry

    lax.fori_loop(0, ts // c, chunk, 0)
    halo_ref[...] = qkv_ref[:, ts - hblk:ts, :]


def _gdn(proj3, gate3, conv_w, alog_b, dtb_b, norm_g, w_out, w_up, w_down, w_in_t, layer, ts=256, nb=GDN_BATCH):
    b, s, _ = proj3.shape
    qkv_w = 3 * GDN_WIDTH
    nt = s // ts
    steps = (b // nb) * nt
    chunks = ts // GDN_CHUNK
    weights = (w_out, w_up, w_down)
    slab = [(w.shape[1] // steps, w.shape[2]) for w in weights]
    assert all(w.shape[1] % steps == 0 and rows % (2 * SUBLANES * chunks) == 0
               for w, (rows, _) in zip(weights, slab))
    cast_in = [pl.BlockSpec((None,) + sl, lambda i, t: (layer, i * nt + t, 0)) for sl in slab]
    cast_out = [pl.BlockSpec(sl, lambda i, t: (i * nt + t, 0)) for sl in slab]
    cast_shape = [jax.ShapeDtypeStruct(w.shape[1:], BF16) for w in weights]
    cast_args = list(weights)
    if layer + 1 < w_in_t.shape[0]:
        n_blk = D_IN_MAIN // REGROUP_BLOCK
        assert n_blk <= steps

        def dst_block(i, t):
            return jnp.minimum(i * nt + t, n_blk - 1)

        cast_in.append(pl.BlockSpec((None, REGROUP_BLOCK, D_MODEL),
                                    lambda i, t: (layer + 1, _regroup_src_block(dst_block(i, t)), 0)))
        cast_out.append(pl.BlockSpec((REGROUP_BLOCK, D_MODEL), lambda i, t: (dst_block(i, t), 0)))
        cast_shape.append(jax.ShapeDtypeStruct((D_IN_MAIN, D_MODEL), BF16))
        cast_args.append(w_in_t)
    return pl.pallas_call(
        functools.partial(_gdn_kernel, len(cast_args)),
        grid=(b // nb, nt),
        in_specs=[
            pl.BlockSpec((nb, ts, qkv_w), lambda i, t: (i, t, OFF_GQKV // qkv_w)),
            pl.BlockSpec((nb, ts, GDN_WIDTH), lambda i, t: (i, t, OFF_GZ // GDN_WIDTH)),
            pl.BlockSpec((nb, ts, GATE_PAD), lambda i, t: (i, t, 0)),
            pl.BlockSpec((GDN_CONV_K, qkv_w), lambda i, t: (0, 0)),
            pl.BlockSpec((1, LANES), lambda i, t: (0, 0)),
            pl.BlockSpec((1, LANES), lambda i, t: (0, 0)),
            pl.BlockSpec((1, GDN_HEAD_DIM), lambda i, t: (0, 0)),
        ] + cast_in,
        out_specs=[pl.BlockSpec((nb, ts, GDN_WIDTH), lambda i, t: (i, t, 0))] + cast_out,
        out_shape=[jax.ShapeDtypeStruct((b, s, GDN_WIDTH), BF16)] + cast_shape,
        scratch_shapes=[pltpu.VMEM((nb, 2 * SUBLANES, qkv_w), BF16),
                        pltpu.VMEM((nb, GDN_HEADS, GDN_HEAD_DIM, GDN_HEAD_DIM), F32),
                        pltpu.VMEM((nb * qkv_w // GDN_HEAD_DIM, 2 * SUBLANES + GDN_CHUNK, GDN_HEAD_DIM), F32)],
        compiler_params=pltpu.CompilerParams(
            dimension_semantics=("parallel", "arbitrary"), vmem_limit_bytes=VMEM_LIMIT),
        name="gdn",
    )(proj3, proj3, gate3, conv_w, alog_b, dtb_b, norm_g, *cast_args)


def _out_proj_kernel(x_ref, yc_ref, ya_ref, yg_ref, w_ref, g_ref, x1_ref, h_ref):
    acc = _dot(yc_ref[...], w_ref[0:CONV_WIDTH, :])
    acc = acc + _dot(ya_ref[...], w_ref[CONV_WIDTH:CONV_WIDTH + ATTN_WIDTH, :])
    acc = acc + _dot(yg_ref[...], w_ref[CONV_WIDTH + ATTN_WIDTH:, :])
    x1 = x_ref[...] + acc
    x1_ref[...] = x1
    ms = jnp.mean(x1 * x1, axis=-1, keepdims=True)
    h_ref[...] = (x1 * lax.rsqrt(ms + EPS) * g_ref[...]).astype(BF16)


def _out_proj(x2, yc, ya, yg, w, g, tm=512):
    m = x2.shape[0]
    return pl.pallas_call(
        _out_proj_kernel,
        grid=(m // tm,),
        in_specs=[
            pl.BlockSpec((tm, D_MODEL), lambda i: (i, 0)),
            pl.BlockSpec((tm, CONV_WIDTH), lambda i: (i, 0)),
            pl.BlockSpec((tm, ATTN_WIDTH), lambda i: (i, 0)),
            pl.BlockSpec((tm, GDN_WIDTH), lambda i: (i, 0)),
            pl.BlockSpec((D_MODEL, D_MODEL), lambda i: (0, 0)),
            pl.BlockSpec((1, D_MODEL), lambda i: (0, 0)),
        ],
        out_specs=[pl.BlockSpec((tm, D_MODEL), lambda i: (i, 0)),
                   pl.BlockSpec((tm, D_MODEL), lambda i: (i, 0))],
        out_shape=[jax.ShapeDtypeStruct((m, D_MODEL), F32),
                   jax.ShapeDtypeStruct((m, D_MODEL), BF16)],
        compiler_params=pltpu.CompilerParams(
            dimension_semantics=("parallel",), vmem_limit_bytes=VMEM_LIMIT),
        name="out_proj",
    )(x2, yc, ya, yg, w, g)


def _mlp_kernel(x1_ref, h_ref, wu_ref, wd_ref, o_ref):
    @pl.when(pl.program_id(1) == 0)
    def _():
        o_ref[...] = x1_ref[...]

    hid = jnp.maximum(_dot(h_ref[...], wu_ref[...]), 0.0)
    hid = (hid * hid).astype(BF16)
    o_ref[...] += _dot(hid, wd_ref[...])


def _mlp(x1, h, w_up, w_down, tm=512, tf=2048):
    m = x1.shape[0]
    return pl.pallas_call(
        _mlp_kernel,
        grid=(m // tm, D_FF // tf),
        in_specs=[
            pl.BlockSpec((tm, D_MODEL), lambda i, f: (i, 0)),
            pl.BlockSpec((tm, D_MODEL), lambda i, f: (i, 0)),
            pl.BlockSpec((D_MODEL, tf), lambda i, f: (0, f)),
            pl.BlockSpec((tf, D_MODEL), lambda i, f: (f, 0)),
        ],
        out_specs=pl.BlockSpec((tm, D_MODEL), lambda i, f: (i, 0)),
        out_shape=jax.ShapeDtypeStruct((m, D_MODEL), F32),
        compiler_params=pltpu.CompilerParams(
            dimension_semantics=("parallel", "arbitrary"), vmem_limit_bytes=VMEM_LIMIT),
        name="mlp",
    )(x1, h, w_up, w_down)


def _gate_w_in(w_in):
    gate = w_in[..., D_IN_MAIN:].astype(BF16)
    return jnp.pad(gate, ((0, 0), (0, 0), (0, GATE_PAD - gate.shape[-1])))


def _decay_lanes(p):
    return jnp.zeros((1, LANES), F32).at[0, GDN_HEADS:2 * GDN_HEADS].set(p)


def _layer(x2, b, s, layer, w_in_b, w_in_t, w_gate_all, w_out, w_up, w_down, norm1_g, conv_w, conv_out_g,
           q_norm_g, k_norm_g, attn_sinks, attn_out_g, gdn_conv_w, gdn_A_log, gdn_dt_bias, gdn_norm_g, norm2_g):
    proj, gates = _in_proj(x2, norm1_g[None, :], w_in_b, w_gate_all, layer)
    proj3 = proj.reshape(b, s, D_IN_MAIN)
    yc = _conv_mixer(proj3, conv_w, conv_out_g[None, :])
    ya = _swa(proj3, jnp.tile(q_norm_g, 2)[None, :], jnp.tile(k_norm_g, ATTN_KV_HEADS)[None, :], attn_sinks,
              attn_out_g[None, :])
    yg, w_out_b, w_up_b, w_down_b, *w_in_next = _gdn(
        proj3, gates.reshape(b, s, GATE_PAD), gdn_conv_w, _decay_lanes(gdn_A_log), _decay_lanes(gdn_dt_bias),
        gdn_norm_g[None, :], w_out, w_up, w_down, w_in_t, layer)
    m = b * s
    x1, h2 = _out_proj(x2, yc.reshape(m, CONV_WIDTH), ya.reshape(m, ATTN_WIDTH), yg.reshape(m, GDN_WIDTH),
                       w_out_b, norm2_g[None, :])
    return _mlp(x1, h2, w_up_b, w_down_b), (w_in_next[0] if w_in_next else None)


def kernel(x, norm1_g, w_in, conv_w, conv_out_g, q_norm_g, k_norm_g, attn_sinks, attn_out_g, gdn_conv_w,
           gdn_A_log, gdn_dt_bias, gdn_norm_g, w_out, norm2_g, w_up, w_down):
    b, s, d = x.shape
    x2 = x.reshape(b * s, d)
    w_in_t, w_gate_all = jnp.swapaxes(w_in, 1, 2), _gate_w_in(w_in)
    w_in_b = _regroup_w_in(w_in_t, 0)
    for l in range(norm1_g.shape[0]):
        x2, w_in_b = _layer(x2, b, s, l, w_in_b, w_in_t, w_gate_all, w_out, w_up, w_down, norm1_g[l], conv_w[l],
                            conv_out_g[l], q_norm_g[l], k_norm_g[l], attn_sinks[l], attn_out_g[l], gdn_conv_w[l],
                            gdn_A_log[l], gdn_dt_bias[l], gdn_norm_g[l], norm2_g[l])
    return x2.reshape(b, s, d)
```

```python
import functools

import jax
import jax.numpy as jnp
from jax import lax
from jax.experimental import pallas as pl
from jax.experimental.pallas import tpu as pltpu

F32 = jnp.float32
BF16 = jnp.bfloat16
HIGHEST = lax.Precision.HIGHEST

D_MODEL = 2048
CONV_WIDTH = 512
CONV_GROUPS = 4
CONV_GROUP_DIM = 128
ATTN_HEAD_DIM = 64
ATTN_HEADS = 8
ATTN_KV_HEADS = 2
ATTN_WIDTH = 512
ATTN_KV_WIDTH = 128
WINDOW = 128
ATTN_BLOCK = 128
SWA_BATCH = 2
CONV_BATCH = 2
GDN_HEAD_DIM = 128
GDN_WIDTH = 1024
GDN_HEADS = 8
GDN_CONV_K = 4
GDN_CHUNK = 64
GDN_BATCH = 2
GDN_INV_BLOCK = 16
assert GDN_CHUNK == 4 * GDN_INV_BLOCK
D_FF = 8192
EPS = 1e-6
LOG2E = 1.4426950408889634

LANES = 128
SUBLANES = 8
VMEM_LIMIT = 56 * 1024 * 1024

GATE_PAD = LANES
MXU_WIDTH = 256
REGROUP_BLOCK = MXU_WIDTH
D_IN_MAIN = 4 * GDN_WIDTH + 3 * CONV_WIDTH + ATTN_WIDTH + 2 * ATTN_KV_WIDTH
IN_PROJ_TN = 5 * MXU_WIDTH
assert D_IN_MAIN % IN_PROJ_TN == 0
OFF_GQKV = 0
OFF_GZ = 3 * GDN_WIDTH
OFF_CB = 4 * GDN_WIDTH
OFF_CC = OFF_CB + CONV_WIDTH
OFF_CX = OFF_CC + CONV_WIDTH
OFF_AQ = OFF_CX + CONV_WIDTH
OFF_AK = OFF_AQ + ATTN_WIDTH
OFF_AV = OFF_AK + ATTN_KV_WIDTH
assert OFF_AV + ATTN_KV_WIDTH == D_IN_MAIN


def _dot(a, b, precision=None):
    return jnp.dot(a, b, preferred_element_type=F32, precision=precision)


def _dot_nt(a, b, precision=None):
    return lax.dot_general(a, b, (((1,), (1,)), ((), ())), preferred_element_type=F32, precision=precision)


def _dot_tn(a, b, precision=None):
    return lax.dot_general(a, b, (((0,), (0,)), ((), ())), preferred_element_type=F32, precision=precision)


def _sigmoid(x):
    return 0.5 + 0.5 * jnp.tanh(0.5 * x)


def _silu(x):
    hx = 0.5 * x
    return hx + hx * jnp.tanh(hx)


def _softplus(x):
    return jnp.maximum(x, 0.0) + jnp.log(1.0 + jnp.exp(-jnp.abs(x)))


def _regroup_kernel(w_ref, o_ref):
    o_ref[...] = w_ref[...].astype(o_ref.dtype)


def _regroup_src_block(j):
    n_front = (3 * CONV_WIDTH + ATTN_WIDTH + 2 * ATTN_KV_WIDTH) // REGROUP_BLOCK
    n_gdn = 4 * GDN_WIDTH // REGROUP_BLOCK
    return jnp.where(j < n_gdn, j + n_front, j - n_gdn)


def _regroup_w_in(w_in_t, layer):
    blk = REGROUP_BLOCK
    return pl.pallas_call(
        _regroup_kernel,
        grid=(D_IN_MAIN // blk,),
        in_specs=[pl.BlockSpec((None, blk, D_MODEL), lambda j: (layer, _regroup_src_block(j), 0))],
        out_specs=pl.BlockSpec((blk, D_MODEL), lambda j: (j, 0)),
        out_shape=jax.ShapeDtypeStruct((D_IN_MAIN, D_MODEL), BF16),
        compiler_params=pltpu.CompilerParams(
            dimension_semantics=("parallel",), vmem_limit_bytes=VMEM_LIMIT),
        name="regroup_w_in",
    )(w_in_t)


def _in_proj_kernel(x_ref, g_ref, w_ref, wg_ref, o_ref, og_ref):
    x = x_ref[...]
    ms = jnp.mean(x * x, axis=-1, keepdims=True)
    h = (x * lax.rsqrt(ms + EPS) * g_ref[...]).astype(BF16)
    og_ref[...] = _dot(h, wg_ref[...])
    for c0 in range(0, w_ref.shape[0], IN_PROJ_TN):
        o_ref[:, c0:c0 + IN_PROJ_TN] = _dot_nt(h, w_ref[c0:c0 + IN_PROJ_TN, :]).astype(o_ref.dtype)


def _in_proj(x2, g, w, wg_all, layer, tm=512):
    m = x2.shape[0]
    n = w.shape[0]
    return pl.pallas_call(
        _in_proj_kernel,
        grid=(m // tm,),
        in_specs=[
            pl.BlockSpec((tm, D_MODEL), lambda i: (i, 0)),
            pl.BlockSpec((1, D_MODEL), lambda i: (0, 0)),
            pl.BlockSpec((n, D_MODEL), lambda i: (0, 0), pipeline_mode=pl.Buffered(1)),
            pl.BlockSpec((None, D_MODEL, GATE_PAD), lambda i: (layer, 0, 0), pipeline_mode=pl.Buffered(1)),
        ],
        out_specs=[pl.BlockSpec((tm, n), lambda i: (i, 0)),
                   pl.BlockSpec((tm, GATE_PAD), lambda i: (i, 0))],
        out_shape=[jax.ShapeDtypeStruct((m, n), BF16),
                   jax.ShapeDtypeStruct((m, GATE_PAD), F32)],
        compiler_params=pltpu.CompilerParams(
            dimension_semantics=("parallel",), vmem_limit_bytes=VMEM_LIMIT),
        name="in_proj",
    )(x2, g, w, wg_all)


def _conv_kernel(cb_ref, cc_ref, cx_ref, w_ref, g_ref, o_ref, zp_ref):
    s_len = cc_ref.shape[1]
    w = w_ref[...]
    for bb in range(cc_ref.shape[0]):
        zp_ref[bb, 0:SUBLANES, :] = jnp.zeros((SUBLANES, zp_ref.shape[2]), F32)
        zp_ref[bb, SUBLANES:, :] = cc_ref[bb].astype(F32) * cx_ref[bb].astype(F32)
        y = (w[0:1] * zp_ref[bb, SUBLANES - 2:SUBLANES - 2 + s_len, :]
             + w[1:2] * zp_ref[bb, SUBLANES - 1:SUBLANES - 1 + s_len, :] + w[2:3] * zp_ref[bb, SUBLANES:, :])
        y = cb_ref[bb].astype(F32) * y
        ms = jnp.mean(y * y, axis=-1, keepdims=True)
        o_ref[bb] = (y * lax.rsqrt(ms + EPS) * g_ref[...]).astype(o_ref.dtype)


def _conv_mixer(proj3, conv_w, conv_out_g, nb=CONV_BATCH):
    b, s, _ = proj3.shape
    gd = CONV_GROUP_DIM

    def col(off):
        return lambda i, g: (i, 0, off // gd + g)

    return pl.pallas_call(
        _conv_kernel,
        grid=(b // nb, CONV_GROUPS),
        in_specs=[
            pl.BlockSpec((nb, s, gd), col(OFF_CB)),
            pl.BlockSpec((nb, s, gd), col(OFF_CC)),
            pl.BlockSpec((nb, s, gd), col(OFF_CX)),
            pl.BlockSpec((3, gd), lambda i, g: (0, g)),
            pl.BlockSpec((1, gd), lambda i, g: (0, g)),
        ],
        out_specs=pl.BlockSpec((nb, s, gd), lambda i, g: (i, 0, g)),
        out_shape=jax.ShapeDtypeStruct((b, s, CONV_WIDTH), BF16),
        scratch_shapes=[pltpu.VMEM((nb, SUBLANES + s, gd), F32)],
        compiler_params=pltpu.CompilerParams(
            dimension_semantics=("parallel", "parallel"), vmem_limit_bytes=VMEM_LIMIT),
        name="conv_mixer",
    )(proj3, proj3, proj3, conv_w, conv_out_g)


def _swa_kernel(q_ref, k_ref, v_ref, qg_ref, kg_ref, sink_ref, og_ref, o_ref, kk_ref, vv_ref):
    nb, s_len = q_ref.shape[0], q_ref.shape[1]
    blk = ATTN_BLOCK
    hd = ATTN_HEAD_DIM
    nh = ATTN_HEADS
    group = nh // ATTN_KV_HEADS
    pair_w = 2 * hd
    units = [(bb, h) for bb in range(nb) for h in range(nh)]
    nu = range(len(units))

    lane = lax.broadcasted_iota(jnp.int32, (1, pair_w), 1)
    lo = lane < hd
    half_mask = (lo.astype(F32), 1.0 - lo.astype(F32))
    avg = jnp.where(lax.broadcasted_iota(jnp.int32, (pair_w, pair_w), 0) // hd
                    == lax.broadcasted_iota(jnp.int32, (pair_w, pair_w), 1) // hd, 1.0 / hd, 0.0).astype(BF16)

    def half_mean_sq(x):
        x2 = x * x
        hi = x2.astype(BF16)
        rem = (x2 - hi.astype(F32)).astype(BF16)
        return _dot(hi, avg) + _dot(rem, avg)

    zeros = jnp.zeros((blk, pair_w), BF16)
    for bb in range(nb):
        k = k_ref[bb].astype(F32)
        kn = k * lax.rsqrt(half_mean_sq(k) + EPS) * kg_ref[...]
        ksw = pltpu.roll(kn, hd, axis=1)
        v = v_ref[bb].astype(F32)
        vsw = pltpu.roll(v, hd, axis=1)
        for j in range(ATTN_KV_HEADS):
            kk_ref[bb, j, 0:blk, :] = zeros
            vv_ref[bb, j, 0:blk, :] = zeros
        kk_ref[bb, 0, blk:, :] = jnp.where(lo, kn, ksw).astype(BF16)
        kk_ref[bb, 1, blk:, :] = jnp.where(lo, ksw, kn).astype(BF16)
        vv_ref[bb, 0, blk:, :] = jnp.where(lo, v, vsw).astype(BF16)
        vv_ref[bb, 1, blk:, :] = jnp.where(lo, vsw, v).astype(BF16)

    qi = lax.broadcasted_iota(jnp.int32, (blk, 2 * blk), 0)
    si = lax.broadcasted_iota(jnp.int32, (blk, 2 * blk), 1)
    rel = qi + blk - si
    band = (rel >= 0) & (rel < WINDOW)
    cur = si >= blk
    q_gain = [qg_ref[...] * (hd ** -0.5 * LOG2E) * half_mask[i] for i in range(2)]

    def body(n, carry):
        r0 = pl.multiple_of(n * blk, blk)
        valid = band & (cur | (n > 0))
        qm = []
        for bb in range(nb):
            for p in range(nh // 2):
                qp = q_ref[bb, pl.ds(r0, blk), p * pair_w:(p + 1) * pair_w].astype(F32)
                qn = qp * lax.rsqrt(half_mean_sq(qp) + EPS)
                qm += [(qn * q_gain[0]).astype(BF16), (qn * q_gain[1]).astype(BF16)]
        s = [jnp.where(valid, _dot_nt(qm[u], kk_ref[bb, h // group, pl.ds(r0, 2 * blk), :]), -jnp.inf)
             for u, (bb, h) in enumerate(units)]
        sink = [sink_ref[h] * LOG2E for bb, h in units]
        m = [jnp.maximum(jnp.max(s[u], axis=-1, keepdims=True), sink[u]) for u in nu]
        p_ = [jnp.exp2(s[u] - m[u]) for u in nu]
        inv = [1.0 / (jnp.sum(p_[u], axis=-1, keepdims=True) + jnp.exp2(sink[u] - m[u])) for u in nu]
        o = [_dot(p_[u].astype(BF16), vv_ref[bb, h // group, pl.ds(r0, 2 * blk), :]) * inv[u]
             for u, (bb, h) in enumerate(units)]
        for bb in range(nb):
            outs = []
            for p in range(nh // 2):
                u = bb * nh + 2 * p
                op = jnp.where(lo, o[u], o[u + 1])
                outs.append(op * lax.rsqrt(half_mean_sq(op) + EPS) * og_ref[:, p * pair_w:(p + 1) * pair_w])
            o_ref[bb, pl.ds(r0, blk), :] = jnp.concatenate(outs, axis=-1).astype(o_ref.dtype)
        return carry

    lax.fori_loop(0, s_len // blk, body, 0)


def _swa(proj3, q_g, k_g2, sinks, out_g, nb=SWA_BATCH):
    b, s, _ = proj3.shape
    return pl.pallas_call(
        _swa_kernel,
        grid=(b // nb,),
        in_specs=[
            pl.BlockSpec((nb, s, ATTN_WIDTH), lambda i: (i, 0, OFF_AQ // ATTN_WIDTH)),
            pl.BlockSpec((nb, s, ATTN_KV_WIDTH), lambda i: (i, 0, OFF_AK // ATTN_KV_WIDTH)),
            pl.BlockSpec((nb, s, ATTN_KV_WIDTH), lambda i: (i, 0, OFF_AV // ATTN_KV_WIDTH)),
            pl.BlockSpec((1, 2 * ATTN_HEAD_DIM), lambda i: (0, 0)),
            pl.BlockSpec((1, ATTN_KV_WIDTH), lambda i: (0, 0)),
            pl.BlockSpec(memory_space=pltpu.SMEM),
            pl.BlockSpec((1, ATTN_WIDTH), lambda i: (0, 0)),
        ],
        out_specs=pl.BlockSpec((nb, s, ATTN_WIDTH), lambda i: (i, 0, 0)),
        out_shape=jax.ShapeDtypeStruct((b, s, ATTN_WIDTH), BF16),
        scratch_shapes=[pltpu.VMEM((nb, ATTN_KV_HEADS, s + ATTN_BLOCK, 2 * ATTN_HEAD_DIM), BF16),
                        pltpu.VMEM((nb, ATTN_KV_HEADS, s + ATTN_BLOCK, 2 * ATTN_HEAD_DIM), BF16)],
        compiler_params=pltpu.CompilerParams(
            dimension_semantics=("parallel",), vmem_limit_bytes=VMEM_LIMIT),
        name="swa",
    )(proj3, proj3, proj3, q_g, k_g2, sinks, out_g)


def _gdn_kernel(n_cast, qkv_ref, z_ref, gate_ref, cw_ref, alog_ref, dtb_ref, ng_ref, *refs):
    cast_src = refs[:n_cast]
    o_ref = refs[n_cast]
    cast_dst = refs[n_cast + 1:2 * n_cast + 1]
    halo_ref, state_ref, win_ref = refs[2 * n_cast + 1:]
    for src, dst in zip(cast_src, cast_dst):
        dst[...] = src[...].astype(dst.dtype)
    nb, ts = qkv_ref.shape[0], qkv_ref.shape[1]
    c = GDN_CHUNK
    hd = GDN_HEAD_DIM
    nh = GDN_HEADS
    hblk = 2 * SUBLANES
    units = [(bb, h) for bb in range(nb) for h in range(nh)]
    nu = range(len(units))

    @pl.when(pl.program_id(1) == 0)
    def _():
        state_ref[...] = jnp.zeros(state_ref.shape, F32)
        halo_ref[...] = jnp.zeros(halo_ref.shape, halo_ref.dtype)

    row = lax.broadcasted_iota(jnp.int32, (c, c), 0)
    col = lax.broadcasted_iota(jnp.int32, (c, c), 1)
    tril = row >= col
    strict = row > col
    blk16 = (row // GDN_INV_BLOCK) == (col // GDN_INV_BLOCK)
    blk32 = (row // (2 * GDN_INV_BLOCK)) == (col // (2 * GDN_INV_BLOCK))
    tril_f = tril.astype(F32)
    sel = (lax.broadcasted_iota(jnp.int32, (2 * nh, LANES), 0)
           == lax.broadcasted_iota(jnp.int32, (2 * nh, LANES), 1)).astype(F32)
    is_beta = lax.broadcasted_iota(jnp.int32, (1, LANES), 1) < nh
    ones_sq = jnp.ones((hd, hd), BF16)
    neg_a = -jnp.exp(alog_ref[...])
    dtb = dtb_ref[...]

    def chunk(ci, carry):
        t0 = pl.multiple_of(ci * c, c)
        tp = pl.multiple_of(jnp.maximum(t0 - hblk, 0), hblk)
        first = ci == 0

        def conv_silu(bb, lo):
            w = cw_ref[:, lo:lo + hd]
            slot = (bb * qkv_ref.shape[2] + lo) // hd
            prev = jnp.where(first, halo_ref[bb, :, lo:lo + hd], qkv_ref[bb, pl.ds(tp, hblk), lo:lo + hd])
            win_ref[slot, 0:hblk, :] = prev.astype(F32)
            win_ref[slot, hblk:, :] = qkv_ref[bb, pl.ds(t0, c), lo:lo + hd].astype(F32)
            y = w[GDN_CONV_K - 1:GDN_CONV_K] * win_ref[slot, hblk:, :]
            for k in range(1, GDN_CONV_K):
                y = y + w[GDN_CONV_K - 1 - k:GDN_CONV_K - k] * win_ref[slot, hblk - k:hblk - k + c, :]
            return _silu(y)

        gmix, gc_all, rows = [], [], []
        for bb in range(nb):
            gl = gate_ref[bb, pl.ds(t0, c), :]
            gm = jnp.where(is_beta, _sigmoid(gl), neg_a * _softplus(gl + dtb))
            ga = _dot(tril_f, gm, HIGHEST)
            gmix.append(gm)
            gc_all.append(ga)
            rows.append(_dot_nt(sel, jnp.concatenate([gm, ga], axis=0), HIGHEST))

        qs, ks, vs = [], [], []
        for bb, h in units:
            q = conv_silu(bb, h * hd)
            k = conv_silu(bb, GDN_WIDTH + h * hd)
            vs.append(conv_silu(bb, 2 * GDN_WIDTH + h * hd))
            qs.append(q * (lax.rsqrt(_dot((q * q).astype(BF16), ones_sq) + EPS) * (hd ** -0.5)))
            ks.append(k * lax.rsqrt(_dot((k * k).astype(BF16), ones_sq) + EPS))

        kkqk = []
        for u in nu:
            kb = ks[u].astype(BF16)
            kkqk.append(_dot_nt(jnp.concatenate([kb, qs[u].astype(BF16)], axis=0), kb))
        beta = [gmix[bb][:, h:h + 1] for bb, h in units]
        gc = [gc_all[bb][:, nh + h:nh + h + 1] for bb, h in units]
        g_last = [gc_all[bb][c - 1:c, nh + h:nh + h + 1] for bb, h in units]
        beta_row = [rows[bb][h:h + 1, :c] for bb, h in units]
        es, ps, qkb, off_diag = [], [], [], []
        for u, (bb, h) in enumerate(units):
            gc_row = rows[bb][nh + h:nh + h + 1, c:]
            decay = jnp.exp(jnp.where(tril, gc[u] - gc_row, 0.0))
            a = jnp.where(strict, kkqk[u][:c] * beta[u] * decay, 0.0)
            qkb.append(jnp.where(tril, kkqk[u][c:] * decay, 0.0).astype(BF16))
            a_d = jnp.where(blk16, a, 0.0)
            es.append(-a_d)
            ps.append(a_d)
            off_diag.append((jnp.where(blk32, a - a_d, 0.0), jnp.where(blk32, 0.0, a)))
        ps = [_dot(ps[u].astype(BF16), ps[u].astype(BF16)) for u in nu]
        for i in range(3):
            if i < 2:
                ep = [_dot(jnp.concatenate([es[u], ps[u]], axis=0).astype(BF16), ps[u].astype(BF16)) for u in nu]
                es = [es[u] + ps[u] + ep[u][:c] for u in nu]
                ps = [ep[u][c:] for u in nu]
            else:
                ep = [_dot(es[u].astype(BF16), ps[u].astype(BF16)) for u in nu]
                es = [es[u] + ps[u] + ep[u] for u in nu]
        for level in range(2):
            ls = [off_diag[u][level] for u in nu]
            g1 = [ls[u] + _dot(es[u].astype(BF16), ls[u].astype(BF16)) for u in nu]
            es = [es[u] - (g1[u] + _dot(g1[u].astype(BF16), es[u].astype(BF16))) for u in nu]
        egc = [jnp.exp(gc[u]) for u in nu]
        wu = []
        for u in nu:
            kv = jnp.concatenate([ks[u] * egc[u], vs[u]], axis=-1)
            e_b = (es[u] * beta_row[u]).astype(BF16)
            wu.append(kv * beta[u] + _dot(e_b, kv.astype(BF16)))
        st = [state_ref[bb, h] for bb, h in units]
        ws_qs = [_dot(jnp.concatenate([wu[u][:, :hd], qs[u] * egc[u]], axis=0).astype(BF16), st[u].astype(BF16))
                 for u in nu]
        vb = [(wu[u][:, hd:] - ws_qs[u][:c]).astype(BF16) for u in nu]
        kd = [(ks[u] * jnp.exp(g_last[u] - gc[u])).astype(BF16) for u in nu]
        os_ = [ws_qs[u][c:] + _dot(qkb[u], vb[u]) for u in nu]
        for u, (bb, h) in enumerate(units):
            state_ref[bb, h] = st[u] * jnp.exp(g_last[u]) + _dot_tn(kd[u], vb[u])
        for u, (bb, h) in enumerate(units):
            o = os_[u]
            oms = jnp.mean(o * o, axis=-1, keepdims=True)
            zh = z_ref[bb, pl.ds(t0, c), h * hd:(h + 1) * hd].astype(F32)
            y = o * lax.rsqrt(oms + EPS) * ng_ref[...] * _silu(zh)
            o_ref[bb, pl.ds(t0, c), h * hd:(h + 1) * hd] = y.astype(o_ref.dtype)
        return carry

    lax.fori_loop(0, ts // c, chunk, 0)
    halo_ref[...] = qkv_ref[:, ts - hblk:ts, :]


def _gdn(proj3, gate3, conv_w, alog_b, dtb_b, norm_g, w_out, w_up, w_down, w_in_t, layer, ts=256, nb=GDN_BATCH):
    b, s, _ = proj3.shape
    qkv_w = 3 * GDN_WIDTH
    nt = s // ts
    steps = (b // nb) * nt
    chunks = ts // GDN_CHUNK
    weights = (w_out, w_up, w_down)
    slab = [(w.shape[1] // steps, w.shape[2]) for w in weights]
    assert all(w.shape[1] % steps == 0 and rows % (2 * SUBLANES * chunks) == 0
               for w, (rows, _) in zip(weights, slab))
    cast_in = [pl.BlockSpec((None,) + sl, lambda i, t: (layer, i * nt + t, 0)) for sl in slab]
    cast_out = [pl.BlockSpec(sl, lambda i, t: (i * nt + t, 0)) for sl in slab]
    cast_shape = [jax.ShapeDtypeStruct(w.shape[1:], BF16) for w in weights]
    cast_args = list(weights)
    if layer + 1 < w_in_t.shape[0]:
        n_blk = D_IN_MAIN // REGROUP_BLOCK
        assert n_blk <= steps

        def dst_block(i, t):
            return jnp.minimum(i * nt + t, n_blk - 1)

        cast_in.append(pl.BlockSpec((None, REGROUP_BLOCK, D_MODEL),
                                    lambda i, t: (layer + 1, _regroup_src_block(dst_block(i, t)), 0)))
        cast_out.append(pl.BlockSpec((REGROUP_BLOCK, D_MODEL), lambda i, t: (dst_block(i, t), 0)))
        cast_shape.append(jax.ShapeDtypeStruct((D_IN_MAIN, D_MODEL), BF16))
        cast_args.append(w_in_t)
    return pl.pallas_call(
        functools.partial(_gdn_kernel, len(cast_args)),
        grid=(b // nb, nt),
        in_specs=[
            pl.BlockSpec((nb, ts, qkv_w), lambda i, t: (i, t, OFF_GQKV // qkv_w)),
            pl.BlockSpec((nb, ts, GDN_WIDTH), lambda i, t: (i, t, OFF_GZ // GDN_WIDTH)),
            pl.BlockSpec((nb, ts, GATE_PAD), lambda i, t: (i, t, 0)),
            pl.BlockSpec((GDN_CONV_K, qkv_w), lambda i, t: (0, 0)),
            pl.BlockSpec((1, LANES), lambda i, t: (0, 0)),
            pl.BlockSpec((1, LANES), lambda i, t: (0, 0)),
            pl.BlockSpec((1, GDN_HEAD_DIM), lambda i, t: (0, 0)),
        ] + cast_in,
        out_specs=[pl.BlockSpec((nb, ts, GDN_WIDTH), lambda i, t: (i, t, 0))] + cast_out,
        out_shape=[jax.ShapeDtypeStruct((b, s, GDN_WIDTH), BF16)] + cast_shape,
        scratch_shapes=[pltpu.VMEM((nb, 2 * SUBLANES, qkv_w), BF16),
                        pltpu.VMEM((nb, GDN_HEADS, GDN_HEAD_DIM, GDN_HEAD_DIM), F32),
                        pltpu.VMEM((nb * qkv_w // GDN_HEAD_DIM, 2 * SUBLANES + GDN_CHUNK, GDN_HEAD_DIM), F32)],
        compiler_params=pltpu.CompilerParams(
            dimension_semantics=("parallel", "arbitrary"), vmem_limit_bytes=VMEM_LIMIT),
        name="gdn",
    )(proj3, proj3, gate3, conv_w, alog_b, dtb_b, norm_g, *cast_args)


def _out_proj_kernel(x_ref, yc_ref, ya_ref, yg_ref, w_ref, g_ref, x1_ref, h_ref):
    acc = _dot(yc_ref[...], w_ref[0:CONV_WIDTH, :])
    acc = acc + _dot(ya_ref[...], w_ref[CONV_WIDTH:CONV_WIDTH + ATTN_WIDTH, :])
    acc = acc + _dot(yg_ref[...], w_ref[CONV_WIDTH + ATTN_WIDTH:, :])
    x1 = x_ref[...] + acc
    x1_ref[...] = x1
    ms = jnp.mean(x1 * x1, axis=-1, keepdims=True)
    h_ref[...] = (x1 * lax.rsqrt(ms + EPS) * g_ref[...]).astype(BF16)


def _out_proj(x2, yc, ya, yg, w, g, tm=512):
    m = x2.shape[0]
    return pl.pallas_call(
        _out_proj_kernel,
        grid=(m // tm,),
        in_specs=[
            pl.BlockSpec((tm, D_MODEL), lambda i: (i, 0)),
            pl.BlockSpec((tm, CONV_WIDTH), lambda i: (i, 0)),
            pl.BlockSpec((tm, ATTN_WIDTH), lambda i: (i, 0)),
            pl.BlockSpec((tm, GDN_WIDTH), lambda i: (i, 0)),
            pl.BlockSpec((D_MODEL, D_MODEL), lambda i: (0, 0)),
            pl.BlockSpec((1, D_MODEL), lambda i: (0, 0)),
        ],
        out_specs=[pl.BlockSpec((tm, D_MODEL), lambda i: (i, 0)),
                   pl.BlockSpec((tm, D_MODEL), lambda i: (i, 0))],
        out_shape=[jax.ShapeDtypeStruct((m, D_MODEL), F32),
                   jax.ShapeDtypeStruct((m, D_MODEL), BF16)],
        compiler_params=pltpu.CompilerParams(
            dimension_semantics=("parallel",), vmem_limit_bytes=VMEM_LIMIT),
        name="out_proj",
    )(x2, yc, ya, yg, w, g)


def _mlp_kernel(x1_ref, h_ref, wu_ref, wd_ref, o_ref):
    @pl.when(pl.program_id(1) == 0)
    def _():
        o_ref[...] = x1_ref[...]

    hid = jnp.maximum(_dot(h_ref[...], wu_ref[...]), 0.0)
    hid = (hid * hid).astype(BF16)
    o_ref[...] += _dot(hid, wd_ref[...])


def _mlp(x1, h, w_up, w_down, tm=512, tf=2048):
    m = x1.shape[0]
    return pl.pallas_call(
        _mlp_kernel,
        grid=(m // tm, D_FF // tf),
        in_specs=[
            pl.BlockSpec((tm, D_MODEL), lambda i, f: (i, 0)),
            pl.BlockSpec((tm, D_MODEL), lambda i, f: (i, 0)),
            pl.BlockSpec((D_MODEL, tf), lambda i, f: (0, f)),
            pl.BlockSpec((tf, D_MODEL), lambda i, f: (f, 0)),
        ],
        out_specs=pl.BlockSpec((tm, D_MODEL), lambda i, f: (i, 0)),
        out_shape=jax.ShapeDtypeStruct((m, D_MODEL), F32),
        compiler_params=pltpu.CompilerParams(
            dimension_semantics=("parallel", "arbitrary"), vmem_limit_bytes=VMEM_LIMIT),
        name="mlp",
    )(x1, h, w_up, w_down)


def _gate_w_in(w_in):
    gate = w_in[..., D_IN_MAIN:].astype(BF16)
    return jnp.pad(gate, ((0, 0), (0, 0), (0, GATE_PAD - gate.shape[-1])))


def _decay_lanes(p):
    return jnp.zeros((1, LANES), F32).at[0, GDN_HEADS:2 * GDN_HEADS].set(p)


def _layer(x2, b, s, layer, w_in_b, w_in_t, w_gate_all, w_out, w_up, w_down, norm1_g, conv_w, conv_out_g,
           q_norm_g, k_norm_g, attn_sinks, attn_out_g, gdn_conv_w, gdn_A_log, gdn_dt_bias, gdn_norm_g, norm2_g):
    proj, gates = _in_proj(x2, norm1_g[None, :], w_in_b, w_gate_all, layer)
    proj3 = proj.reshape(b, s, D_IN_MAIN)
    yc = _conv_mixer(proj3, conv_w, conv_out_g[None, :])
    ya = _swa(proj3, jnp.tile(q_norm_g, 2)[None, :], jnp.tile(k_norm_g, ATTN_KV_HEADS)[None, :], attn_sinks,
              attn_out_g[None, :])
    yg, w_out_b, w_up_b, w_down_b, *w_in_next = _gdn(
        proj3, gates.reshape(b, s, GATE_PAD), gdn_conv_w, _decay_lanes(gdn_A_log), _decay_lanes(gdn_dt_bias),
        gdn_norm_g[None, :], w_out, w_up, w_down, w_in_t, layer)
    m = b * s
    x1, h2 = _out_proj(x2, yc.reshape(m, CONV_WIDTH), ya.reshape(m, ATTN_WIDTH), yg.reshape(m, GDN_WIDTH),
                       w_out_b, norm2_g[None, :])
    return _mlp(x1, h2, w_up_b, w_down_b), (w_in_next[0] if w_in_next else None)


def kernel(x, norm1_g, w_in, conv_w, conv_out_g, q_norm_g, k_norm_g, attn_sinks, attn_out_g, gdn_conv_w,
           gdn_A_log, gdn_dt_bias, gdn_norm_g, w_out, norm2_g, w_up, w_down):
    b, s, d = x.shape
    x2 = x.reshape(b * s, d)
    w_in_t, w_gate_all = jnp.swapaxes(w_in, 1, 2), _gate_w_in(w_in)
    w_in_b = _regroup_w_in(w_in_t, 0)
    for l in range(norm1_g.shape[0]):
        x2, w_in_b = _layer(x2, b, s, l, w_in_b, w_in_t, w_gate_all, w_out, w_up, w_down, norm1_g[l], conv_w[l],
                            conv_out_g[l], q_norm_g[l], k_norm_g[l], attn_sinks[l], attn_out_g[l], gdn_conv_w[l],
                            gdn_A_log[l], gdn_dt_bias[l], gdn_norm_g[l], norm2_g[l])
    return x2.reshape(b, s, d)
```
